```python
import math
import jax, jax.numpy as jnp
from jax import lax
import numpy as np

D_MODEL = 4096
BATCH = 4
SEQ = 2048
DEPTH = 2
DEC_BATCH = 8
DEC_SEQ = 8
PAST_LEN = 16384
PAGE_SIZE = 128

MIX = D_MODEL
D_SB = MIX // 4
SB_HEAD_DIM = 128
SB_HEADS = D_SB // SB_HEAD_DIM
SB_Q_BLOCK = 128
SB_BIAS_INIT = -8.0
D_SSM = MIX // 2
SSM_HEAD_DIM = 64
SSM_HEADS = D_SSM // SSM_HEAD_DIM
SSM_STATE = 128
SSM_GROUPS = 4
SSM_CONV = 4
SSM_CONV_DIM = D_SSM + 2 * SSM_GROUPS * SSM_STATE
SSM_CHUNK = 128
D_GLA = MIX - D_SB - D_SSM
GLA_HEADS = 8
GLA_V_DIM = D_GLA // GLA_HEADS
GLA_K_DIM = GLA_V_DIM // 2
GLA_RANK = 16
GLA_TAU = 16.0
GLA_CHUNK = 64
D_FF = ((8 * D_MODEL // 3 + 255) // 256) * 256
EPS = 1e-6
SPLIT_SIZES = (D_SB, D_SB, D_SB,
               D_SSM, SSM_CONV_DIM, SSM_HEADS,
               GLA_HEADS * GLA_K_DIM, GLA_HEADS * GLA_K_DIM,
               D_GLA, D_GLA, GLA_RANK)
N_IN = sum(SPLIT_SIZES)

kernel_name = "hybrid_sb_ssd_gla_decode_step"


def _rmsnorm(x, g):
    xf = x.astype(jnp.float32)
    y = xf * lax.rsqrt(jnp.mean(xf * xf, axis=-1, keepdims=True) + EPS)
    return (y * g.astype(jnp.float32)).astype(x.dtype)


def _rms(xf):
    return xf * lax.rsqrt(jnp.mean(xf * xf, axis=-1, keepdims=True) + EPS)


def _chunk_len(T, L):
    return L if T % L == 0 else T


def _to_chunks(a, L):
    B, T = a.shape[0], a.shape[1]
    return a.reshape((B, T // L, L) + a.shape[2:]).swapaxes(0, 1)


def _from_chunks(a):
    a = a.swapaxes(0, 1)
    return a.reshape((a.shape[0], a.shape[1] * a.shape[2]) + a.shape[3:])


def _stick_breaking(q, k, v, bias, q_offset):
    B, T, H, Dh = q.shape
    Tk = k.shape[1]
    qb = _chunk_len(T, SB_Q_BLOCK)
    scale = Dh ** -0.5
    kf = k.astype(jnp.float32)
    vf = v.astype(jnp.float32)
    bf = bias.astype(jnp.float32)[None, :, None, None]
    kpos = jnp.arange(Tk)
    q_blocks = _to_chunks(q.astype(jnp.float32), qb)
    qpos_blocks = (q_offset + jnp.arange(T)).reshape(T // qb, qb)

    def block(args):
        qblk, qpos = args
        z = jnp.einsum('bqhd,bkhd->bhqk', qblk, kf) * scale + bf
        mask = kpos[None, :] < qpos[:, None]
        log_1mb = jnp.where(mask, jax.nn.log_sigmoid(-z), 0.0)
        suffix = lax.cumsum(log_1mb, axis=3, reverse=True) - log_1mb
        w = jnp.where(mask, jnp.exp(jax.nn.log_sigmoid(z) + suffix), 0.0)
        return jnp.einsum('bhqk,bkhd->bqhd', w, vf)

    out = _from_chunks(lax.map(block, (q_blocks, qpos_blocks)))
    return out.reshape(B, T, H * Dh)


def _causal_conv(xbc, ctx, w, b):
    xp = jnp.concatenate([ctx.astype(xbc.dtype), xbc], axis=1)
    out = lax.conv_general_dilated(xp, w[:, None, :].astype(xbc.dtype), window_strides=(1,), padding='VALID',
                                   dimension_numbers=('NWC', 'WIO', 'NWC'),
                                   feature_group_count=xbc.shape[-1])
    return out + b.astype(xbc.dtype), xp[:, xp.shape[1] - (SSM_CONV - 1):]


def _ssd(x, dt, A, Bm, Cm, h0):
    Bsz, T, H, P = x.shape
    G, N = Bm.shape[2], Bm.shape[3]
    R = H // G
    L = _chunk_len(T, SSM_CHUNK)
    xs = _to_chunks((x * dt[..., None]).reshape(Bsz, T, G, R, P), L)
    a = _to_chunks((dt * A).reshape(Bsz, T, G, R), L)
    bc_all = _to_chunks(Bm, L)
    cc_all = _to_chunks(Cm, L)
    tri = jnp.tril(jnp.ones((L, L), dtype=bool))

    def step(h, inp):
        xc, ac, bc, cc = inp
        cum = jnp.cumsum(ac, axis=1)
        seg = cum[:, :, None] - cum[:, None, :]
        decay = jnp.exp(jnp.where(tri[None, :, :, None, None], seg, -jnp.inf))
        att = jnp.einsum('btgn,bsgn->btsg', cc, bc)[..., None] * decay
        y = jnp.einsum('btsgr,bsgrp->btgrp', att, xc)
        y = y + jnp.einsum('btgn,bgrpn->btgrp', cc, h) * jnp.exp(cum)[..., None]
        last = cum[:, -1]
        h = jnp.exp(last)[..., None, None] * h + jnp.einsum(
            'bsgn,bsgrp->bgrpn', bc, xc * jnp.exp(last[:, None] - cum)[..., None])
        return h, y

    hT, ys = lax.scan(step, h0.reshape(Bsz, G, R, P, N), (xs, a, bc_all, cc_all))
    return _from_chunks(ys).reshape(Bsz, T, H, P), hT.reshape(Bsz, H, P, N)


def _gla(q, k, v, log_a, S0):
    T = q.shape[1]
    L = _chunk_len(T, GLA_CHUNK)
    tri = jnp.tril(jnp.ones((L, L), dtype=bool))

    def step(S, inp):
        qc, kc, vc, gc = inp
        b = jnp.cumsum(gc, axis=1)
        qt = qc * jnp.exp(b)
        kt = kc * jnp.exp(-b)
        att = jnp.where(tri[None, None], jnp.einsum('bthk,bshk->bhts', qt, kt), 0.0)
        o = jnp.einsum('bhts,bshv->bthv', att, vc) + jnp.einsum('bthk,bhkv->bthv', qt, S)
        last = b[:, -1]
        S = jnp.exp(last)[..., None] * S + jnp.einsum(
            'bshk,bshv->bhkv', kc * jnp.exp(last[:, None] - b), vc)
        return S, o

    ST, os_ = lax.scan(step, S0, (_to_chunks(q, L), _to_chunks(k, L), _to_chunks(v, L), _to_chunks(log_a, L)))
    return _from_chunks(os_), ST


def _mixer(h, w_in, sb_bias, conv_w, conv_b, dt_bias, a_log, d_skip, ssm_norm_g, gla_gk_w, gla_gk_b, gla_norm_g,
           w_out, past_k, past_v, conv_ctx, ssm_h0, gla_s0):
    f32 = jnp.float32
    B, T, _ = h.shape
    split_points = np.cumsum(SPLIT_SIZES)[:-1].tolist()
    (q_sb, k_sb, v_sb, z, xbc, dt_raw, q_gl, k_gl, v_gl, r_gl, gk_low) = jnp.split(h @ w_in, split_points, axis=-1)

    q_sb = q_sb.reshape(B, T, SB_HEADS, SB_HEAD_DIM)
    k_sb = k_sb.reshape(B, T, SB_HEADS, SB_HEAD_DIM)
    v_sb = v_sb.reshape(B, T, SB_HEADS, SB_HEAD_DIM)
    if past_k is None:
        keys, vals, q_off = k_sb, v_sb, 0
    else:
        keys = jnp.concatenate([past_k.astype(k_sb.dtype), k_sb], axis=1)
        vals = jnp.concatenate([past_v.astype(v_sb.dtype), v_sb], axis=1)
        q_off = past_k.shape[1]
    o_sb = _stick_breaking(q_sb, keys, vals, sb_bias, q_off)

    xbc_c, conv_state = _causal_conv(xbc, conv_ctx, conv_w, conv_b)
    xbc_c = jax.nn.silu(xbc_c.astype(f32))
    x_ssm, b_ssm, c_ssm = jnp.split(xbc_c, [D_SSM, D_SSM + SSM_GROUPS * SSM_STATE], axis=-1)
    x_ssm = x_ssm.reshape(B, T, SSM_HEADS, SSM_HEAD_DIM)
    dt = jax.nn.softplus(dt_raw.astype(f32) + dt_bias.astype(f32))
    y_ssm, ssm_state = _ssd(x_ssm, dt, -jnp.exp(a_log.astype(f32)),
                            b_ssm.reshape(B, T, SSM_GROUPS, SSM_STATE),
                            c_ssm.reshape(B, T, SSM_GROUPS, SSM_STATE), ssm_h0.astype(f32))
    y_ssm = (y_ssm + d_skip.astype(f32)[:, None] * x_ssm).reshape(B, T, D_SSM) * jax.nn.silu(z.astype(f32))
    y_ssm = _rms(y_ssm.reshape(B, T, SSM_GROUPS, D_SSM // SSM_GROUPS)).reshape(B, T, D_SSM) * ssm_norm_g.astype(f32)

    log_a = jax.nn.log_sigmoid(gk_low.astype(f32) @ gla_gk_w.astype(f32) + gla_gk_b.astype(f32)) / GLA_TAU
    o_gl, gla_state = _gla(q_gl.reshape(B, T, GLA_HEADS, GLA_K_DIM).astype(f32) * GLA_K_DIM ** -0.5,
                           k_gl.reshape(B, T, GLA_HEADS, GLA_K_DIM).astype(f32),
                           v_gl.reshape(B, T, GLA_HEADS, GLA_V_DIM).astype(f32),
                           log_a.reshape(B, T, GLA_HEADS, GLA_K_DIM), gla_s0.astype(f32))
    o_gl = (_rms(o_gl) * gla_norm_g.astype(f32)).reshape(B, T, D_GLA) * jax.nn.silu(r_gl.astype(f32))

    mixed = jnp.concatenate([o_sb, y_ssm, o_gl], axis=-1).astype(h.dtype)
    return mixed @ w_out, (k_sb, v_sb, conv_state, ssm_state.astype(h.dtype), gla_state.astype(h.dtype))


def _layer(x, norm1_g, w_in, sb_bias, conv_w, conv_b, dt_bias, a_log, d_skip, ssm_norm_g, gla_gk_w, gla_gk_b,
           gla_norm_g, w_out, norm2_g, w_gate, w_up, w_down, past_k, past_v, conv_ctx, ssm_h0, gla_s0):
    mix, state = _mixer(_rmsnorm(x, norm1_g), w_in, sb_bias, conv_w, conv_b, dt_bias, a_log, d_skip, ssm_norm_g,
                        gla_gk_w, gla_gk_b, gla_norm_g, w_out, past_k, past_v, conv_ctx, ssm_h0, gla_s0)
    x = x + mix
    h = _rmsnorm(x, norm2_g)
    x = x + (jax.nn.silu(h @ w_gate) * (h @ w_up)) @ w_down
    return x, state


def _gather_pages(cache_l, page_table):
    g = cache_l[page_table]
    return g.reshape(g.shape[0], g.shape[1] * g.shape[2], g.shape[3], g.shape[4])


def setup_inputs(seed: int = 0) -> dict:
    key = jax.random.key(seed)
    ks = jax.random.split(key, 26)
    f32 = jnp.float32

    def nrm(k, shape, scale):
        return jax.random.normal(k, shape, f32) * scale

    n_pages = PAST_LEN // PAGE_SIZE
    n_used = DEC_BATCH * n_pages
    n_pool = n_used + n_used // 4
    page_table = jax.random.permutation(ks[0], n_pool)[:n_used].reshape(DEC_BATCH, n_pages).astype(jnp.int32)
    dt0 = jnp.exp(jax.random.uniform(ks[12], (DEPTH, SSM_HEADS), f32, math.log(1e-3), math.log(1e-1)))
    return {
        "x_prompt": nrm(ks[1], (BATCH, SEQ, D_MODEL), 1.0),
        "x_sample": nrm(ks[2], (DEC_BATCH, DEC_SEQ, D_MODEL), 1.0),
        "cache_k": nrm(ks[3], (DEPTH, n_pool, PAGE_SIZE, SB_HEADS, SB_HEAD_DIM), 1.0),
        "cache_v": nrm(ks[4], (DEPTH, n_pool, PAGE_SIZE, SB_HEADS, SB_HEAD_DIM), 1.0),
        "page_table": page_table,
        "state_conv": nrm(ks[5], (DEPTH, DEC_BATCH, SSM_CONV - 1, SSM_CONV_DIM), 1.0),
        "state_ssm": nrm(ks[6], (DEPTH, DEC_BATCH, SSM_HEADS, SSM_HEAD_DIM, SSM_STATE), 0.1),
        "state_gla": nrm(ks[7], (DEPTH, DEC_BATCH, GLA_HEADS, GLA_K_DIM, GLA_V_DIM), 2.0),
        "norm1_g": 1.0 + nrm(ks[8], (DEPTH, D_MODEL), 0.02),
        "w_in": nrm(ks[9], (DEPTH, D_MODEL, N_IN), D_MODEL ** -0.5),
        "sb_bias": SB_BIAS_INIT + nrm(ks[25], (DEPTH, SB_HEADS), 0.5),
        "conv_w": nrm(ks[10], (DEPTH, SSM_CONV, SSM_CONV_DIM), SSM_CONV ** -0.5),
        "conv_b": nrm(ks[11], (DEPTH, SSM_CONV_DIM), 0.01),
        "dt_bias": dt0 + jnp.log(-jnp.expm1(-dt0)),
        "a_log": jnp.log(jax.random.uniform(ks[13], (DEPTH, SSM_HEADS), f32, 1.0, 16.0)),
        "d_skip": 1.0 + nrm(ks[14], (DEPTH, SSM_HEADS), 0.02),
        "ssm_norm_g": 1.0 + nrm(ks[15], (DEPTH, D_SSM), 0.02),
        "gla_gk_w": nrm(ks[16], (DEPTH, GLA_RANK, GLA_HEADS * GLA_K_DIM), GLA_RANK ** -0.5),
        "gla_gk_b": nrm(ks[17], (DEPTH, GLA_HEADS * GLA_K_DIM), 0.01),
        "gla_norm_g": 1.0 + nrm(ks[18], (DEPTH, GLA_V_DIM), 0.02),
        "w_out": nrm(ks[19], (DEPTH, MIX, D_MODEL), MIX ** -0.5),
        "norm2_g": 1.0 + nrm(ks[20], (DEPTH, D_MODEL), 0.02),
        "w_gate": nrm(ks[21], (DEPTH, D_MODEL, D_FF), D_MODEL ** -0.5),
        "w_up": nrm(ks[22], (DEPTH, D_MODEL, D_FF), D_MODEL ** -0.5),
        "w_down": nrm(ks[23], (DEPTH, D_FF, D_MODEL), D_FF ** -0.5),
        "final_norm_g": 1.0 + nrm(ks[24], (D_MODEL,), 0.02),
    }


def reference(x_prompt, x_sample, cache_k, cache_v, page_table, state_conv, state_ssm, state_gla,
              norm1_g, w_in, sb_bias, conv_w, conv_b, dt_bias, a_log, d_skip, ssm_norm_g, gla_gk_w, gla_gk_b,
              gla_norm_g, w_out, norm2_g, w_gate, w_up, w_down, final_norm_g):
    def weights(l):
        return (norm1_g[l], w_in[l], sb_bias[l], conv_w[l], conv_b[l], dt_bias[l], a_log[l], d_skip[l],
                ssm_norm_g[l], gla_gk_w[l], gla_gk_b[l], gla_norm_g[l], w_out[l], norm2_g[l], w_gate[l],
                w_up[l], w_down[l])

    bp = x_prompt.shape[0]
    hp = x_prompt
    prompt_states = []
    for l in range(DEPTH):
        hp, st = _layer(hp, *weights(l), None, None,
                        jnp.zeros((bp, SSM_CONV - 1, SSM_CONV_DIM), x_prompt.dtype),
                        jnp.zeros((bp, SSM_HEADS, SSM_HEAD_DIM, SSM_STATE), jnp.float32),
                        jnp.zeros((bp, GLA_HEADS, GLA_K_DIM, GLA_V_DIM), jnp.float32))
        prompt_states.append(st)
    y_prompt = _rmsnorm(hp, final_norm_g)

    hs = x_sample
    sample_states = []
    for l in range(DEPTH):
        hs, st = _layer(hs, *weights(l), _gather_pages(cache_k[l], page_table),
                        _gather_pages(cache_v[l], page_table), state_conv[l], state_ssm[l], state_gla[l])
        sample_states.append(st)
    y_sample = _rmsnorm(hs, final_norm_g)

    k_rows_prompt = jnp.stack([s[0] for s in prompt_states])
    v_rows_prompt = jnp.stack([s[1] for s in prompt_states])
    conv_prompt = jnp.stack([s[2] for s in prompt_states])
    ssm_prompt = jnp.stack([s[3] for s in prompt_states])
    gla_prompt = jnp.stack([s[4] for s in prompt_states])
    k_rows_sample = jnp.stack([s[0] for s in sample_states])
    v_rows_sample = jnp.stack([s[1] for s in sample_states])
    conv_sample = jnp.stack([s[2] for s in sample_states])
    ssm_sample = jnp.stack([s[3] for s in sample_states])
    gla_sample = jnp.stack([s[4] for s in sample_states])
    return (y_prompt, y_sample, k_rows_prompt, v_rows_prompt, conv_prompt, ssm_prompt, gla_prompt,
            k_rows_sample, v_rows_sample, conv_sample, ssm_sample, gla_sample)
```

```python
import functools

import jax
import jax.numpy as jnp
from jax import lax
from jax.experimental import pallas as pl
from jax.experimental.pallas import tpu as pltpu

F32 = jnp.float32
BF16 = jnp.bfloat16

EPS = 1e-6
SB_HEADS = 8
SB_HEAD_DIM = 128
D_SB = SB_HEADS * SB_HEAD_DIM
SSM_HEADS = 32
SSM_HEAD_DIM = 64
D_SSM = SSM_HEADS * SSM_HEAD_DIM
SSM_STATE = 128
SSM_GROUPS = 4
SSM_CONV = 4
SSM_CONV_DIM = D_SSM + 2 * SSM_GROUPS * SSM_STATE
GLA_HEADS = 8
GLA_K_DIM = 64
GLA_V_DIM = 128
D_GLA = GLA_HEADS * GLA_V_DIM
GLA_RANK = 16
GLA_TAU = 16.0
GLA_CHUNK = 64
PAGE_SIZE = 128

LANES = 128
TILE = 128
VMEM_LIMIT = 60 * 1024 * 1024


def _cparams(sem):
    return pltpu.CompilerParams(dimension_semantics=sem, vmem_limit_bytes=VMEM_LIMIT)


def _softplus(z):
    return jnp.maximum(z, 0.0) + jnp.log1p(jnp.exp(-jnp.abs(z)))


def _silu(x):
    return x * (1.0 / (1.0 + jnp.exp(-x)))


def _split3(d):
    d0 = d.astype(BF16)
    r1 = d - d0.astype(F32)
    d1 = r1.astype(BF16)
    d2 = (r1 - d1.astype(F32)).astype(BF16)
    return d0, d1, d2


def _dot(a, b):
    return jnp.dot(a, b, preferred_element_type=F32)


def _dot_nt(a, b):
    return lax.dot_general(a, b, (((1,), (1,)), ((), ())), preferred_element_type=F32)


def _sel_dot_r(data, sel):
    d0, d1, d2 = _split3(data)
    return _dot(d0, sel) + _dot(d1, sel) + _dot(d2, sel)


def _sel_dot_l(sel, data):
    d0, d1, d2 = _split3(data)
    return _dot(sel, d0) + _dot(sel, d1) + _dot(sel, d2)


def _iota(shape, dim):
    return lax.broadcasted_iota(jnp.int32, shape, dim)


def _rmsnorm_kernel(x_ref, g_ref, o_ref):
    x = x_ref[...]
    y = x * lax.rsqrt(jnp.mean(x * x, axis=-1, keepdims=True) + EPS)
    o_ref[...] = (y * g_ref[...]).astype(o_ref.dtype)


def _rmsnorm(x, g, out_dtype, tm):
    m, d = x.shape
    return pl.pallas_call(
        _rmsnorm_kernel,
        grid=(m // tm,),
        in_specs=[pl.BlockSpec((tm, d), lambda i: (i, 0)), pl.BlockSpec((1, d), lambda i: (0, 0))],
        out_specs=pl.BlockSpec((tm, d), lambda i: (i, 0)),
        out_shape=jax.ShapeDtypeStruct((m, d), out_dtype),
        compiler_params=_cparams(("arbitrary",)),
    )(x, g.reshape(1, d))


def _mm_kernel(x_ref, w_ref, *rest, has_res):
    if has_res:
        res_ref, o_ref, wbf_ref = rest
    else:
        o_ref, wbf_ref = rest

    @pl.when(pl.program_id(1) == 0)
    def _():
        wbf_ref[...] = w_ref[...].astype(BF16)

    acc = _dot(x_ref[...], wbf_ref[...])
    if has_res:
        acc = acc + res_ref[...]
    o_ref[...] = acc.astype(o_ref.dtype)


def _matmul(x, w, res, tm, tn):
    m, k = x.shape
    n = w.shape[1]
    in_specs = [pl.BlockSpec((tm, k), lambda j, i: (i, 0)), pl.BlockSpec((k, tn), lambda j, i: (0, j))]
    args = [x, w]
    if res is not None:
        in_specs.append(pl.BlockSpec((tm, tn), lambda j, i: (i, j)))
        args.append(res)
    return pl.pallas_call(
        functools.partial(_mm_kernel, has_res=res is not None),
        grid=(pl.cdiv(n, tn), m // tm),
        in_specs=in_specs,
        out_specs=pl.BlockSpec((tm, tn), lambda j, i: (i, j)),
        out_shape=jax.ShapeDtypeStruct((m, n), F32),
        scratch_shapes=[pltpu.VMEM((k, tn), BF16)],
        compiler_params=_cparams(("arbitrary", "arbitrary")),
    )(*args)


def _gate_up_kernel(x_ref, wg_ref, wu_ref, o_ref, wbf_ref, *, tn):
    @pl.when(pl.program_id(1) == 0)
    def _():
        wbf_ref[:, :tn] = wg_ref[...].astype(BF16)
        wbf_ref[:, tn:] = wu_ref[...].astype(BF16)

    gu = _dot(x_ref[...], wbf_ref[...])
    o_ref[...] = (_silu(gu[:, :tn]) * gu[:, tn:]).astype(o_ref.dtype)


def _gate_up(x, wg, wu, tm, tn):
    m, k = x.shape
    f = wg.shape[1]
    return pl.pallas_call(
        functools.partial(_gate_up_kernel, tn=tn),
        grid=(f // tn, m // tm),
        in_specs=[pl.BlockSpec((tm, k), lambda j, i: (i, 0)),
                  pl.BlockSpec((k, tn), lambda j, i: (0, j)),
                  pl.BlockSpec((k, tn), lambda j, i: (0, j))],
        out_specs=pl.BlockSpec((tm, tn), lambda j, i: (i, j)),
        out_shape=jax.ShapeDtypeStruct((m, f), BF16),
        scratch_shapes=[pltpu.VMEM((k, 2 * tn), BF16)],
        compiler_params=_cparams(("arbitrary", "arbitrary")),
    )(x, wg, wu)


def _mm_ksplit_kernel(x_ref, w_ref, res_ref, o_ref, acc_ref):
    kk = pl.program_id(2)

    @pl.when(kk == 0)
    def _():
        acc_ref[...] = res_ref[...]

    acc_ref[...] += _dot(x_ref[...], w_ref[...].astype(BF16))

    @pl.when(kk == pl.num_programs(2) - 1)
    def _():
        o_ref[...] = acc_ref[...]


def _matmul_ksplit(x, w, res, tm, tn, tk):
    m, k = x.shape
    n = w.shape[1]
    return pl.pallas_call(
        _mm_ksplit_kernel,
        grid=(n // tn, m // tm, k // tk),
        in_specs=[pl.BlockSpec((tm, tk), lambda j, i, kk: (i, kk)),
                  pl.BlockSpec((tk, tn), lambda j, i, kk: (kk, j)),
                  pl.BlockSpec((tm, tn), lambda j, i, kk: (i, j))],
        out_specs=pl.BlockSpec((tm, tn), lambda j, i, kk: (i, j)),
        out_shape=jax.ShapeDtypeStruct((m, n), F32),
        scratch_shapes=[pltpu.VMEM((tm, tn), F32)],
        compiler_params=_cparams(("arbitrary", "arbitrary", "arbitrary")),
    )(x, w, res)


def _sb_prompt_kernel(bias_ref, q_ref, k_ref, v_ref, o_ref, *, tq, tk, scale):
    h = pl.program_id(1)
    i = pl.program_id(2)
    q = (q_ref[...] * scale).astype(BF16)
    bias = bias_ref[h]
    qpos = i * tq + _iota((tq, tk), 0)
    col = _iota((tq, tk), 1)
    later = (_iota((tk, tk), 0) > _iota((tk, tk), 1)).astype(BF16)
    nkb = ((i + 1) * tq) // tk

    def body(jj, carry):
        acc, run = carry
        j = nkb - 1 - jj
        start = pl.multiple_of(j * tk, tk)
        kb = k_ref[pl.ds(start, tk), :].astype(BF16)
        vb = v_ref[pl.ds(start, tk), :].astype(BF16)
        z = _dot_nt(q, kb) + bias
        mask = (start + col) < qpos
        sp = _softplus(z)
        lg = jnp.where(mask, -sp, 0.0)
        hi = lg.astype(BF16)
        lo = (lg - hi.astype(F32)).astype(BF16)
        suffix = _dot(hi, later) + _dot(lo, later) + run
        w = jnp.where(mask, jnp.exp(z - sp + suffix), 0.0)
        acc = acc + _dot(w.astype(BF16), vb)
        run = run + jnp.sum(lg, axis=1, keepdims=True)
        return acc, run

    acc, _ = lax.fori_loop(0, nkb, body, (jnp.zeros((tq, SB_HEAD_DIM), F32), jnp.zeros((tq, 1), F32)))
    o_ref[...] = acc.astype(o_ref.dtype)


def _sb_prompt(proj, bias, nb, t, tq=128, tk=128):
    nq = t // tq
    hb = SB_HEADS
    return pl.pallas_call(
        functools.partial(_sb_prompt_kernel, tq=tq, tk=tk, scale=SB_HEAD_DIM ** -0.5),
        grid=(nb, hb, nq),
        in_specs=[pl.BlockSpec(memory_space=pltpu.SMEM),
                  pl.BlockSpec((tq, SB_HEAD_DIM), lambda b, h, i: (b * nq + i, h)),
                  pl.BlockSpec((t, SB_HEAD_DIM), lambda b, h, i: (b, hb + h)),
                  pl.BlockSpec((t, SB_HEAD_DIM), lambda b, h, i: (b, 2 * hb + h))],
        out_specs=pl.BlockSpec((tq, SB_HEAD_DIM), lambda b, h, i: (b * nq + i, h)),
        out_shape=jax.ShapeDtypeStruct((nb * t, D_SB), BF16),
        compiler_params=_cparams(("arbitrary", "arbitrary", "arbitrary")),
    )(bias, proj, proj, proj)


def _sb_sample_kernel(pt_ref, qbd_ref, bias_ref, kown_ref, vown_ref, *refs, pp, nq):
    del pt_ref
    kp = refs[:pp]
    vp = refs[pp:2 * pp]
    o_ref = refs[2 * pp]
    acc_ref, run_ref = refs[2 * pp + 1:]
    s = pl.program_id(1)
    n = PAGE_SIZE
    hpc = 256 // SB_HEAD_DIM
    rows_c = hpc * nq
    nchunks = SB_HEADS // hpc
    earlier = (_iota((n, n), 1) > _iota((n, n), 0)).astype(BF16)
    qbd = qbd_ref[...]
    bias = bias_ref[...]

    def process(k_ref, v_ref, own):
        z = _dot(k_ref[...].astype(BF16), qbd) + bias
        sp = _softplus(z)
        if own:
            mask = _iota((n, LANES), 0) < (_iota((n, LANES), 1) % nq)
            lg = jnp.where(mask, -sp, 0.0)
        else:
            lg = -sp
        hi = lg.astype(BF16)
        lo = (lg - hi.astype(F32)).astype(BF16)
        suffix = _dot(earlier, hi) + _dot(earlier, lo) + run_ref[...]
        w = jnp.exp(z - sp + suffix)
        if own:
            w = jnp.where(mask, w, 0.0)
        run_ref[...] += jnp.sum(lg, axis=0, keepdims=True)
        wt = w.T.astype(BF16)
        vb = v_ref[...].astype(BF16)
        for c in range(nchunks):
            acc_ref[c] += _dot(wt[c * rows_c:(c + 1) * rows_c, :], vb[:, c * 256:(c + 1) * 256])

    @pl.when(s == 0)
    def _():
        acc_ref[...] = jnp.zeros_like(acc_ref)
        run_ref[...] = jnp.zeros_like(run_ref)
        process(kown_ref, vown_ref, True)

    for c in range(pp):
        process(kp[c], vp[c], False)

    @pl.when(s == pl.num_programs(1) - 1)
    def _():
        for hh in range(SB_HEADS):
            c, e = divmod(hh, hpc)
            o_ref[:, hh * SB_HEAD_DIM:(hh + 1) * SB_HEAD_DIM] = acc_ref[
                c, e * nq:(e + 1) * nq, e * SB_HEAD_DIM:(e + 1) * SB_HEAD_DIM]


def _sb_sample(q, k_own, v_own, bias, cache_k, cache_v, page_table, layer, pp=4):
    db, nq, _ = q.shape
    depth, n_pool = cache_k.shape[0], cache_k.shape[1]
    n_pages = page_table.shape[1]
    scale = SB_HEAD_DIM ** -0.5
    qh = (q * scale).reshape(db, nq, SB_HEADS, SB_HEAD_DIM)
    eye = jnp.eye(SB_HEADS, dtype=F32)
    qbd = jnp.einsum('bthd,hg->bhdgt', qh, eye).reshape(db, D_SB, SB_HEADS * nq)
    qbd = jnp.pad(qbd, ((0, 0), (0, 0), (0, LANES - SB_HEADS * nq))).astype(BF16)
    bias_l = jnp.pad(jnp.repeat(bias.astype(F32), nq), (0, LANES - SB_HEADS * nq)).reshape(1, LANES)
    pad = ((0, 0), (0, PAGE_SIZE - nq), (0, 0))
    kown = jnp.pad(k_own, pad)
    vown = jnp.pad(v_own, pad)
    ck = cache_k.reshape(depth * n_pool, PAGE_SIZE, D_SB)
    cv = cache_v.reshape(depth * n_pool, PAGE_SIZE, D_SB)
    base = layer * n_pool

    def page_map(c):
        return lambda b, s, pt: (base + pt[b, n_pages - 1 - (s * pp + c)], 0, 0)

    page_spec = [pl.BlockSpec((None, PAGE_SIZE, D_SB), page_map(c)) for c in range(pp)]
    grid_spec = pltpu.PrefetchScalarGridSpec(
        num_scalar_prefetch=1,
        grid=(db, n_pages // pp),
        in_specs=[pl.BlockSpec((None, D_SB, LANES), lambda b, s, pt: (b, 0, 0)),
                  pl.BlockSpec((1, LANES), lambda b, s, pt: (0, 0)),
                  pl.BlockSpec((None, PAGE_SIZE, D_SB), lambda b, s, pt: (b, 0, 0)),
                  pl.BlockSpec((None, PAGE_SIZE, D_SB), lambda b, s, pt: (b, 0, 0))] + page_spec + page_spec,
        out_specs=pl.BlockSpec((nq, D_SB), lambda b, s, pt: (b, 0)),
        scratch_shapes=[pltpu.VMEM((SB_HEADS * SB_HEAD_DIM // 256, 2 * nq, 256), F32),
                        pltpu.VMEM((1, LANES), F32)],
    )
    return pl.pallas_call(
        functools.partial(_sb_sample_kernel, pp=pp, nq=nq),
        grid_spec=grid_spec,
        out_shape=jax.ShapeDtypeStruct((db * nq, D_SB), F32),
        compiler_params=_cparams(("arbitrary", "arbitrary")),
    )(page_table, qbd, bias_l, kown, vown, *([ck] * pp), *([cv] * pp))


def _ssd_kernel(*refs, tb, has_init):
    if has_init:
        (z0_ref, z1_ref, xa_ref, xb_ref, bc_ref, dt_ref, ctx_ref, h0_ref, cw_ref, cb_ref, dtb_ref, alog_ref,
         dsk_ref, ng_ref, y_ref, hT_ref, xp_ref, hs_ref, yb_ref) = refs
    else:
        (z0_ref, z1_ref, xa_ref, xb_ref, bc_ref, dt_ref, ctx_ref, cw_ref, cb_ref, dtb_ref, alog_ref,
         dsk_ref, ng_ref, y_ref, hT_ref, xp_ref, hs_ref, yb_ref) = refs
        h0_ref = None
    c = pl.program_id(1)
    L = TILE
    P2 = LANES
    n_pairs = D_SSM // P2

    @pl.when(c == 0)
    def _():
        xp_ref[0:8, :] = ctx_ref[...]
        if has_init:
            hs_ref[...] = h0_ref[...]
        else:
            hs_ref[...] = jnp.zeros_like(hs_ref)

    xp_ref[8:8 + tb, 0:1024] = xa_ref[...]
    xp_ref[8:8 + tb, 1024:2048] = xb_ref[...]
    xp_ref[8:8 + tb, 2048:3072] = bc_ref[...]
    if tb < L:
        xp_ref[8 + tb:8 + L, :] = jnp.zeros((L - tb, SSM_CONV_DIM), F32)

    conv = cb_ref[...] + cw_ref[0:1, :] * xp_ref[5:5 + L, :]
    for i in range(1, SSM_CONV):
        conv = conv + cw_ref[i:i + 1, :] * xp_ref[5 + i:5 + i + L, :]
    xp_ref[0:8, :] = xp_ref[tb:tb + 8, :]
    xc = _silu(conv)

    row = _iota((L, LANES), 0)
    lane = _iota((L, LANES), 1)
    dt_raw = dt_ref[...]
    if tb < L:
        dt_raw = jnp.concatenate([dt_raw, jnp.zeros((L - tb, LANES), F32)], axis=0)
    valid = (lane < SSM_HEADS) & (row < tb)
    dtv = jnp.where(valid, _softplus(dt_raw + dtb_ref[...]), 0.0)
    a = dtv * (-jnp.exp(alog_ref[...]))
    lower = (_iota((L, L), 0) >= _iota((L, L), 1))
    lower_b = lower.astype(BF16)
    upper_b = (_iota((L, L), 0) <= _iota((L, L), 1)).astype(BF16)
    cum = _sel_dot_l(lower_b, a)
    cum_t = _sel_dot_r(a.T, upper_b)
    last = cum[L - 1:L, :]
    spread = (_iota((LANES, D_SSM), 0) == _iota((LANES, D_SSM), 1) // SSM_HEAD_DIM).astype(BF16)
    spread_t = (_iota((D_SSM, LANES), 0) // SSM_HEAD_DIM == _iota((D_SSM, LANES), 1)).astype(BF16)
    dt_x = _sel_dot_r(dtv, spread)
    ecum_x = _sel_dot_r(jnp.exp(cum), spread)
    wdec_x = _sel_dot_r(jnp.exp(last - cum), spread)
    sdec = _sel_dot_l(spread_t, jnp.broadcast_to(jnp.exp(cum_t[:, L - 1:L]), (LANES, LANES)))

    first_half = lane < SSM_HEAD_DIM
    pairs_per_group = n_pairs // SSM_GROUPS
    for g in range(SSM_GROUPS):
        bg = xc[:, D_SSM + g * SSM_STATE:D_SSM + (g + 1) * SSM_STATE]
        cg = xc[:, D_SSM + (SSM_GROUPS + g) * SSM_STATE:D_SSM + (SSM_GROUPS + g + 1) * SSM_STATE]
        bg_b = bg.astype(BF16)
        cg_b = cg.astype(BF16)
        cb = _dot_nt(cg_b, bg_b)
        for r in range(pairs_per_group):
            p = g * pairs_per_group + r
            sl = slice(p * P2, (p + 1) * P2)
            x_p = xc[:, sl]
            xs_p = x_p * dt_x[:, sl]
            xs_b = xs_p.astype(BF16)
            ys = []
            for e in range(2):
                hh = 2 * p + e
                seg = jnp.broadcast_to(cum[:, hh:hh + 1], (L, L)) - jnp.broadcast_to(cum_t[hh:hh + 1, :], (L, L))
                dec = jnp.where(lower, jnp.exp(jnp.minimum(seg, 0.0)), 0.0)
                ys.append(_dot((cb * dec).astype(BF16), xs_b))
            y_p = jnp.where(first_half, ys[0], ys[1])
            h_p = hs_ref[sl, :]
            y_p = y_p + _dot_nt(cg_b, h_p.astype(BF16)) * ecum_x[:, sl]
            xw = (xs_p * wdec_x[:, sl]).astype(BF16)
            hs_ref[sl, :] = sdec[sl, :] * h_p + _dot(xw.T, bg_b)
            yb_ref[:, sl] = y_p + dsk_ref[:, sl] * x_p

    zfull = jnp.concatenate([z0_ref[...], z1_ref[...]], axis=1)
    y = yb_ref[0:tb, :] * _silu(zfull)
    gw = D_SSM // SSM_GROUPS
    for g in range(SSM_GROUPS):
        yg = y[:, g * gw:(g + 1) * gw]
        yn = yg * lax.rsqrt(jnp.mean(yg * yg, axis=-1, keepdims=True) + EPS)
        y_ref[:, g * gw:(g + 1) * gw] = (yn * ng_ref[:, g * gw:(g + 1) * gw]).astype(y_ref.dtype)

    @pl.when(c == pl.num_programs(1) - 1)
    def _():
        hT_ref[...] = hs_ref[...]


def _ssd(proj, row0, nb, t, ctx8, h0, conv_w, conv_b, dt_bias, a_log, d_skip, norm_g, out_dtype):
    tb = min(t, TILE)
    nc = t // tb
    rb0 = row0 // tb
    has_init = h0 is not None

    def colblk(width, idx):
        return pl.BlockSpec((tb, width), lambda b, c: (rb0 + b * nc + c, idx))

    def full(shape):
        return pl.BlockSpec(shape, lambda b, c: (0,) * len(shape))

    in_specs = [colblk(1024, 3), colblk(1024, 4), colblk(1024, 5), colblk(1024, 6), colblk(1024, 7),
                colblk(LANES, 64),
                pl.BlockSpec((None, 8, SSM_CONV_DIM), lambda b, c: (b, 0, 0))]
    args = [proj] * 6 + [ctx8]
    if has_init:
        in_specs.append(pl.BlockSpec((None, D_SSM, SSM_STATE), lambda b, c: (b, 0, 0)))
        args.append(h0)
    padl = (0, LANES - SSM_HEADS)
    in_specs += [full((SSM_CONV, SSM_CONV_DIM)), full((1, SSM_CONV_DIM)), full((1, LANES)), full((1, LANES)),
                 full((1, D_SSM)), full((1, D_SSM))]
    args += [conv_w, conv_b.reshape(1, -1), jnp.pad(dt_bias, padl).reshape(1, LANES),
             jnp.pad(a_log, padl).reshape(1, LANES), jnp.repeat(d_skip, SSM_HEAD_DIM).reshape(1, D_SSM),
             norm_g.reshape(1, D_SSM)]
    return pl.pallas_call(
        functools.partial(_ssd_kernel, tb=tb, has_init=has_init),
        grid=(nb, nc),
        in_specs=in_specs,
        out_specs=[pl.BlockSpec((tb, D_SSM), lambda b, c: (b * nc + c, 0)),
                   pl.BlockSpec((None, D_SSM, SSM_STATE), lambda b, c: (b, 0, 0))],
        out_shape=[jax.ShapeDtypeStruct((nb * t, D_SSM), out_dtype),
                   jax.ShapeDtypeStruct((nb, D_SSM, SSM_STATE), F32)],
        scratch_shapes=[pltpu.VMEM((8 + TILE, SSM_CONV_DIM), F32),
                        pltpu.VMEM((D_SSM, SSM_STATE), F32),
                        pltpu.VMEM((TILE, D_SSM), F32)],
        compiler_params=_cparams(("arbitrary", "arbitrary")),
    )(*args)


def _gla_kernel(*refs, tb, has_init):
    if has_init:
        (q_ref, k_ref, v_ref, r_ref, gk_ref, s0_ref, gw_ref, gb_ref, ng_ref, o_ref, sT_ref, st_ref) = refs
    else:
        (q_ref, k_ref, v_ref, r_ref, gk_ref, gw_ref, gb_ref, ng_ref, o_ref, sT_ref, st_ref) = refs
        s0_ref = None
    c = pl.program_id(1)
    L = TILE
    CH = min(GLA_CHUNK, tb)
    n_sub = max(tb // CH, 1)
    KP = GLA_HEADS * GLA_K_DIM

    @pl.when(c == 0)
    def _():
        if has_init:
            st_ref[...] = s0_ref[...]
        else:
            st_ref[...] = jnp.zeros_like(st_ref)

    def padrows(x):
        if tb < L:
            return jnp.concatenate([x, jnp.zeros((L - tb, x.shape[1]), F32)], axis=0)
        return x

    row = _iota((L, L), 0)
    col = _iota((L, L), 1)
    same_chunk = (row // CH) == (col // CH)
    lower = (row >= col) & same_chunk
    lower_b = lower.astype(BF16)
    rvalid = _iota((L, KP), 0) < tb
    glog = _dot(gk_ref[...].astype(BF16), gw_ref[...].astype(BF16)) + gb_ref[...]
    glog = padrows(-_softplus(-glog) * (1.0 / GLA_TAU))
    glog = jnp.where(rvalid, glog, 0.0)
    q_all = padrows(q_ref[...]) * (GLA_K_DIM ** -0.5)
    k_all = jnp.where(rvalid, padrows(k_ref[...]), 0.0)
    v_all = padrows(v_ref[...])
    lane = _iota((L, LANES), 1)
    rowl = _iota((L, LANES), 0)

    for p in range(KP // LANES):
        sl = slice(p * LANES, (p + 1) * LANES)
        g_p = glog[:, sl]
        b_p = _sel_dot_l(lower_b, g_p)
        qt = q_all[:, sl] * jnp.exp(b_p)
        kt = (k_all[:, sl] * jnp.exp(-b_p)).astype(BF16)
        lasts = [b_p[min((j + 1) * CH, L) - 1:min((j + 1) * CH, L), :] for j in range(n_sub)]
        s_cur = st_ref[sl, :]
        s_list = [s_cur]
        kw_t = []
        for j in range(n_sub):
            in_j = (rowl // CH) == j
            kw = jnp.where(in_j, k_all[:, sl] * jnp.exp(lasts[j] - b_p), 0.0)
            kw_t.append(kw.T.astype(BF16))
        heads_out = []
        new_states = [[None, None] for _ in range(n_sub)]
        for e in range(2):
            hh = 2 * p + e
            own = (lane // GLA_K_DIM) == e
            qm = jnp.where(own, qt, 0.0).astype(BF16)
            att = jnp.where(lower, _dot_nt(qm, kt), 0.0)
            v_h = v_all[:, hh * GLA_V_DIM:(hh + 1) * GLA_V_DIM].astype(BF16)
            heads_out.append((qm, _dot(att.astype(BF16), v_h), v_h))
        decs = []
        for j in range(n_sub):
            dcol = jnp.exp(lasts[j]).reshape(1, LANES)
            decs.append(jnp.broadcast_to(dcol, (LANES, LANES)).T)
        for j in range(n_sub):
            upd0 = _dot(kw_t[j], heads_out[0][2])
            upd1 = _dot(kw_t[j], heads_out[1][2])
            upd = jnp.where(_iota((LANES, GLA_V_DIM), 0) < GLA_K_DIM, upd0, upd1)
            s_list.append(decs[j] * s_list[j] + upd)
        st_ref[sl, :] = s_list[n_sub]
        for e in range(2):
            hh = 2 * p + e
            qm, o_h, _ = heads_out[e]
            inter = _dot(qm, s_list[0].astype(BF16))
            for j in range(1, n_sub):
                inter = jnp.where((rowl // CH) == j, _dot(qm, s_list[j].astype(BF16)), inter)
            o_h = o_h + inter
            o_h = o_h[0:tb, :]
            on = o_h * lax.rsqrt(jnp.mean(o_h * o_h, axis=-1, keepdims=True) + EPS) * ng_ref[...]
            vs = slice(hh * GLA_V_DIM, (hh + 1) * GLA_V_DIM)
            o_ref[:, vs] = (on * _silu(r_ref[:, vs])).astype(o_ref.dtype)

    @pl.when(c == pl.num_programs(1) - 1)
    def _():
        sT_ref[...] = st_ref[...]


def _gla(q, k, v, r, gk_low, nb, t, s0, gk_w, gk_b, norm_g, out_dtype):
    tb = min(t, TILE)
    nc = t // tb
    has_init = s0 is not None
    KP = GLA_HEADS * GLA_K_DIM

    def rows(width):
        return pl.BlockSpec((tb, width), lambda b, c: (b * nc + c, 0))

    def full(shape):
        return pl.BlockSpec(shape, lambda b, c: (0,) * len(shape))

    in_specs = [rows(KP), rows(KP), rows(D_GLA), rows(D_GLA), rows(GLA_RANK)]
    args = [q, k, v, r, gk_low]
    if has_init:
        in_specs.append(pl.BlockSpec((None, KP, GLA_V_DIM), lambda b, c: (b, 0, 0)))
        args.append(s0)
    in_specs += [full((GLA_RANK, KP)), full((1, KP)), full((1, GLA_V_DIM))]
    args += [gk_w, gk_b.reshape(1, KP), norm_g.reshape(1, GLA_V_DIM)]
    return pl.pallas_call(
        functools.partial(_gla_kernel, tb=tb, has_init=has_init),
        grid=(nb, nc),
        in_specs=in_specs,
        out_specs=[rows(D_GLA), pl.BlockSpec((None, KP, GLA_V_DIM), lambda b, c: (b, 0, 0))],
        out_shape=[jax.ShapeDtypeStruct((nb * t, D_GLA), out_dtype),
                   jax.ShapeDtypeStruct((nb, KP, GLA_V_DIM), F32)],
        scratch_shapes=[pltpu.VMEM((KP, GLA_V_DIM), F32)],
        compiler_params=_cparams(("arbitrary", "arbitrary")),
    )(*args)


def kernel(x_prompt, x_sample, cache_k, cache_v, page_table, state_conv, state_ssm, state_gla, norm1_g, w_in, sb_bias, conv_w, conv_b, dt_bias, a_log, d_skip, ssm_norm_g, gla_gk_w, gla_gk_b, gla_norm_g, w_out, norm2_g, w_gate, w_up, w_down, final_norm_g):
    bp, tp, d = x_prompt.shape
    bs, ts, _ = x_sample.shape
    depth = w_in.shape[0]
    mp = bp * tp
    ms = bs * ts
    m = mp + ms
    tm = m // 6
    tm_s = m // 12
    d_ff = w_gate.shape[2]
    KP = GLA_HEADS * GLA_K_DIM
    off_gla = 3 * D_SB + D_SSM + SSM_CONV_DIM + SSM_HEADS

    x = jnp.concatenate([x_prompt.reshape(mp, d), x_sample.reshape(ms, d)], axis=0)
    outs = {k: [] for k in ("kp", "vp", "cp", "sp", "gp", "ks", "vs", "cs", "ss", "gs")}

    for l in range(depth):
        xn = _rmsnorm(x, norm1_g[l], BF16, tm_s)
        proj = _matmul(xn, w_in[l], None, tm, 512)

        k_rows = proj[:, D_SB:2 * D_SB]
        v_rows = proj[:, 2 * D_SB:3 * D_SB]
        xbc = proj[:, 3 * D_SB + D_SSM:3 * D_SB + D_SSM + SSM_CONV_DIM]
        gq = proj[:, off_gla:off_gla + KP]
        gk = proj[:, off_gla + KP:off_gla + 2 * KP]
        gv = proj[:, off_gla + 2 * KP:off_gla + 2 * KP + D_GLA]
        gr = proj[:, off_gla + 2 * KP + D_GLA:off_gla + 2 * KP + 2 * D_GLA]
        glow = proj[:, off_gla + 2 * KP + 2 * D_GLA:]

        o_sb_p = _sb_prompt(proj, sb_bias[l], bp, tp)
        ctx_p = jnp.zeros((bp, 8, SSM_CONV_DIM), F32)
        y_ssm_p, ssm_p = _ssd(proj, 0, bp, tp, ctx_p, None, conv_w[l], conv_b[l], dt_bias[l], a_log[l],
                              d_skip[l], ssm_norm_g[l], BF16)
        o_gl_p, gla_p = _gla(gq[:mp], gk[:mp], gv[:mp], gr[:mp], glow[:mp], bp, tp, None,
                             gla_gk_w[l], gla_gk_b[l], gla_norm_g[l], BF16)

        q_s = proj[mp:, 0:D_SB].reshape(bs, ts, D_SB)
        k_s = k_rows[mp:].reshape(bs, ts, D_SB)
        v_s = v_rows[mp:].reshape(bs, ts, D_SB)
        o_sb_s = _sb_sample(q_s, k_s, v_s, sb_bias[l], cache_k, cache_v, page_table, l)
        ctx_s = jnp.pad(state_conv[l], ((0, 0), (8 - (SSM_CONV - 1), 0), (0, 0)))
        y_ssm_s, ssm_s = _ssd(proj, mp, bs, ts, ctx_s, state_ssm[l].reshape(bs, D_SSM, SSM_STATE),
                              conv_w[l], conv_b[l], dt_bias[l], a_log[l], d_skip[l], ssm_norm_g[l], F32)
        o_gl_s, gla_s = _gla(gq[mp:], gk[mp:], gv[mp:], gr[mp:], glow[mp:], bs, ts,
                             state_gla[l].reshape(bs, KP, GLA_V_DIM),
                             gla_gk_w[l], gla_gk_b[l], gla_norm_g[l], F32)

        mixed = jnp.concatenate(
            [jnp.concatenate([o_sb_p, y_ssm_p, o_gl_p], axis=1),
             jnp.concatenate([o_sb_s, y_ssm_s, o_gl_s], axis=1).astype(BF16)], axis=0)
        x = _matmul(mixed, w_out[l], x, tm_s, 512)
        hn = _rmsnorm(x, norm2_g[l], BF16, tm_s)
        hmid = _gate_up(hn, w_gate[l], w_up[l], tm, 256)
        x = _matmul_ksplit(hmid, w_down[l], x, tm_s, 512, d_ff // 2)

        xbc_p = xbc[:mp].reshape(bp, tp, SSM_CONV_DIM)
        xbc_s = jnp.concatenate([state_conv[l], xbc[mp:].reshape(bs, ts, SSM_CONV_DIM)], axis=1)
        outs["kp"].append(k_rows[:mp].reshape(bp, tp, SB_HEADS, SB_HEAD_DIM))
        outs["vp"].append(v_rows[:mp].reshape(bp, tp, SB_HEADS, SB_HEAD_DIM))
        outs["cp"].append(xbc_p[:, tp - (SSM_CONV - 1):])
        outs["sp"].append(ssm_p.reshape(bp, SSM_HEADS, SSM_HEAD_DIM, SSM_STATE))
        outs["gp"].append(gla_p.reshape(bp, GLA_HEADS, GLA_K_DIM, GLA_V_DIM))
        outs["ks"].append(k_s.reshape(bs, ts, SB_HEADS, SB_HEAD_DIM))
        outs["vs"].append(v_s.reshape(bs, ts, SB_HEADS, SB_HEAD_DIM))
        outs["cs"].append(xbc_s[:, xbc_s.shape[1] - (SSM_CONV - 1):])
        outs["ss"].append(ssm_s.reshape(bs, SSM_HEADS, SSM_HEAD_DIM, SSM_STATE))
        outs["gs"].append(gla_s.reshape(bs, GLA_HEADS, GLA_K_DIM, GLA_V_DIM))

    y = _rmsnorm(x, final_norm_g, F32, tm_s)
    st = {k: jnp.stack(v) for k, v in outs.items()}
    return (y[:mp].reshape(bp, tp, d), y[mp:].reshape(bs, ts, d),
            st["kp"], st["vp"], st["cp"], st["sp"], st["gp"],
            st["ks"], st["vs"], st["cs"], st["ss"], st["gs"])
```

```python
import functools

import jax
import jax.numpy as jnp
from jax import lax
from jax.experimental import pallas as pl
from jax.experimental.pallas import tpu as pltpu

F32 = jnp.float32
BF16 = jnp.bfloat16

EPS = 1e-6
SB_HEADS = 8
SB_HEAD_DIM = 128
D_SB = SB_HEADS * SB_HEAD_DIM
SSM_HEADS = 32
SSM_HEAD_DIM = 64
D_SSM = SSM_HEADS * SSM_HEAD_DIM
SSM_STATE = 128
SSM_GROUPS = 4
SSM_CONV = 4
SSM_CONV_DIM = D_SSM + 2 * SSM_GROUPS * SSM_STATE
GLA_HEADS = 8
GLA_K_DIM = 64
GLA_V_DIM = 128
D_GLA = GLA_HEADS * GLA_V_DIM
GLA_RANK = 16
GLA_TAU = 16.0
GLA_CHUNK = 64
PAGE_SIZE = 128

LANES = 128
TILE = 128
VMEM_LIMIT = 60 * 1024 * 1024


def _cparams(sem):
    return pltpu.CompilerParams(dimension_semantics=sem, vmem_limit_bytes=VMEM_LIMIT)


def _softplus(z):
    return jnp.maximum(z, 0.0) + jnp.log1p(jnp.exp(-jnp.abs(z)))


def _silu(x):
    return x * (1.0 / (1.0 + jnp.exp(-x)))


def _split3(d):
    d0 = d.astype(BF16)
    r1 = d - d0.astype(F32)
    d1 = r1.astype(BF16)
    d2 = (r1 - d1.astype(F32)).astype(BF16)
    return d0, d1, d2


def _dot(a, b):
    return jnp.dot(a, b, preferred_element_type=F32)


def _dot_nt(a, b):
    return lax.dot_general(a, b, (((1,), (1,)), ((), ())), preferred_element_type=F32)


def _sel_dot_r(data, sel):
    d0, d1, d2 = _split3(data)
    return _dot(d0, sel) + _dot(d1, sel) + _dot(d2, sel)


def _sel_dot_l(sel, data):
    d0, d1, d2 = _split3(data)
    return _dot(sel, d0) + _dot(sel, d1) + _dot(sel, d2)


def _iota(shape, dim):
    return lax.broadcasted_iota(jnp.int32, shape, dim)


def _rmsnorm_kernel(x_ref, g_ref, o_ref):
    x = x_ref[...]
    y = x * lax.rsqrt(jnp.mean(x * x, axis=-1, keepdims=True) + EPS)
    o_ref[...] = (y * g_ref[...]).astype(o_ref.dtype)


def _row_tile(m, target):
    best = None
    for t in range(16, target + 1, 16):
        if m % t == 0:
            best = t
    assert best is not None, (m, target)
    return best


def _rmsnorm(x, g, out_dtype, tm, rows=None):
    m, d = x.shape
    m = m if rows is None else rows
    return pl.pallas_call(
        _rmsnorm_kernel,
        grid=(m // tm,),
        in_specs=[pl.BlockSpec((tm, d), lambda i: (i, 0)), pl.BlockSpec((1, d), lambda i: (0, 0))],
        out_specs=pl.BlockSpec((tm, d), lambda i: (i, 0)),
        out_shape=jax.ShapeDtypeStruct((m, d), out_dtype),
        compiler_params=_cparams(("arbitrary",)),
    )(x, g.reshape(1, d))


def _mm_kernel(x_ref, w_ref, *rest, has_res):
    if has_res:
        res_ref, o_ref, wbf_ref = rest
    else:
        o_ref, wbf_ref = rest

    @pl.when(pl.program_id(1) == 0)
    def _():
        wbf_ref[...] = w_ref[...].astype(BF16)

    acc = _dot(x_ref[...], wbf_ref[...])
    if has_res:
        acc = acc + res_ref[...]
    o_ref[...] = acc.astype(o_ref.dtype)


def _matmul(x, w, layer, res, tm, tn):
    m, k = x.shape
    n = w.shape[2]
    in_specs = [pl.BlockSpec((tm, k), lambda j, i: (i, 0)),
                pl.BlockSpec((None, k, tn), lambda j, i: (layer, 0, j))]
    args = [x, w]
    if res is not None:
        in_specs.append(pl.BlockSpec((tm, tn), lambda j, i: (i, j)))
        args.append(res)
    return pl.pallas_call(
        functools.partial(_mm_kernel, has_res=res is not None),
        grid=(pl.cdiv(n, tn), m // tm),
        in_specs=in_specs,
        out_specs=pl.BlockSpec((tm, tn), lambda j, i: (i, j)),
        out_shape=jax.ShapeDtypeStruct((m, n), F32),
        scratch_shapes=[pltpu.VMEM((k, tn), BF16)],
        compiler_params=_cparams(("arbitrary", "arbitrary")),
    )(*args)


def _gate_up_kernel(x_ref, wg_ref, wu_ref, o_ref, wbf_ref, *, tn):
    @pl.when(pl.program_id(1) == 0)
    def _():
        wbf_ref[:, :tn] = wg_ref[...].astype(BF16)
        wbf_ref[:, tn:] = wu_ref[...].astype(BF16)

    gu = _dot(x_ref[...], wbf_ref[...])
    o_ref[...] = (_silu(gu[:, :tn]) * gu[:, tn:]).astype(o_ref.dtype)


def _gate_up(x, wg, wu, layer, tm, tn):
    m, k = x.shape
    f = wg.shape[2]
    wspec = pl.BlockSpec((None, k, tn), lambda j, i: (layer, 0, j))
    return pl.pallas_call(
        functools.partial(_gate_up_kernel, tn=tn),
        grid=(f // tn, m // tm),
        in_specs=[pl.BlockSpec((tm, k), lambda j, i: (i, 0)), wspec, wspec],
        out_specs=pl.BlockSpec((tm, tn), lambda j, i: (i, j)),
        out_shape=jax.ShapeDtypeStruct((m, f), BF16),
        scratch_shapes=[pltpu.VMEM((k, 2 * tn), BF16)],
        compiler_params=_cparams(("arbitrary", "arbitrary")),
    )(x, wg, wu)


def _cast_kernel(w_ref, o_ref):
    o_ref[...] = w_ref[...].astype(o_ref.dtype)


def _cast_bf16(w, tr):
    nl, r, c = w.shape
    spec = pl.BlockSpec((None, tr, c), lambda l, i: (l, i, 0))
    return pl.pallas_call(
        _cast_kernel,
        grid=(nl, r // tr),
        in_specs=[spec],
        out_specs=spec,
        out_shape=jax.ShapeDtypeStruct(w.shape, BF16),
        compiler_params=_cparams(("arbitrary", "arbitrary")),
    )(w)


def _mm_res_kernel(x_ref, w_ref, res_ref, o_ref):
    o_ref[...] = _dot(x_ref[...], w_ref[...]) + res_ref[...]


def _matmul_bf16w(x, w, layer, res, tm, tn):
    m, k = x.shape
    n = w.shape[2]
    return pl.pallas_call(
        _mm_res_kernel,
        grid=(m // tm, n // tn),
        in_specs=[pl.BlockSpec((tm, k), lambda i, j: (i, 0), pipeline_mode=pl.Buffered(1)),
                  pl.BlockSpec((None, k, tn), lambda i, j: (layer, 0, j)),
                  pl.BlockSpec((tm, tn), lambda i, j: (i, j))],
        out_specs=pl.BlockSpec((tm, tn), lambda i, j: (i, j)),
        out_shape=jax.ShapeDtypeStruct((m, n), F32),
        compiler_params=_cparams(("arbitrary", "arbitrary")),
    )(x, w, res)


SB_SUB = LANES
SB_GROUP = 2


def _sb_prompt_kernel(bias_ref, q_ref, k_ref, v_ref, o_ref, acc_ref, run_ref, *, tq, scale):
    h = pl.program_id(1)
    i = pl.program_id(2)
    sub = SB_SUB
    bias = bias_ref[h]
    q = (q_ref[...] * scale).astype(BF16)
    later = _iota((sub, sub), 0) > _iota((sub, sub), 1)
    suffix_total = jnp.concatenate([later.astype(BF16), jnp.ones((sub, sub), BF16)], axis=1)

    def sub_block(qv, start, run, mask):
        kb = k_ref[pl.ds(start, sub), :].astype(BF16)
        vb = v_ref[pl.ds(start, sub), :].astype(BF16)
        z = _dot_nt(qv, kb) + bias
        sp = _softplus(z)
        lg = -sp if mask is None else jnp.where(mask, -sp, 0.0)
        hi = lg.astype(BF16)
        lo = (lg - hi.astype(F32)).astype(BF16)
        st = _dot(hi, suffix_total) + _dot(lo, suffix_total)
        e = jnp.exp(z - sp + st[:, :sub] + run)
        w = e if mask is None else jnp.where(mask, e, 0.0)
        return _dot(w.astype(BF16), vb), run + st[:, sub:]

    acc_ref[...] = jnp.zeros_like(acc_ref)
    run_ref[...] = jnp.zeros_like(run_ref)
    q0 = pl.multiple_of(i * tq, tq)
    for jd in reversed(range(tq // sub)):
        r0 = jd * sub
        rows = tq - r0
        mask = _iota((rows, sub), 1) < _iota((rows, sub), 0)
        pv, run = sub_block(q[r0:], q0 + r0, run_ref[r0:, :], mask)
        acc_ref[r0:, :] += pv
        run_ref[r0:, :] = run

    def body(g, carry):
        run = run_ref[...]
        total = None
        for u in range(SB_GROUP):
            start = pl.multiple_of(q0 - (g * SB_GROUP + u + 1) * sub, sub)
            pv, run = sub_block(q, start, run, None)
            total = pv if total is None else total + pv
        acc_ref[...] += total
        run_ref[...] = run
        return carry

    lax.fori_loop(0, (i * tq) // (SB_GROUP * sub), body, 0)
    o_ref[...] = acc_ref[...].astype(o_ref.dtype)


def _sb_prompt(proj, bias, nb, t, tq=512):
    assert t % tq == 0 and tq % (SB_GROUP * SB_SUB) == 0
    nq = t // tq
    hb = SB_HEADS
    return pl.pallas_call(
        functools.partial(_sb_prompt_kernel, tq=tq, scale=SB_HEAD_DIM ** -0.5),
        grid=(nb, hb, nq),
        in_specs=[pl.BlockSpec(memory_space=pltpu.SMEM),
                  pl.BlockSpec((tq, SB_HEAD_DIM), lambda b, h, i: (b * nq + i, h)),
                  pl.BlockSpec((t, SB_HEAD_DIM), lambda b, h, i: (b, hb + h)),
                  pl.BlockSpec((t, SB_HEAD_DIM), lambda b, h, i: (b, 2 * hb + h))],
        out_specs=pl.BlockSpec((tq, SB_HEAD_DIM), lambda b, h, i: (b * nq + i, h)),
        out_shape=jax.ShapeDtypeStruct((nb * t, D_SB), BF16),
        scratch_shapes=[pltpu.VMEM((tq, SB_HEAD_DIM), F32), pltpu.VMEM((tq, SB_SUB), F32)],
        compiler_params=_cparams(("arbitrary", "arbitrary", "arbitrary")),
    )(bias, proj, proj, proj)


def _sb_sample_kernel(pt_ref, qbd_ref, bias_ref, kown_ref, vown_ref, *refs, pp, nq):
    del pt_ref
    kp = refs[:pp]
    vp = refs[pp:2 * pp]
    o_ref = refs[2 * pp]
    acc_ref, run_ref = refs[2 * pp + 1:]
    s = pl.program_id(1)
    n = PAGE_SIZE
    rows_c = 2 * nq
    earlier = (_iota((n, n), 1) > _iota((n, n), 0)).astype(BF16)
    qbd = qbd_ref[...]
    bias = bias_ref[...]

    def head_rows(ref, hh):
        return ref[pl.ds(hh, n, stride=SB_HEADS), :].astype(BF16)

    def process(k_ref, v_ref, own):
        kcat = jnp.concatenate([head_rows(k_ref, hh) for hh in range(SB_HEADS)], axis=1)
        z = _dot(kcat, qbd) + bias
        sp = _softplus(z)
        if own:
            mask = _iota((n, LANES), 0) < (_iota((n, LANES), 1) % nq)
            lg = jnp.where(mask, -sp, 0.0)
        else:
            lg = -sp
        hi = lg.astype(BF16)
        lo = (lg - hi.astype(F32)).astype(BF16)
        suffix = _dot(earlier, hi) + _dot(earlier, lo) + run_ref[...]
        w = jnp.exp(z - sp + suffix)
        if own:
            w = jnp.where(mask, w, 0.0)
        run_ref[...] += jnp.sum(lg, axis=0, keepdims=True)
        wt = w.T.astype(BF16)
        for hh in range(SB_HEADS):
            c = hh // 2
            acc_ref[hh] += _dot(wt[c * rows_c:(c + 1) * rows_c, :], head_rows(v_ref, hh))

    @pl.when(s == 0)
    def _():
        acc_ref[...] = jnp.zeros_like(acc_ref)
        run_ref[...] = jnp.zeros_like(run_ref)
        process(kown_ref, vown_ref, True)

    for c in range(pp):
        process(kp[c], vp[c], False)

    @pl.when(s == pl.num_programs(1) - 1)
    def _():
        for hh in range(SB_HEADS):
            e = hh % 2
            o_ref[:, hh * SB_HEAD_DIM:(hh + 1) * SB_HEAD_DIM] = acc_ref[hh, e * nq:(e + 1) * nq, :]


def _sb_sample(q, k_own, v_own, bias, cache_k, cache_v, page_table, layer, pp=4):
    db, nq, _ = q.shape
    assert 2 * nq == 16 and SB_HEADS * nq <= LANES and cache_k.shape[2:] == (PAGE_SIZE, SB_HEADS, SB_HEAD_DIM)
    depth, n_pool = cache_k.shape[0], cache_k.shape[1]
    n_pages = page_table.shape[1]
    scale = SB_HEAD_DIM ** -0.5
    qh = (q * scale).reshape(db, nq, SB_HEADS, SB_HEAD_DIM)
    eye = jnp.eye(SB_HEADS, dtype=F32)
    qbd = jnp.einsum('bthd,hg->bhdgt', qh, eye).reshape(db, D_SB, SB_HEADS * nq)
    qbd = jnp.pad(qbd, ((0, 0), (0, 0), (0, LANES - SB_HEADS * nq))).astype(BF16)
    bias_l = jnp.pad(jnp.repeat(bias.astype(F32), nq), (0, LANES - SB_HEADS * nq)).reshape(1, LANES)
    pad = ((0, 0), (0, PAGE_SIZE - nq), (0, 0))
    page_rows = PAGE_SIZE * SB_HEADS
    kown = jnp.pad(k_own, pad).reshape(db * page_rows, SB_HEAD_DIM)
    vown = jnp.pad(v_own, pad).reshape(db * page_rows, SB_HEAD_DIM)
    ck = cache_k.reshape(depth * n_pool * page_rows, SB_HEAD_DIM)
    cv = cache_v.reshape(depth * n_pool * page_rows, SB_HEAD_DIM)
    base = layer * n_pool

    def page_map(c):
        return lambda b, s, pt: (base + pt[b, n_pages - 1 - (s * pp + c)], 0)

    page_spec = [pl.BlockSpec((page_rows, SB_HEAD_DIM), page_map(c)) for c in range(pp)]
    own_spec = pl.BlockSpec((page_rows, SB_HEAD_DIM), lambda b, s, pt: (b, 0))
    grid_spec = pltpu.PrefetchScalarGridSpec(
        num_scalar_prefetch=1,
        grid=(db, n_pages // pp),
        in_specs=[pl.BlockSpec((None, D_SB, LANES), lambda b, s, pt: (b, 0, 0)),
                  pl.BlockSpec((1, LANES), lambda b, s, pt: (0, 0)),
                  own_spec, own_spec] + page_spec + page_spec,
        out_specs=pl.BlockSpec((nq, D_SB), lambda b, s, pt: (b, 0)),
        scratch_shapes=[pltpu.VMEM((SB_HEADS, 2 * nq, SB_HEAD_DIM), F32),
                        pltpu.VMEM((1, LANES), F32)],
    )
    return pl.pallas_call(
        functools.partial(_sb_sample_kernel, pp=pp, nq=nq),
        grid_spec=grid_spec,
        out_shape=jax.ShapeDtypeStruct((db * nq, D_SB), F32),
        compiler_params=_cparams(("arbitrary", "arbitrary")),
    )(page_table, qbd, bias_l, kown, vown, *([ck] * pp), *([cv] * pp))


def _ssd_kernel(*refs, tb, has_init):
    if has_init:
        (z0_ref, z1_ref, xa_ref, xb_ref, bc_ref, dt_ref, ctx_ref, h0_ref, cw_ref, cb_ref, dtb_ref, alog_ref,
         dsk_ref, ng_ref, y_ref, hT_ref, xp_ref, hs_ref, yb_ref) = refs
    else:
        (z0_ref, z1_ref, xa_ref, xb_ref, bc_ref, dt_ref, ctx_ref, cw_ref, cb_ref, dtb_ref, alog_ref,
         dsk_ref, ng_ref, y_ref, hT_ref, xp_ref, hs_ref, yb_ref) = refs
        h0_ref = None
    c = pl.program_id(1)
    L = TILE
    P2 = LANES
    n_pairs = D_SSM // P2

    @pl.when(c == 0)
    def _():
        xp_ref[0:8, :] = ctx_ref[...]
        if has_init:
            hs_ref[...] = h0_ref[...]
        else:
            hs_ref[...] = jnp.zeros_like(hs_ref)

    xp_ref[8:8 + tb, 0:1024] = xa_ref[...]
    xp_ref[8:8 + tb, 1024:2048] = xb_ref[...]
    xp_ref[8:8 + tb, 2048:3072] = bc_ref[...]
    if tb < L:
        xp_ref[8 + tb:8 + L, :] = jnp.zeros((L - tb, SSM_CONV_DIM), F32)

    conv = cb_ref[...] + cw_ref[0:1, :] * xp_ref[5:5 + L, :]
    for i in range(1, SSM_CONV):
        conv = conv + cw_ref[i:i + 1, :] * xp_ref[5 + i:5 + i + L, :]
    xp_ref[0:8, :] = xp_ref[tb:tb + 8, :]
    xc = _silu(conv)

    row = _iota((L, LANES), 0)
    lane = _iota((L, LANES), 1)
    dt_raw = dt_ref[...]
    if tb < L:
        dt_raw = jnp.concatenate([dt_raw, jnp.zeros((L - tb, LANES), F32)], axis=0)
    valid = (lane < SSM_HEADS) & (row < tb)
    dtv = jnp.where(valid, _softplus(dt_raw + dtb_ref[...]), 0.0)
    a = dtv * (-jnp.exp(alog_ref[...]))
    lower = (_iota((L, L), 0) >= _iota((L, L), 1))
    lower_b = lower.astype(BF16)
    upper_b = (_iota((L, L), 0) <= _iota((L, L), 1)).astype(BF16)
    cum = _sel_dot_l(lower_b, a)
    cum_t = _sel_dot_r(a.T, upper_b)
    last = cum[L - 1:L, :]
    spread = (_iota((LANES, D_SSM), 0) == _iota((LANES, D_SSM), 1) // SSM_HEAD_DIM).astype(BF16)
    spread_t = (_iota((D_SSM, LANES), 0) // SSM_HEAD_DIM == _iota((D_SSM, LANES), 1)).astype(BF16)
    dt_x = _sel_dot_r(dtv, spread)
    ecum_x = _sel_dot_r(jnp.exp(cum), spread)
    wdec_x = _sel_dot_r(jnp.exp(last - cum), spread)
    sdec = _sel_dot_l(spread_t, jnp.broadcast_to(jnp.exp(cum_t[:, L - 1:L]), (LANES, LANES)))

    first_half = lane < SSM_HEAD_DIM
    pairs_per_group = n_pairs // SSM_GROUPS
    for g in range(SSM_GROUPS):
        bg = xc[:, D_SSM + g * SSM_STATE:D_SSM + (g + 1) * SSM_STATE]
        cg = xc[:, D_SSM + (SSM_GROUPS + g) * SSM_STATE:D_SSM + (SSM_GROUPS + g + 1) * SSM_STATE]
        bg_b = bg.astype(BF16)
        cg_b = cg.astype(BF16)
        cb = _dot_nt(cg_b, bg_b)
        for r in range(pairs_per_group):
            p = g * pairs_per_group + r
            sl = slice(p * P2, (p + 1) * P2)
            x_p = xc[:, sl]
            xs_p = x_p * dt_x[:, sl]
            xs_b = xs_p.astype(BF16)
            ys = []
            for e in range(2):
                hh = 2 * p + e
                seg = jnp.broadcast_to(cum[:, hh:hh + 1], (L, L)) - jnp.broadcast_to(cum_t[hh:hh + 1, :], (L, L))
                dec = jnp.where(lower, jnp.exp(jnp.minimum(seg, 0.0)), 0.0)
                ys.append(_dot((cb * dec).astype(BF16), xs_b))
            y_p = jnp.where(first_half, ys[0], ys[1])
            h_p = hs_ref[sl, :]
            y_p = y_p + _dot_nt(cg_b, h_p.astype(BF16)) * ecum_x[:, sl]
            xw = (xs_p * wdec_x[:, sl]).astype(BF16)
            hs_ref[sl, :] = sdec[sl, :] * h_p + _dot(xw.T, bg_b)
            yb_ref[:, sl] = y_p + dsk_ref[:, sl] * x_p

    zfull = jnp.concatenate([z0_ref[...], z1_ref[...]], axis=1)
    y = yb_ref[0:tb, :] * _silu(zfull)
    gw = D_SSM // SSM_GROUPS
    for g in range(SSM_GROUPS):
        yg = y[:, g * gw:(g + 1) * gw]
        yn = yg * lax.rsqrt(jnp.mean(yg * yg, axis=-1, keepdims=True) + EPS)
        y_ref[:, g * gw:(g + 1) * gw] = (yn * ng_ref[:, g * gw:(g + 1) * gw]).astype(y_ref.dtype)

    @pl.when(c == pl.num_programs(1) - 1)
    def _():
        hT_ref[...] = hs_ref[...]


def _ssd(proj, row0, nb, t, ctx8, h0, conv_w, conv_b, dt_bias, a_log, d_skip, norm_g, out_dtype):
    tb = min(t, TILE)
    nc = t // tb
    rb0 = row0 // tb
    has_init = h0 is not None

    def colblk(width, idx):
        return pl.BlockSpec((tb, width), lambda b, c: (rb0 + b * nc + c, idx))

    def full(shape):
        return pl.BlockSpec(shape, lambda b, c: (0,) * len(shape))

    in_specs = [colblk(1024, 3), colblk(1024, 4), colblk(1024, 5), colblk(1024, 6), colblk(1024, 7),
                colblk(LANES, 64),
                pl.BlockSpec((None, 8, SSM_CONV_DIM), lambda b, c: (b, 0, 0))]
    args = [proj] * 6 + [ctx8]
    if has_init:
        in_specs.append(pl.BlockSpec((None, D_SSM, SSM_STATE), lambda b, c: (b, 0, 0)))
        args.append(h0)
    padl = (0, LANES - SSM_HEADS)
    in_specs += [full((SSM_CONV, SSM_CONV_DIM)), full((1, SSM_CONV_DIM)), full((1, LANES)), full((1, LANES)),
                 full((1, D_SSM)), full((1, D_SSM))]
    args += [conv_w, conv_b.reshape(1, -1), jnp.pad(dt_bias, padl).reshape(1, LANES),
             jnp.pad(a_log, padl).reshape(1, LANES), jnp.repeat(d_skip, SSM_HEAD_DIM).reshape(1, D_SSM),
             norm_g.reshape(1, D_SSM)]
    return pl.pallas_call(
        functools.partial(_ssd_kernel, tb=tb, has_init=has_init),
        grid=(nb, nc),
        in_specs=in_specs,
        out_specs=[pl.BlockSpec((tb, D_SSM), lambda b, c: (b * nc + c, 0)),
                   pl.BlockSpec((None, D_SSM, SSM_STATE), lambda b, c: (b, 0, 0))],
        out_shape=[jax.ShapeDtypeStruct((nb * t, D_SSM), out_dtype),
                   jax.ShapeDtypeStruct((nb, D_SSM, SSM_STATE), F32)],
        scratch_shapes=[pltpu.VMEM((8 + TILE, SSM_CONV_DIM), F32),
                        pltpu.VMEM((D_SSM, SSM_STATE), F32),
                        pltpu.VMEM((TILE, D_SSM), F32)],
        compiler_params=_cparams(("arbitrary", "arbitrary")),
    )(*args)


def _gla_kernel(*refs, tb, has_init):
    if has_init:
        (q_ref, k_ref, v_ref, r_ref, gk_ref, s0_ref, gw_ref, gb_ref, ng_ref, o_ref, sT_ref, st_ref) = refs
    else:
        (q_ref, k_ref, v_ref, r_ref, gk_ref, gw_ref, gb_ref, ng_ref, o_ref, sT_ref, st_ref) = refs
        s0_ref = None
    c = pl.program_id(1)
    L = TILE
    CH = min(GLA_CHUNK, tb)
    n_sub = max(tb // CH, 1)
    KP = GLA_HEADS * GLA_K_DIM

    @pl.when(c == 0)
    def _():
        if has_init:
            st_ref[...] = s0_ref[...]
        else:
            st_ref[...] = jnp.zeros_like(st_ref)

    def padrows(x):
        if tb < L:
            return jnp.concatenate([x, jnp.zeros((L - tb, x.shape[1]), F32)], axis=0)
        return x

    row = _iota((L, L), 0)
    col = _iota((L, L), 1)
    same_chunk = (row // CH) == (col // CH)
    lower = (row >= col) & same_chunk
    lower_b = lower.astype(BF16)
    rvalid = _iota((L, KP), 0) < tb
    glog = _dot(gk_ref[...].astype(BF16), gw_ref[...].astype(BF16)) + gb_ref[...]
    glog = padrows(-_softplus(-glog) * (1.0 / GLA_TAU))
    glog = jnp.where(rvalid, glog, 0.0)
    q_all = padrows(q_ref[...]) * (GLA_K_DIM ** -0.5)
    k_all = jnp.where(rvalid, padrows(k_ref[...]), 0.0)
    v_all = padrows(v_ref[...])
    lane = _iota((L, LANES), 1)
    rowl = _iota((L, LANES), 0)

    for p in range(KP // LANES):
        sl = slice(p * LANES, (p + 1) * LANES)
        g_p = glog[:, sl]
        b_p = _sel_dot_l(lower_b, g_p)
        qt = q_all[:, sl] * jnp.exp(b_p)
        kt = (k_all[:, sl] * jnp.exp(-b_p)).astype(BF16)
        lasts = [b_p[min((j + 1) * CH, L) - 1:min((j + 1) * CH, L), :] for j in range(n_sub)]
        s_cur = st_ref[sl, :]
        s_list = [s_cur]
        kw_t = []
        for j in range(n_sub):
            in_j = (rowl // CH) == j
            kw = jnp.where(in_j, k_all[:, sl] * jnp.exp(lasts[j] - b_p), 0.0)
            kw_t.append(kw.T.astype(BF16))
        heads_out = []
        new_states = [[None, None] for _ in range(n_sub)]
        for e in range(2):
            hh = 2 * p + e
            own = (lane // GLA_K_DIM) == e
            qm = jnp.where(own, qt, 0.0).astype(BF16)
            att = jnp.where(lower, _dot_nt(qm, kt), 0.0)
            v_h = v_all[:, hh * GLA_V_DIM:(hh + 1) * GLA_V_DIM].astype(BF16)
            heads_out.append((qm, _dot(att.astype(BF16), v_h), v_h))
        decs = []
        for j in range(n_sub):
            dcol = jnp.exp(lasts[j]).reshape(1, LANES)
            decs.append(jnp.broadcast_to(dcol, (LANES, LANES)).T)
        for j in range(n_sub):
            upd0 = _dot(kw_t[j], heads_out[0][2])
            upd1 = _dot(kw_t[j], heads_out[1][2])
            upd = jnp.where(_iota((LANES, GLA_V_DIM), 0) < GLA_K_DIM, upd0, upd1)
            s_list.append(decs[j] * s_list[j] + upd)
        st_ref[sl, :] = s_list[n_sub]
        for e in range(2):
            hh = 2 * p + e
            qm, o_h, _ = heads_out[e]
            inter = _dot(qm, s_list[0].astype(BF16))
            for j in range(1, n_sub):
                inter = jnp.where((rowl // CH) == j, _dot(qm, s_list[j].astype(BF16)), inter)
            o_h = o_h + inter
            o_h = o_h[0:tb, :]
            on = o_h * lax.rsqrt(jnp.mean(o_h * o_h, axis=-1, keepdims=True) + EPS) * ng_ref[...]
            vs = slice(hh * GLA_V_DIM, (hh + 1) * GLA_V_DIM)
            o_ref[:, vs] = (on * _silu(r_ref[:, vs])).astype(o_ref.dtype)

    @pl.when(c == pl.num_programs(1) - 1)
    def _():
        sT_ref[...] = st_ref[...]


def _gla(q, k, v, r, gk_low, nb, t, s0, gk_w, gk_b, norm_g, out_dtype):
    tb = min(t, TILE)
    nc = t // tb
    has_init = s0 is not None
    KP = GLA_HEADS * GLA_K_DIM

    def rows(width):
        return pl.BlockSpec((tb, width), lambda b, c: (b * nc + c, 0))

    def full(shape):
        return pl.BlockSpec(shape, lambda b, c: (0,) * len(shape))

    in_specs = [rows(KP), rows(KP), rows(D_GLA), rows(D_GLA), rows(GLA_RANK)]
    args = [q, k, v, r, gk_low]
    if has_init:
        in_specs.append(pl.BlockSpec((None, KP, GLA_V_DIM), lambda b, c: (b, 0, 0)))
        args.append(s0)
    in_specs += [full((GLA_RANK, KP)), full((1, KP)), full((1, GLA_V_DIM))]
    args += [gk_w, gk_b.reshape(1, KP), norm_g.reshape(1, GLA_V_DIM)]
    return pl.pallas_call(
        functools.partial(_gla_kernel, tb=tb, has_init=has_init),
        grid=(nb, nc),
        in_specs=in_specs,
        out_specs=[rows(D_GLA), pl.BlockSpec((None, KP, GLA_V_DIM), lambda b, c: (b, 0, 0))],
        out_shape=[jax.ShapeDtypeStruct((nb * t, D_GLA), out_dtype),
                   jax.ShapeDtypeStruct((nb, KP, GLA_V_DIM), F32)],
        scratch_shapes=[pltpu.VMEM((KP, GLA_V_DIM), F32)],
        compiler_params=_cparams(("arbitrary", "arbitrary")),
    )(*args)


def kernel(x_prompt, x_sample, cache_k, cache_v, page_table, state_conv, state_ssm, state_gla, norm1_g, w_in, sb_bias, conv_w, conv_b, dt_bias, a_log, d_skip, ssm_norm_g, gla_gk_w, gla_gk_b, gla_norm_g, w_out, norm2_g, w_gate, w_up, w_down, final_norm_g):
    bp, tp, d = x_prompt.shape
    bs, ts, _ = x_sample.shape
    depth = w_in.shape[0]
    mp = bp * tp
    ms = bs * ts
    m = mp + ms
    tm = _row_tile(m, 1376)
    tm_s = _row_tile(m, 688)
    d_ff = w_gate.shape[2]
    KP = GLA_HEADS * GLA_K_DIM
    off_xbc = 3 * D_SB + D_SSM
    off_gla = off_xbc + SSM_CONV_DIM + SSM_HEADS
    n_tail = SSM_CONV - 1

    x = jnp.concatenate([x_prompt.reshape(mp, d), x_sample.reshape(ms, d)], axis=0)
    w_down_bf = _cast_bf16(w_down, _row_tile(d_ff, 688))
    outs = {k: [] for k in ("kp", "vp", "cp", "sp", "gp", "ks", "vs", "cs", "ss", "gs")}

    for l in range(depth):
        xn = _rmsnorm(x, norm1_g[l], BF16, tm_s)
        proj = _matmul(xn, w_in, l, None, tm, 512)

        k_rows = proj[:, D_SB:2 * D_SB]
        v_rows = proj[:, 2 * D_SB:3 * D_SB]
        gq = proj[:, off_gla:off_gla + KP]
        gk = proj[:, off_gla + KP:off_gla + 2 * KP]
        gv = proj[:, off_gla + 2 * KP:off_gla + 2 * KP + D_GLA]
        gr = proj[:, off_gla + 2 * KP + D_GLA:off_gla + 2 * KP + 2 * D_GLA]
        glow = proj[:, off_gla + 2 * KP + 2 * D_GLA:]

        o_sb_p = _sb_prompt(proj, sb_bias[l], bp, tp)
        ctx_p = jnp.zeros((bp, 8, SSM_CONV_DIM), F32)
        y_ssm_p, ssm_p = _ssd(proj, 0, bp, tp, ctx_p, None, conv_w[l], conv_b[l], dt_bias[l], a_log[l],
                              d_skip[l], ssm_norm_g[l], BF16)
        o_gl_p, gla_p = _gla(gq[:mp], gk[:mp], gv[:mp], gr[:mp], glow[:mp], bp, tp, None,
                             gla_gk_w[l], gla_gk_b[l], gla_norm_g[l], BF16)

        q_s = proj[mp:, 0:D_SB].reshape(bs, ts, D_SB)
        k_s = k_rows[mp:].reshape(bs, ts, D_SB)
        v_s = v_rows[mp:].reshape(bs, ts, D_SB)
        o_sb_s = _sb_sample(q_s, k_s, v_s, sb_bias[l], cache_k, cache_v, page_table, l)
        ctx_s = jnp.pad(state_conv[l], ((0, 0), (8 - (SSM_CONV - 1), 0), (0, 0)))
        y_ssm_s, ssm_s = _ssd(proj, mp, bs, ts, ctx_s, state_ssm[l].reshape(bs, D_SSM, SSM_STATE),
                              conv_w[l], conv_b[l], dt_bias[l], a_log[l], d_skip[l], ssm_norm_g[l], F32)
        o_gl_s, gla_s = _gla(gq[mp:], gk[mp:], gv[mp:], gr[mp:], glow[mp:], bs, ts,
                             state_gla[l].reshape(bs, KP, GLA_V_DIM),
                             gla_gk_w[l], gla_gk_b[l], gla_norm_g[l], F32)

        mixed = jnp.concatenate(
            [jnp.concatenate([o_sb_p, y_ssm_p, o_gl_p], axis=1),
             jnp.concatenate([o_sb_s, y_ssm_s, o_gl_s], axis=1).astype(BF16)], axis=0)
        x = _matmul(mixed, w_out, l, x, tm_s, 512)
        hn = _rmsnorm(x, norm2_g[l], BF16, tm_s)
        hmid = _gate_up(hn, w_gate, w_up, l, tm, 256)
        x = _matmul_bf16w(hmid, w_down_bf, l, x, tm_s, 512)

        xbc_p = proj[:mp].reshape(bp, tp, -1)[:, tp - n_tail:, off_xbc:off_xbc + SSM_CONV_DIM]
        xbc_s = jnp.concatenate([state_conv[l], proj[mp:, off_xbc:off_xbc + SSM_CONV_DIM].reshape(bs, ts, -1)], axis=1)
        outs["kp"].append(k_rows[:mp].reshape(bp, tp, SB_HEADS, SB_HEAD_DIM))
        outs["vp"].append(v_rows[:mp].reshape(bp, tp, SB_HEADS, SB_HEAD_DIM))
        outs["cp"].append(xbc_p)
        outs["sp"].append(ssm_p.reshape(bp, SSM_HEADS, SSM_HEAD_DIM, SSM_STATE))
        outs["gp"].append(gla_p.reshape(bp, GLA_HEADS, GLA_K_DIM, GLA_V_DIM))
        outs["ks"].append(k_s.reshape(bs, ts, SB_HEADS, SB_HEAD_DIM))
        outs["vs"].append(v_s.reshape(bs, ts, SB_HEADS, SB_HEAD_DIM))
        outs["cs"].append(xbc_s[:, xbc_s.shape[1] - n_tail:])
        outs["ss"].append(ssm_s.reshape(bs, SSM_HEADS, SSM_HEAD_DIM, SSM_STATE))
        outs["gs"].append(gla_s.reshape(bs, GLA_HEADS, GLA_K_DIM, GLA_V_DIM))

    y_p = _rmsnorm(x, final_norm_g, F32, _row_tile(mp, 512), rows=mp)
    y_s = _rmsnorm(x[mp:], final_norm_g, F32, ms)
    st = {k: jnp.stack(v) for k, v in outs.items()}
    return (y_p.reshape(bp, tp, d), y_s.reshape(bs, ts, d),
            st["kp"], st["vp"], st["cp"], st["sp"], st["gp"],
            st["ks"], st["vs"], st["cs"], st["ss"], st["gs"])
```

```python
import functools

import jax
import jax.numpy as jnp
from jax import lax
from jax.experimental import pallas as pl
from jax.experimental.pallas import tpu as pltpu

F32 = jnp.float32
BF16 = jnp.bfloat16

EPS = 1e-6
SB_HEADS = 8
SB_HEAD_DIM = 128
D_SB = SB_HEADS * SB_HEAD_DIM
SSM_HEADS = 32
SSM_HEAD_DIM = 64
D_SSM = SSM_HEADS * SSM_HEAD_DIM
SSM_STATE = 128
SSM_GROUPS = 4
SSM_CONV = 4
SSM_CONV_DIM = D_SSM + 2 * SSM_GROUPS * SSM_STATE
GLA_HEADS = 8
GLA_K_DIM = 64
GLA_V_DIM = 128
D_GLA = GLA_HEADS * GLA_V_DIM
GLA_RANK = 16
GLA_TAU = 16.0
GLA_CHUNK = 64
PAGE_SIZE = 128

LANES = 128
TILE = 128
VMEM_LIMIT = 60 * 1024 * 1024


def _cparams(sem):
    return pltpu.CompilerParams(dimension_semantics=sem, vmem_limit_bytes=VMEM_LIMIT)


def _softplus(z):
    return jnp.maximum(z, 0.0) + jnp.log(1.0 + jnp.exp(-jnp.abs(z)))


def _silu(x):
    return x * (0.5 + 0.5 * jnp.tanh(0.5 * x))


def _split2(d):
    d0 = d.astype(BF16)
    d1 = (d - d0.astype(F32)).astype(BF16)
    return d0, d1


def _dot(a, b):
    return jnp.dot(a, b, preferred_element_type=F32)


def _dot_nt(a, b):
    return lax.dot_general(a, b, (((1,), (1,)), ((), ())), preferred_element_type=F32)


def _sel_dot_r(data, sel):
    d0, d1 = _split2(data)
    return _dot(d0, sel) + _dot(d1, sel)


def _sel_dot_l(sel, data):
    d0, d1 = _split2(data)
    return _dot(sel, d0) + _dot(sel, d1)


def _iota(shape, dim):
    return lax.broadcasted_iota(jnp.int32, shape, dim)


def _rmsnorm_kernel(x_ref, g_ref, o_ref):
    x = x_ref[...]
    y = x * lax.rsqrt(jnp.mean(x * x, axis=-1, keepdims=True) + EPS)
    o_ref[...] = (y * g_ref[...]).astype(o_ref.dtype)


def _row_tile(m, target):
    best = None
    for t in range(16, target + 1, 16):
        if m % t == 0:
            best = t
    assert best is not None, (m, target)
    return best


def _rmsnorm(x, g, out_dtype, tm, rows=None):
    m, d = x.shape
    m = m if rows is None else rows
    return pl.pallas_call(
        _rmsnorm_kernel,
        grid=(m // tm,),
        in_specs=[pl.BlockSpec((tm, d), lambda i: (i, 0)), pl.BlockSpec((1, d), lambda i: (0, 0))],
        out_specs=pl.BlockSpec((tm, d), lambda i: (i, 0)),
        out_shape=jax.ShapeDtypeStruct((m, d), out_dtype),
        compiler_params=_cparams(("arbitrary",)),
    )(x, g.reshape(1, d))


def _mm_kernel(x_ref, w_ref, *rest, has_res, w_is_nk):
    if has_res:
        res_ref, o_ref, wbf_ref = rest
    else:
        o_ref, wbf_ref = rest

    @pl.when(pl.program_id(1) == 0)
    def _():
        wbf_ref[...] = w_ref[...].astype(BF16)

    acc = _dot_nt(x_ref[...], wbf_ref[...]) if w_is_nk else _dot(x_ref[...], wbf_ref[...])
    if has_res:
        acc = acc + res_ref[...]
    o_ref[...] = acc.astype(o_ref.dtype)


def _matmul(x, w, layer, res, tm, tn, w_is_nk=False):
    m, k = x.shape
    if w_is_nk:
        n = w.shape[1]
        w_spec = pl.BlockSpec((None, tn, k), lambda j, i: (layer, j, 0))
        w_scratch = pltpu.VMEM((tn, k), BF16)
    else:
        n = w.shape[2]
        w_spec = pl.BlockSpec((None, k, tn), lambda j, i: (layer, 0, j))
        w_scratch = pltpu.VMEM((k, tn), BF16)
    in_specs = [pl.BlockSpec((tm, k), lambda j, i: (i, 0)), w_spec]
    args = [x, w]
    if res is not None:
        in_specs.append(pl.BlockSpec((tm, tn), lambda j, i: (i, j)))
        args.append(res)
    return pl.pallas_call(
        functools.partial(_mm_kernel, has_res=res is not None, w_is_nk=w_is_nk),
        grid=(pl.cdiv(n, tn), m // tm),
        in_specs=in_specs,
        out_specs=pl.BlockSpec((tm, tn), lambda j, i: (i, j)),
        out_shape=jax.ShapeDtypeStruct((m, n), F32),
        scratch_shapes=[w_scratch],
        compiler_params=_cparams(("arbitrary", "arbitrary")),
    )(*args)


MM_SLABS = 4


def _slabs(rows, n):
    step = -(-rows // (16 * n)) * 16
    bounds = [min(i * step, rows) for i in range(n + 1)]
    return [(a, b) for a, b in zip(bounds[:-1], bounds[1:]) if b > a]


def _gate_up_kernel(x_ref, wg_ref, wu_ref, o_ref, wbf_ref, *, tn):
    @pl.when(pl.program_id(1) == 0)
    def _():
        wbf_ref[:, :tn] = wg_ref[...].astype(BF16)
        wbf_ref[:, tn:] = wu_ref[...].astype(BF16)

    for r0, r1 in _slabs(o_ref.shape[0], MM_SLABS):
        gu = _dot(x_ref[r0:r1, :], wbf_ref[...])
        o_ref[r0:r1, :] = (_silu(gu[:, :tn]) * gu[:, tn:]).astype(o_ref.dtype)


def _gate_up(x, wg, wu, layer, tm, tn):
    m, k = x.shape
    f = wg.shape[2]
    wspec = pl.BlockSpec((None, k, tn), lambda j, i: (layer, 0, j))
    return pl.pallas_call(
        functools.partial(_gate_up_kernel, tn=tn),
        grid=(f // tn, m // tm),
        in_specs=[pl.BlockSpec((tm, k), lambda j, i: (i, 0)), wspec, wspec],
        out_specs=pl.BlockSpec((tm, tn), lambda j, i: (i, j)),
        out_shape=jax.ShapeDtypeStruct((m, f), BF16),
        scratch_shapes=[pltpu.VMEM((k, 2 * tn), BF16)],
        compiler_params=_cparams(("arbitrary", "arbitrary")),
    )(x, wg, wu)


def _cast_kernel(w_ref, o_ref):
    o_ref[...] = w_ref[...].astype(o_ref.dtype)


def _cast_bf16(w, tr):
    nl, r, c = w.shape
    spec = pl.BlockSpec((None, tr, c), lambda l, i: (l, i, 0))
    return pl.pallas_call(
        _cast_kernel,
        grid=(nl, r // tr),
        in_specs=[spec],
        out_specs=spec,
        out_shape=jax.ShapeDtypeStruct(w.shape, BF16),
        compiler_params=_cparams(("arbitrary", "arbitrary")),
    )(w)


def _mm_res_kernel(x_ref, w_ref, res_ref, o_ref):
    o_ref[...] = _dot(x_ref[...], w_ref[...]) + res_ref[...]


def _matmul_bf16w(x, w, layer, res, tm, tn):
    m, k = x.shape
    n = w.shape[2]
    return pl.pallas_call(
        _mm_res_kernel,
        grid=(m // tm, n // tn),
        in_specs=[pl.BlockSpec((tm, k), lambda i, j: (i, 0), pipeline_mode=pl.Buffered(1)),
                  pl.BlockSpec((None, k, tn), lambda i, j: (layer, 0, j)),
                  pl.BlockSpec((tm, tn), lambda i, j: (i, j))],
        out_specs=pl.BlockSpec((tm, tn), lambda i, j: (i, j)),
        out_shape=jax.ShapeDtypeStruct((m, n), F32),
        compiler_params=_cparams(("arbitrary", "arbitrary")),
    )(x, w, res)


SB_KEYS = 256
SB_UNROLL = 2


def _sb_prompt_kernel(bias_ref, q_ref, k_ref, v_ref, o_ref, acc_ref, run_ref, *, tq, scale):
    h = pl.program_id(1)
    i = pl.program_id(2)
    nk = SB_KEYS
    bias = bias_ref[h]
    q = (q_ref[...] * scale).astype(BF16)
    later = _iota((nk, nk), 0) > _iota((nk, nk), 1)
    suffix_total = jnp.concatenate([later.astype(BF16), jnp.ones((nk, LANES), BF16)], axis=1)

    def block(qv, start, run, mask):
        kb = k_ref[pl.ds(start, nk), :].astype(BF16)
        vb = v_ref[pl.ds(start, nk), :].astype(BF16)
        z = _dot_nt(qv, kb) + bias
        sp = _softplus(z)
        lg = -sp if mask is None else jnp.where(mask, -sp, 0.0)
        st = _dot(lg.astype(BF16), suffix_total)
        e = jnp.exp(z - sp + st[:, :nk] + jnp.concatenate([run] * (nk // LANES), axis=1))
        w = e if mask is None else jnp.where(mask, e, 0.0)
        return _dot(w.astype(BF16), vb), run + st[:, nk:]

    acc_ref[...] = jnp.zeros_like(acc_ref)
    run_ref[...] = jnp.zeros_like(run_ref)
    q0 = pl.multiple_of(i * tq, tq)
    for jd in reversed(range(tq // nk)):
        r0 = jd * nk
        rows = tq - r0
        mask = _iota((rows, nk), 1) < _iota((rows, nk), 0)
        pv, run = block(q[r0:], q0 + r0, run_ref[r0:, :], mask)
        acc_ref[r0:, :] += pv
        run_ref[r0:, :] = run

    def body(g, carry):
        run = run_ref[...]
        total = None
        for u in range(SB_UNROLL):
            start = pl.multiple_of(q0 - (g * SB_UNROLL + u + 1) * nk, nk)
            pv, run = block(q, start, run, None)
            total = pv if total is None else total + pv
        acc_ref[...] += total
        run_ref[...] = run
        return carry

    lax.fori_loop(0, (i * tq) // (SB_UNROLL * nk), body, 0)
    o_ref[...] = acc_ref[...].astype(o_ref.dtype)


def _sb_prompt(proj, bias, nb, t, tq=512):
    assert t % tq == 0 and tq % (SB_UNROLL * SB_KEYS) == 0
    nq = t // tq
    hb = SB_HEADS
    return pl.pallas_call(
        functools.partial(_sb_prompt_kernel, tq=tq, scale=SB_HEAD_DIM ** -0.5),
        grid=(nb, hb, nq),
        in_specs=[pl.BlockSpec(memory_space=pltpu.SMEM),
                  pl.BlockSpec((tq, SB_HEAD_DIM), lambda b, h, i: (b * nq + i, h)),
                  pl.BlockSpec((t, SB_HEAD_DIM), lambda b, h, i: (b, hb + h)),
                  pl.BlockSpec((t, SB_HEAD_DIM), lambda b, h, i: (b, 2 * hb + h))],
        out_specs=pl.BlockSpec((tq, SB_HEAD_DIM), lambda b, h, i: (b * nq + i, h)),
        out_shape=jax.ShapeDtypeStruct((nb * t, D_SB), BF16),
        scratch_shapes=[pltpu.VMEM((tq, SB_HEAD_DIM), F32), pltpu.VMEM((tq, LANES), F32)],
        compiler_params=_cparams(("arbitrary", "arbitrary", "arbitrary")),
    )(bias, proj, proj, proj)


def _sb_sample_kernel(pt_ref, qbd_ref, bias_ref, kown_ref, vown_ref, *refs, pp, nq):
    del pt_ref
    kp = refs[:pp]
    vp = refs[pp:2 * pp]
    o_ref = refs[2 * pp]
    acc_ref, run_ref = refs[2 * pp + 1:]
    s = pl.program_id(1)
    n = PAGE_SIZE
    rows_c = 2 * nq
    earlier = (_iota((n, n), 1) > _iota((n, n), 0)).astype(BF16)
    qbd = qbd_ref[...]
    bias = bias_ref[...]

    def head_rows(ref, hh):
        return ref[pl.ds(hh, n, stride=SB_HEADS), :].astype(BF16)

    def scores(k_ref, mask):
        kcat = jnp.concatenate([head_rows(k_ref, hh) for hh in range(SB_HEADS)], axis=1)
        z = _dot(kcat, qbd) + bias
        sp = _softplus(z)
        lg = -sp if mask is None else jnp.where(mask, -sp, 0.0)
        return z - sp, _dot(earlier, lg.astype(BF16)), jnp.sum(lg, axis=0, keepdims=True)

    def add_values(v_ref, w, acc):
        wt = w.T.astype(BF16)
        return [acc[hh] + _dot(wt[(hh // 2) * rows_c:(hh // 2 + 1) * rows_c, :], head_rows(v_ref, hh))
                for hh in range(SB_HEADS)]

    @pl.when(s == 0)
    def _():
        mask = _iota((n, LANES), 0) < (_iota((n, LANES), 1) % nq)
        log_beta, local, total = scores(kown_ref, mask)
        w = jnp.where(mask, jnp.exp(log_beta + local), 0.0)
        acc = add_values(vown_ref, w, [jnp.zeros((rows_c, SB_HEAD_DIM), F32)] * SB_HEADS)
        for hh in range(SB_HEADS):
            acc_ref[hh] = acc[hh]
        run_ref[...] = total

    page_scores = [scores(kp[c], None) for c in range(pp)]
    run = run_ref[...]
    acc = [acc_ref[hh] for hh in range(SB_HEADS)]
    for c in range(pp):
        log_beta, local, total = page_scores[c]
        acc = add_values(vp[c], jnp.exp(log_beta + local + run), acc)
        run = run + total
    run_ref[...] = run
    for hh in range(SB_HEADS):
        acc_ref[hh] = acc[hh]

    @pl.when(s == pl.num_programs(1) - 1)
    def _():
        for hh in range(SB_HEADS):
            e = hh % 2
            o_ref[:, hh * SB_HEAD_DIM:(hh + 1) * SB_HEAD_DIM] = acc_ref[hh, e * nq:(e + 1) * nq, :]


def _sb_sample(q, k_own, v_own, bias, cache_k, cache_v, page_table, layer, pp=4):
    db, nq, _ = q.shape
    assert 2 * nq == 16 and SB_HEADS * nq <= LANES and cache_k.shape[2:] == (PAGE_SIZE, SB_HEADS, SB_HEAD_DIM)
    depth, n_pool = cache_k.shape[0], cache_k.shape[1]
    n_pages = page_table.shape[1]
    scale = SB_HEAD_DIM ** -0.5
    qh = (q * scale).reshape(db, nq, SB_HEADS, SB_HEAD_DIM)
    eye = jnp.eye(SB_HEADS, dtype=F32)
    qbd = jnp.einsum('bthd,hg->bhdgt', qh, eye).reshape(db, D_SB, SB_HEADS * nq)
    qbd = jnp.pad(qbd, ((0, 0), (0, 0), (0, LANES - SB_HEADS * nq))).astype(BF16)
    bias_l = jnp.pad(jnp.repeat(bias.astype(F32), nq), (0, LANES - SB_HEADS * nq)).reshape(1, LANES)
    pad = ((0, 0), (0, PAGE_SIZE - nq), (0, 0))
    page_rows = PAGE_SIZE * SB_HEADS
    kown = jnp.pad(k_own, pad).reshape(db * page_rows, SB_HEAD_DIM)
    vown = jnp.pad(v_own, pad).reshape(db * page_rows, SB_HEAD_DIM)
    ck = cache_k.reshape(depth * n_pool * page_rows, SB_HEAD_DIM)
    cv = cache_v.reshape(depth * n_pool * page_rows, SB_HEAD_DIM)
    base = layer * n_pool

    def page_map(c):
        return lambda b, s, pt: (base + pt[b, n_pages - 1 - (s * pp + c)], 0)

    page_spec = [pl.BlockSpec((page_rows, SB_HEAD_DIM), page_map(c)) for c in range(pp)]
    own_spec = pl.BlockSpec((page_rows, SB_HEAD_DIM), lambda b, s, pt: (b, 0))
    grid_spec = pltpu.PrefetchScalarGridSpec(
        num_scalar_prefetch=1,
        grid=(db, n_pages // pp),
        in_specs=[pl.BlockSpec((None, D_SB, LANES), lambda b, s, pt: (b, 0, 0)),
                  pl.BlockSpec((1, LANES), lambda b, s, pt: (0, 0)),
                  own_spec, own_spec] + page_spec + page_spec,
        out_specs=pl.BlockSpec((nq, D_SB), lambda b, s, pt: (b, 0)),
        scratch_shapes=[pltpu.VMEM((SB_HEADS, 2 * nq, SB_HEAD_DIM), F32),
                        pltpu.VMEM((1, LANES), F32)],
    )
    return pl.pallas_call(
        functools.partial(_sb_sample_kernel, pp=pp, nq=nq),
        grid_spec=grid_spec,
        out_shape=jax.ShapeDtypeStruct((db * nq, D_SB), F32),
        compiler_params=_cparams(("arbitrary", "arbitrary")),
    )(page_table, qbd, bias_l, kown, vown, *([ck] * pp), *([cv] * pp))


def _ssd_kernel(*refs, tb, has_init):
    if has_init:
        (z0_ref, z1_ref, xa_ref, xb_ref, bc_ref, dt_ref, ctx_ref, h0_ref, cw_ref, cb_ref, dtb_ref, alog_ref,
         dsk_ref, ng_ref, y_ref, hT_ref, xp_ref, hs_ref, yb_ref) = refs
    else:
        (z0_ref, z1_ref, xa_ref, xb_ref, bc_ref, dt_ref, ctx_ref, cw_ref, cb_ref, dtb_ref, alog_ref,
         dsk_ref, ng_ref, y_ref, hT_ref, xp_ref, hs_ref, yb_ref) = refs
        h0_ref = None
    c = pl.program_id(1)
    L = TILE
    P2 = LANES
    n_pairs = D_SSM // P2

    @pl.when(c == 0)
    def _():
        xp_ref[0:8, :] = ctx_ref[...]
        if has_init:
            hs_ref[...] = h0_ref[...]
        else:
            hs_ref[...] = jnp.zeros_like(hs_ref)

    xp_ref[8:8 + tb, 0:1024] = xa_ref[...]
    xp_ref[8:8 + tb, 1024:2048] = xb_ref[...]
    xp_ref[8:8 + tb, 2048:3072] = bc_ref[...]
    if tb < L:
        xp_ref[8 + tb:8 + L, :] = jnp.zeros((L - tb, SSM_CONV_DIM), F32)

    conv = cb_ref[...] + cw_ref[0:1, :] * xp_ref[5:5 + L, :]
    for i in range(1, SSM_CONV):
        conv = conv + cw_ref[i:i + 1, :] * xp_ref[5 + i:5 + i + L, :]
    xp_ref[0:8, :] = xp_ref[tb:tb + 8, :]
    xc = _silu(conv)

    row = _iota((L, LANES), 0)
    lane = _iota((L, LANES), 1)
    dt_raw = dt_ref[...]
    if tb < L:
        dt_raw = jnp.concatenate([dt_raw, jnp.zeros((L - tb, LANES), F32)], axis=0)
    valid = (lane < SSM_HEADS) & (row < tb)
    dtv = jnp.where(valid, _softplus(dt_raw + dtb_ref[...]), 0.0)
    a = dtv * (-jnp.exp(alog_ref[...]))
    lower = (_iota((L, L), 0) >= _iota((L, L), 1))
    lower_b = lower.astype(BF16)
    upper_b = (_iota((L, L), 0) <= _iota((L, L), 1)).astype(BF16)
    cum = _sel_dot_l(lower_b, a)
    cum_t = _sel_dot_r(a.T, upper_b)
    last = cum[L - 1:L, :]
    spread = (_iota((LANES, D_SSM), 0) == _iota((LANES, D_SSM), 1) // SSM_HEAD_DIM).astype(BF16)
    spread_t = (_iota((D_SSM, LANES), 0) // SSM_HEAD_DIM == _iota((D_SSM, LANES), 1)).astype(BF16)
    dt_x = _sel_dot_r(dtv, spread)
    ecum_x = _sel_dot_r(jnp.exp(cum), spread)
    wdec_x = _sel_dot_r(jnp.exp(last - cum), spread)
    sdec = _sel_dot_l(spread_t, jnp.broadcast_to(jnp.exp(cum_t[:, L - 1:L]), (LANES, LANES)))

    first_half = lane < SSM_HEAD_DIM
    pairs_per_group = n_pairs // SSM_GROUPS
    for g in range(SSM_GROUPS):
        bg = xc[:, D_SSM + g * SSM_STATE:D_SSM + (g + 1) * SSM_STATE]
        cg = xc[:, D_SSM + (SSM_GROUPS + g) * SSM_STATE:D_SSM + (SSM_GROUPS + g + 1) * SSM_STATE]
        bg_b = bg.astype(BF16)
        cg_b = cg.astype(BF16)
        cb = _dot_nt(cg_b, bg_b)
        for r in range(pairs_per_group):
            p = g * pairs_per_group + r
            sl = slice(p * P2, (p + 1) * P2)
            x_p = xc[:, sl]
            xs_p = x_p * dt_x[:, sl]
            xs_b = xs_p.astype(BF16)
            ys = []
            for e in range(2):
                hh = 2 * p + e
                seg = jnp.broadcast_to(cum[:, hh:hh + 1], (L, L)) - jnp.broadcast_to(cum_t[hh:hh + 1, :], (L, L))
                dec = jnp.where(lower, jnp.exp(jnp.minimum(seg, 0.0)), 0.0)
                ys.append(_dot((cb * dec).astype(BF16), xs_b))
            y_p = jnp.where(first_half, ys[0], ys[1])
            h_p = hs_ref[sl, :]
            y_p = y_p + _dot_nt(cg_b, h_p.astype(BF16)) * ecum_x[:, sl]
            xw = (xs_p * wdec_x[:, sl]).astype(BF16)
            hs_ref[sl, :] = sdec[sl, :] * h_p + _dot(xw.T, bg_b)
            yb_ref[:, sl] = y_p + dsk_ref[:, sl] * x_p

    zfull = jnp.concatenate([z0_ref[...], z1_ref[...]], axis=1)
    y = yb_ref[0:tb, :] * _silu(zfull)
    gw = D_SSM // SSM_GROUPS
    for g in range(SSM_GROUPS):
        yg = y[:, g * gw:(g + 1) * gw]
        yn = yg * lax.rsqrt(jnp.mean(yg * yg, axis=-1, keepdims=True) + EPS)
        y_ref[:, g * gw:(g + 1) * gw] = (yn * ng_ref[:, g * gw:(g + 1) * gw]).astype(y_ref.dtype)

    @pl.when(c == pl.num_programs(1) - 1)
    def _():
        hT_ref[...] = hs_ref[...]


def _ssd(proj, row0, nb, t, ctx8, h0, conv_w, conv_b, dt_bias, a_log, d_skip, norm_g, out_dtype):
    tb = min(t, TILE)
    nc = t // tb
    rb0 = row0 // tb
    has_init = h0 is not None

    def colblk(width, idx):
        return pl.BlockSpec((tb, width), lambda b, c: (rb0 + b * nc + c, idx))

    def full(shape):
        return pl.BlockSpec(shape, lambda b, c: (0,) * len(shape))

    in_specs = [colblk(1024, 3), colblk(1024, 4), colblk(1024, 5), colblk(1024, 6), colblk(1024, 7),
                colblk(LANES, 64),
                pl.BlockSpec((None, 8, SSM_CONV_DIM), lambda b, c: (b, 0, 0))]
    args = [proj] * 6 + [ctx8]
    if has_init:
        in_specs.append(pl.BlockSpec((None, D_SSM, SSM_STATE), lambda b, c: (b, 0, 0)))
        args.append(h0)
    padl = (0, LANES - SSM_HEADS)
    in_specs += [full((SSM_CONV, SSM_CONV_DIM)), full((1, SSM_CONV_DIM)), full((1, LANES)), full((1, LANES)),
                 full((1, D_SSM)), full((1, D_SSM))]
    args += [conv_w, conv_b.reshape(1, -1), jnp.pad(dt_bias, padl).reshape(1, LANES),
             jnp.pad(a_log, padl).reshape(1, LANES), jnp.repeat(d_skip, SSM_HEAD_DIM).reshape(1, D_SSM),
             norm_g.reshape(1, D_SSM)]
    return pl.pallas_call(
        functools.partial(_ssd_kernel, tb=tb, has_init=has_init),
        grid=(nb, nc),
        in_specs=in_specs,
        out_specs=[pl.BlockSpec((tb, D_SSM), lambda b, c: (b * nc + c, 0)),
                   pl.BlockSpec((None, D_SSM, SSM_STATE), lambda b, c: (b, 0, 0))],
        out_shape=[jax.ShapeDtypeStruct((nb * t, D_SSM), out_dtype),
                   jax.ShapeDtypeStruct((nb, D_SSM, SSM_STATE), F32)],
        scratch_shapes=[pltpu.VMEM((8 + TILE, SSM_CONV_DIM), F32),
                        pltpu.VMEM((D_SSM, SSM_STATE), F32),
                        pltpu.VMEM((TILE, D_SSM), F32)],
        compiler_params=_cparams(("arbitrary", "arbitrary")),
    )(*args)


def _gla_kernel(*refs, tb, has_init):
    if has_init:
        (q_ref, k_ref, v_ref, r_ref, gk_ref, s0_ref, gw_ref, gb_ref, ng_ref, o_ref, sT_ref, st_ref) = refs
    else:
        (q_ref, k_ref, v_ref, r_ref, gk_ref, gw_ref, gb_ref, ng_ref, o_ref, sT_ref, st_ref) = refs
        s0_ref = None
    c = pl.program_id(1)
    L = TILE
    CH = min(GLA_CHUNK, tb)
    n_sub = max(tb // CH, 1)
    KP = GLA_HEADS * GLA_K_DIM

    @pl.when(c == 0)
    def _():
        if has_init:
            st_ref[...] = s0_ref[...]
        else:
            st_ref[...] = jnp.zeros_like(st_ref)

    def padrows(x):
        if tb < L:
            return jnp.concatenate([x, jnp.zeros((L - tb, x.shape[1]), F32)], axis=0)
        return x

    row = _iota((L, L), 0)
    col = _iota((L, L), 1)
    same_chunk = (row // CH) == (col // CH)
    lower = (row >= col) & same_chunk
    lower_b = lower.astype(BF16)
    rvalid = _iota((L, KP), 0) < tb
    glog = _dot(gk_ref[...].astype(BF16), gw_ref[...].astype(BF16)) + gb_ref[...]
    glog = padrows(-_softplus(-glog) * (1.0 / GLA_TAU))
    glog = jnp.where(rvalid, glog, 0.0)
    q_all = padrows(q_ref[...]) * (GLA_K_DIM ** -0.5)
    k_all = jnp.where(rvalid, padrows(k_ref[...]), 0.0)
    v_all = padrows(v_ref[...])
    lane = _iota((L, LANES), 1)
    rowl = _iota((L, LANES), 0)

    for p in range(KP // LANES):
        sl = slice(p * LANES, (p + 1) * LANES)
        g_p = glog[:, sl]
        b_p = _sel_dot_l(lower_b, g_p)
        qt = q_all[:, sl] * jnp.exp(b_p)
        kt = (k_all[:, sl] * jnp.exp(-b_p)).astype(BF16)
        lasts = [b_p[min((j + 1) * CH, L) - 1:min((j + 1) * CH, L), :] for j in range(n_sub)]
        s_cur = st_ref[sl, :]
        s_list = [s_cur]
        kw_t = []
        for j in range(n_sub):
            in_j = (rowl // CH) == j
            kw = jnp.where(in_j, k_all[:, sl] * jnp.exp(lasts[j] - b_p), 0.0)
            kw_t.append(kw.T.astype(BF16))
        heads_out = []
        new_states = [[None, None] for _ in range(n_sub)]
        for e in range(2):
            hh = 2 * p + e
            own = (lane // GLA_K_DIM) == e
            qm = jnp.where(own, qt, 0.0).astype(BF16)
            att = jnp.where(lower, _dot_nt(qm, kt), 0.0)
            v_h = v_all[:, hh * GLA_V_DIM:(hh + 1) * GLA_V_DIM].astype(BF16)
            heads_out.append((qm, _dot(att.astype(BF16), v_h), v_h))
        decs = []
        for j in range(n_sub):
            dcol = jnp.exp(lasts[j]).reshape(1, LANES)
            decs.append(jnp.broadcast_to(dcol, (LANES, LANES)).T)
        for j in range(n_sub):
            upd0 = _dot(kw_t[j], heads_out[0][2])
            upd1 = _dot(kw_t[j], heads_out[1][2])
            upd = jnp.where(_iota((LANES, GLA_V_DIM), 0) < GLA_K_DIM, upd0, upd1)
            s_list.append(decs[j] * s_list[j] + upd)
        st_ref[sl, :] = s_list[n_sub]
        for e in range(2):
            hh = 2 * p + e
            qm, o_h, _ = heads_out[e]
            inter = _dot(qm, s_list[0].astype(BF16))
            for j in range(1, n_sub):
                inter = jnp.where((rowl // CH) == j, _dot(qm, s_list[j].astype(BF16)), inter)
            o_h = o_h + inter
            o_h = o_h[0:tb, :]
            on = o_h * lax.rsqrt(jnp.mean(o_h * o_h, axis=-1, keepdims=True) + EPS) * ng_ref[...]
            vs = slice(hh * GLA_V_DIM, (hh + 1) * GLA_V_DIM)
            o_ref[:, vs] = (on * _silu(r_ref[:, vs])).astype(o_ref.dtype)

    @pl.when(c == pl.num_programs(1) - 1)
    def _():
        sT_ref[...] = st_ref[...]


def _gla(q, k, v, r, gk_low, nb, t, s0, gk_w, gk_b, norm_g, out_dtype):
    tb = min(t, TILE)
    nc = t // tb
    has_init = s0 is not None
    KP = GLA_HEADS * GLA_K_DIM

    def rows(width):
        return pl.BlockSpec((tb, width), lambda b, c: (b * nc + c, 0))

    def full(shape):
        return pl.BlockSpec(shape, lambda b, c: (0,) * len(shape))

    in_specs = [rows(KP), rows(KP), rows(D_GLA), rows(D_GLA), rows(GLA_RANK)]
    args = [q, k, v, r, gk_low]
    if has_init:
        in_specs.append(pl.BlockSpec((None, KP, GLA_V_DIM), lambda b, c: (b, 0, 0)))
        args.append(s0)
    in_specs += [full((GLA_RANK, KP)), full((1, KP)), full((1, GLA_V_DIM))]
    args += [gk_w, gk_b.reshape(1, KP), norm_g.reshape(1, GLA_V_DIM)]
    return pl.pallas_call(
        functools.partial(_gla_kernel, tb=tb, has_init=has_init),
        grid=(nb, nc),
        in_specs=in_specs,
        out_specs=[rows(D_GLA), pl.BlockSpec((None, KP, GLA_V_DIM), lambda b, c: (b, 0, 0))],
        out_shape=[jax.ShapeDtypeStruct((nb * t, D_GLA), out_dtype),
                   jax.ShapeDtypeStruct((nb, KP, GLA_V_DIM), F32)],
        scratch_shapes=[pltpu.VMEM((KP, GLA_V_DIM), F32)],
        compiler_params=_cparams(("arbitrary", "arbitrary")),
    )(*args)


def kernel(x_prompt, x_sample, cache_k, cache_v, page_table, state_conv, state_ssm, state_gla, norm1_g, w_in, sb_bias, conv_w, conv_b, dt_bias, a_log, d_skip, ssm_norm_g, gla_gk_w, gla_gk_b, gla_norm_g, w_out, norm2_g, w_gate, w_up, w_down, final_norm_g):
    bp, tp, d = x_prompt.shape
    bs, ts, _ = x_sample.shape
    depth = w_in.shape[0]
    mp = bp * tp
    ms = bs * ts
    m = mp + ms
    tm = _row_tile(m, 1376)
    tm_s = _row_tile(m, 688)
    d_ff = w_gate.shape[2]
    KP = GLA_HEADS * GLA_K_DIM
    off_xbc = 3 * D_SB + D_SSM
    off_gla = off_xbc + SSM_CONV_DIM + SSM_HEADS
    n_tail = SSM_CONV - 1

    x = jnp.concatenate([x_prompt.reshape(mp, d), x_sample.reshape(ms, d)], axis=0)
    w_down_bf = _cast_bf16(w_down, _row_tile(d_ff, 688))
    w_in_nk = jnp.swapaxes(w_in, 1, 2)
    outs = {k: [] for k in ("kp", "vp", "cp", "sp", "gp", "ks", "vs", "cs", "ss", "gs")}

    for l in range(depth):
        xn = _rmsnorm(x, norm1_g[l], BF16, tm_s)
        proj = _matmul(xn, w_in_nk, l, None, tm, 512, w_is_nk=True)

        k_rows = proj[:, D_SB:2 * D_SB]
        v_rows = proj[:, 2 * D_SB:3 * D_SB]
        gq = proj[:, off_gla:off_gla + KP]
        gk = proj[:, off_gla + KP:off_gla + 2 * KP]
        gv = proj[:, off_gla + 2 * KP:off_gla + 2 * KP + D_GLA]
        gr = proj[:, off_gla + 2 * KP + D_GLA:off_gla + 2 * KP + 2 * D_GLA]
        glow = proj[:, off_gla + 2 * KP + 2 * D_GLA:]

        o_sb_p = _sb_prompt(proj, sb_bias[l], bp, tp)
        ctx_p = jnp.zeros((bp, 8, SSM_CONV_DIM), F32)
        y_ssm_p, ssm_p = _ssd(proj, 0, bp, tp, ctx_p, None, conv_w[l], conv_b[l], dt_bias[l], a_log[l],
                              d_skip[l], ssm_norm_g[l], BF16)
        o_gl_p, gla_p = _gla(gq[:mp], gk[:mp], gv[:mp], gr[:mp], glow[:mp], bp, tp, None,
                             gla_gk_w[l], gla_gk_b[l], gla_norm_g[l], BF16)

        q_s = proj[mp:, 0:D_SB].reshape(bs, ts, D_SB)
        k_s = k_rows[mp:].reshape(bs, ts, D_SB)
        v_s = v_rows[mp:].reshape(bs, ts, D_SB)
        o_sb_s = _sb_sample(q_s, k_s, v_s, sb_bias[l], cache_k, cache_v, page_table, l)
        ctx_s = jnp.pad(state_conv[l], ((0, 0), (8 - (SSM_CONV - 1), 0), (0, 0)))
        y_ssm_s, ssm_s = _ssd(proj, mp, bs, ts, ctx_s, state_ssm[l].reshape(bs, D_SSM, SSM_STATE),
                              conv_w[l], conv_b[l], dt_bias[l], a_log[l], d_skip[l], ssm_norm_g[l], F32)
        o_gl_s, gla_s = _gla(gq[mp:], gk[mp:], gv[mp:], gr[mp:], glow[mp:], bs, ts,
                             state_gla[l].reshape(bs, KP, GLA_V_DIM),
                             gla_gk_w[l], gla_gk_b[l], gla_norm_g[l], F32)

        mixed = jnp.concatenate(
            [jnp.concatenate([o_sb_p, y_ssm_p, o_gl_p], axis=1),
             jnp.concatenate([o_sb_s, y_ssm_s, o_gl_s], axis=1).astype(BF16)], axis=0)
        x = _matmul(mixed, w_out, l, x, tm, 512)
        hn = _rmsnorm(x, norm2_g[l], BF16, tm_s)
        hmid = _gate_up(hn, w_gate, w_up, l, tm, 256)
        x = _matmul_bf16w(hmid, w_down_bf, l, x, tm_s, 512)

        xbc_p = jnp.stack([proj[(b + 1) * tp - n_tail:(b + 1) * tp, off_xbc:off_xbc + SSM_CONV_DIM]
                           for b in range(bp)])
        xbc_s = jnp.concatenate([state_conv[l], proj[mp:, off_xbc:off_xbc + SSM_CONV_DIM].reshape(bs, ts, -1)], axis=1)
        outs["kp"].append(k_rows[:mp].reshape(bp, tp, SB_HEADS, SB_HEAD_DIM))
        outs["vp"].append(v_rows[:mp].reshape(bp, tp, SB_HEADS, SB_HEAD_DIM))
        outs["cp"].append(xbc_p)
        outs["sp"].append(ssm_p.reshape(bp, SSM_HEADS, SSM_HEAD_DIM, SSM_STATE))
        outs["gp"].append(gla_p.reshape(bp, GLA_HEADS, GLA_K_DIM, GLA_V_DIM))
        outs["ks"].append(k_s.reshape(bs, ts, SB_HEADS, SB_HEAD_DIM))
        outs["vs"].append(v_s.reshape(bs, ts, SB_HEADS, SB_HEAD_DIM))
        outs["cs"].append(xbc_s[:, xbc_s.shape[1] - n_tail:])
        outs["ss"].append(ssm_s.reshape(bs, SSM_HEADS, SSM_HEAD_DIM, SSM_STATE))
        outs["gs"].append(gla_s.reshape(bs, GLA_HEADS, GLA_K_DIM, GLA_V_DIM))

    y_p = _rmsnorm(x, final_norm_g, F32, _row_tile(mp, 512), rows=mp)
    y_s = _rmsnorm(x[mp:], final_norm_g, F32, ms)
    st = {k: jnp.stack(v) for k, v in outs.items()}
    return (y_p.reshape(bp, tp, d), y_s.reshape(bs, ts, d),
            st["kp"], st["vp"], st["cp"], st["sp"], st["gp"],
            st["ks"], st["vs"], st["cs"], st["ss"], st["gs"])
```

```python
import functools

import jax
import jax.numpy as jnp
from jax import lax
from jax.experimental import pallas as pl
from jax.experimental.pallas import tpu as pltpu

F32 = jnp.float32
BF16 = jnp.bfloat16

EPS = 1e-6
SB_HEADS = 8
SB_HEAD_DIM = 128
D_SB = SB_HEADS * SB_HEAD_DIM
SSM_HEADS = 32
SSM_HEAD_DIM = 64
D_SSM = SSM_HEADS * SSM_HEAD_DIM
SSM_STATE = 128
SSM_GROUPS = 4
SSM_CONV = 4
SSM_CONV_DIM = D_SSM + 2 * SSM_GROUPS * SSM_STATE
GLA_HEADS = 8
GLA_K_DIM = 64
GLA_V_DIM = 128
D_GLA = GLA_HEADS * GLA_V_DIM
GLA_RANK = 16
GLA_TAU = 16.0
GLA_CHUNK = 64
PAGE_SIZE = 128

LANES = 128
TILE = 128
VMEM_LIMIT = 60 * 1024 * 1024


def _cparams(sem):
    return pltpu.CompilerParams(dimension_semantics=sem, vmem_limit_bytes=VMEM_LIMIT)


def _softplus(z):
    return jnp.maximum(z, 0.0) + jnp.log(1.0 + jnp.exp(-jnp.abs(z)))


def _silu(x):
    return x * (0.5 + 0.5 * jnp.tanh(0.5 * x))


def _split2(d):
    d0 = d.astype(BF16)
    d1 = (d - d0.astype(F32)).astype(BF16)
    return d0, d1


def _dot(a, b):
    return jnp.dot(a, b, preferred_element_type=F32)


def _dot_nt(a, b):
    return lax.dot_general(a, b, (((1,), (1,)), ((), ())), preferred_element_type=F32)


def _sel_dot_r(data, sel):
    d0, d1 = _split2(data)
    return _dot(d0, sel) + _dot(d1, sel)


def _sel_dot_l(sel, data):
    d0, d1 = _split2(data)
    return _dot(sel, d0) + _dot(sel, d1)


def _iota(shape, dim):
    return lax.broadcasted_iota(jnp.int32, shape, dim)


def _rmsnorm_kernel(x_ref, g_ref, o_ref):
    x = x_ref[...]
    y = x * lax.rsqrt(jnp.mean(x * x, axis=-1, keepdims=True) + EPS)
    o_ref[...] = (y * g_ref[...]).astype(o_ref.dtype)


def _row_tile(m, target):
    best = None
    for t in range(16, target + 1, 16):
        if m % t == 0:
            best = t
    assert best is not None, (m, target)
    return best


def _rmsnorm(x, g, out_dtype, tm, rows=None):
    m, d = x.shape
    m = m if rows is None else rows
    return pl.pallas_call(
        _rmsnorm_kernel,
        grid=(m // tm,),
        in_specs=[pl.BlockSpec((tm, d), lambda i: (i, 0)), pl.BlockSpec((1, d), lambda i: (0, 0))],
        out_specs=pl.BlockSpec((tm, d), lambda i: (i, 0)),
        out_shape=jax.ShapeDtypeStruct((m, d), out_dtype),
        compiler_params=_cparams(("arbitrary",)),
    )(x, g.reshape(1, d))


def _mm_kernel(x_ref, w_ref, *rest, has_res, w_is_nk, k_order):
    if has_res:
        res_ref, o_ref, wbf_ref = rest
    else:
        o_ref, wbf_ref = rest

    @pl.when(pl.program_id(1) == 0)
    def _():
        if k_order is None:
            wbf_ref[...] = w_ref[...].astype(BF16)
        else:
            off = 0
            for a, b in k_order:
                wbf_ref[off:off + b - a, :] = w_ref[a:b, :].astype(BF16)
                off += b - a

    acc = _dot_nt(x_ref[...], wbf_ref[...]) if w_is_nk else _dot(x_ref[...], wbf_ref[...])
    if has_res:
        acc = acc + res_ref[...]
    o_ref[...] = acc.astype(o_ref.dtype)


def _matmul(x, w, layer, res, tm, tn, w_is_nk=False, k_order=None):
    m, k = x.shape
    assert k_order is None or (not w_is_nk and sum(b - a for a, b in k_order) == k)
    if w_is_nk:
        n = w.shape[1]
        w_spec = pl.BlockSpec((None, tn, k), lambda j, i: (layer, j, 0))
        w_scratch = pltpu.VMEM((tn, k), BF16)
    else:
        n = w.shape[2]
        w_spec = pl.BlockSpec((None, k, tn), lambda j, i: (layer, 0, j))
        w_scratch = pltpu.VMEM((k, tn), BF16)
    in_specs = [pl.BlockSpec((tm, k), lambda j, i: (i, 0)), w_spec]
    args = [x, w]
    if res is not None:
        in_specs.append(pl.BlockSpec((tm, tn), lambda j, i: (i, j)))
        args.append(res)
    return pl.pallas_call(
        functools.partial(_mm_kernel, has_res=res is not None, w_is_nk=w_is_nk, k_order=k_order),
        grid=(pl.cdiv(n, tn), m // tm),
        in_specs=in_specs,
        out_specs=pl.BlockSpec((tm, tn), lambda j, i: (i, j)),
        out_shape=jax.ShapeDtypeStruct((m, n), F32),
        scratch_shapes=[w_scratch],
        compiler_params=_cparams(("arbitrary", "arbitrary")),
    )(*args)


MM_SLABS = 4


def _slabs(rows, n):
    step = -(-rows // (16 * n)) * 16
    bounds = [min(i * step, rows) for i in range(n + 1)]
    return [(a, b) for a, b in zip(bounds[:-1], bounds[1:]) if b > a]


def _gate_up_kernel(x_ref, wg_ref, wu_ref, o_ref, wbf_ref, *, tn):
    @pl.when(pl.program_id(1) == 0)
    def _():
        wbf_ref[:, :tn] = wg_ref[...].astype(BF16)
        wbf_ref[:, tn:] = wu_ref[...].astype(BF16)

    for r0, r1 in _slabs(o_ref.shape[0], MM_SLABS):
        gu = _dot(x_ref[r0:r1, :], wbf_ref[...])
        o_ref[r0:r1, :] = (_silu(gu[:, :tn]) * gu[:, tn:]).astype(o_ref.dtype)


def _gate_up(x, wg, wu, layer, tm, tn):
    m, k = x.shape
    f = wg.shape[2]
    wspec = pl.BlockSpec((None, k, tn), lambda j, i: (layer, 0, j))
    return pl.pallas_call(
        functools.partial(_gate_up_kernel, tn=tn),
        grid=(f // tn, m // tm),
        in_specs=[pl.BlockSpec((tm, k), lambda j, i: (i, 0)), wspec, wspec],
        out_specs=pl.BlockSpec((tm, tn), lambda j, i: (i, j)),
        out_shape=jax.ShapeDtypeStruct((m, f), BF16),
        scratch_shapes=[pltpu.VMEM((k, 2 * tn), BF16)],
        compiler_params=_cparams(("arbitrary", "arbitrary")),
    )(x, wg, wu)


def _cast_kernel(w_ref, o_ref):
    o_ref[...] = w_ref[...].astype(o_ref.dtype)


def _cast_bf16(w, tr):
    nl, r, c = w.shape
    spec = pl.BlockSpec((None, tr, c), lambda l, i: (l, i, 0))
    return pl.pallas_call(
        _cast_kernel,
        grid=(nl, r // tr),
        in_specs=[spec],
        out_specs=spec,
        out_shape=jax.ShapeDtypeStruct(w.shape, BF16),
        compiler_params=_cparams(("arbitrary", "arbitrary")),
    )(w)


def _mm_res_kernel(x_ref, w_ref, res_ref, o_ref):
    o_ref[...] = _dot(x_ref[...], w_ref[...]) + res_ref[...]


def _matmul_bf16w(x, w, layer, res, tm, tn):
    m, k = x.shape
    n = w.shape[2]
    return pl.pallas_call(
        _mm_res_kernel,
        grid=(m // tm, n // tn),
        in_specs=[pl.BlockSpec((tm, k), lambda i, j: (i, 0), pipeline_mode=pl.Buffered(1)),
                  pl.BlockSpec((None, k, tn), lambda i, j: (layer, 0, j)),
                  pl.BlockSpec((tm, tn), lambda i, j: (i, j))],
        out_specs=pl.BlockSpec((tm, tn), lambda i, j: (i, j)),
        out_shape=jax.ShapeDtypeStruct((m, n), F32),
        compiler_params=_cparams(("arbitrary", "arbitrary")),
    )(x, w, res)


SB_KEYS = 256
SB_UNROLL = 2


def _sb_prompt_kernel(bias_ref, q_ref, k_ref, v_ref, buf_ref, o_ref, acc_ref, run_ref, *, tq, scale):
    del buf_ref
    h = pl.program_id(1)
    i = pl.program_id(2)
    nk = SB_KEYS
    bias = bias_ref[h]
    q = (q_ref[...] * scale).astype(BF16)
    later = _iota((nk, nk), 0) > _iota((nk, nk), 1)
    suffix_total = jnp.concatenate([later.astype(BF16), jnp.ones((nk, LANES), BF16)], axis=1)

    def block(qv, start, run, mask):
        kb = k_ref[pl.ds(start, nk), :].astype(BF16)
        vb = v_ref[pl.ds(start, nk), :].astype(BF16)
        z = _dot_nt(qv, kb) + bias
        sp = _softplus(z)
        lg = -sp if mask is None else jnp.where(mask, -sp, 0.0)
        st = _dot(lg.astype(BF16), suffix_total)
        e = jnp.exp(z - sp + st[:, :nk] + jnp.concatenate([run] * (nk // LANES), axis=1))
        w = e if mask is None else jnp.where(mask, e, 0.0)
        return _dot(w.astype(BF16), vb), run + st[:, nk:]

    acc_ref[...] = jnp.zeros_like(acc_ref)
    run_ref[...] = jnp.zeros_like(run_ref)
    q0 = pl.multiple_of(i * tq, tq)
    for jd in reversed(range(tq // nk)):
        r0 = jd * nk
        rows = tq - r0
        mask = _iota((rows, nk), 1) < _iota((rows, nk), 0)
        pv, run = block(q[r0:], q0 + r0, run_ref[r0:, :], mask)
        acc_ref[r0:, :] += pv
        run_ref[r0:, :] = run

    def body(g, carry):
        run = run_ref[...]
        total = None
        for u in range(SB_UNROLL):
            start = pl.multiple_of(q0 - (g * SB_UNROLL + u + 1) * nk, nk)
            pv, run = block(q, start, run, None)
            total = pv if total is None else total + pv
        acc_ref[...] += total
        run_ref[...] = run
        return carry

    lax.fori_loop(0, (i * tq) // (SB_UNROLL * nk), body, 0)
    o_ref[...] = acc_ref[...].astype(o_ref.dtype)


def _sb_prompt(proj, bias, nb, t, buf, col0, tq=512):
    assert t % tq == 0 and tq % (SB_UNROLL * SB_KEYS) == 0 and col0 % SB_HEAD_DIM == 0
    nq = t // tq
    hb = SB_HEADS
    cb0 = col0 // SB_HEAD_DIM
    return pl.pallas_call(
        functools.partial(_sb_prompt_kernel, tq=tq, scale=SB_HEAD_DIM ** -0.5),
        grid=(nb, hb, nq),
        in_specs=[pl.BlockSpec(memory_space=pltpu.SMEM),
                  pl.BlockSpec((tq, SB_HEAD_DIM), lambda b, h, i: (b * nq + i, h)),
                  pl.BlockSpec((t, SB_HEAD_DIM), lambda b, h, i: (b, hb + h)),
                  pl.BlockSpec((t, SB_HEAD_DIM), lambda b, h, i: (b, 2 * hb + h)),
                  pl.BlockSpec(memory_space=pl.ANY)],
        out_specs=pl.BlockSpec((tq, SB_HEAD_DIM), lambda b, h, i: (b * nq + i, cb0 + h)),
        out_shape=jax.ShapeDtypeStruct(buf.shape, buf.dtype),
        input_output_aliases={4: 0},
        scratch_shapes=[pltpu.VMEM((tq, SB_HEAD_DIM), F32), pltpu.VMEM((tq, LANES), F32)],
        compiler_params=_cparams(("arbitrary", "arbitrary", "arbitrary")),
    )(bias, proj, proj, proj, buf)


def _sb_sample_kernel(pt_ref, qbd_ref, bias_ref, kown_ref, vown_ref, *refs, pp, nq):
    del pt_ref
    kp = refs[:pp]
    vp = refs[pp:2 * pp]
    o_ref = refs[2 * pp]
    acc_ref, run_ref = refs[2 * pp + 1:]
    s = pl.program_id(1)
    n = PAGE_SIZE
    rows_c = 2 * nq
    earlier = (_iota((n, n), 1) > _iota((n, n), 0)).astype(BF16)
    qbd = qbd_ref[...]
    bias = bias_ref[...]

    def head_rows(ref, hh):
        return ref[pl.ds(hh, n, stride=SB_HEADS), :].astype(BF16)

    def scores(k_ref, mask):
        kcat = jnp.concatenate([head_rows(k_ref, hh) for hh in range(SB_HEADS)], axis=1)
        z = _dot(kcat, qbd) + bias
        sp = _softplus(z)
        lg = -sp if mask is None else jnp.where(mask, -sp, 0.0)
        return z - sp, _dot(earlier, lg.astype(BF16)), jnp.sum(lg, axis=0, keepdims=True)

    def add_values(v_ref, w, acc):
        wt = w.T.astype(BF16)
        return [acc[hh] + _dot(wt[(hh // 2) * rows_c:(hh // 2 + 1) * rows_c, :], head_rows(v_ref, hh))
                for hh in range(SB_HEADS)]

    @pl.when(s == 0)
    def _():
        mask = _iota((n, LANES), 0) < (_iota((n, LANES), 1) % nq)
        log_beta, local, total = scores(kown_ref, mask)
        w = jnp.where(mask, jnp.exp(log_beta + local), 0.0)
        acc = add_values(vown_ref, w, [jnp.zeros((rows_c, SB_HEAD_DIM), F32)] * SB_HEADS)
        for hh in range(SB_HEADS):
            acc_ref[hh] = acc[hh]
        run_ref[...] = total

    page_scores = [scores(kp[c], None) for c in range(pp)]
    run = run_ref[...]
    acc = [acc_ref[hh] for hh in range(SB_HEADS)]
    for c in range(pp):
        log_beta, local, total = page_scores[c]
        acc = add_values(vp[c], jnp.exp(log_beta + local + run), acc)
        run = run + total
    run_ref[...] = run
    for hh in range(SB_HEADS):
        acc_ref[hh] = acc[hh]

    @pl.when(s == pl.num_programs(1) - 1)
    def _():
        for hh in range(SB_HEADS):
            e = hh % 2
            o_ref[:, hh * SB_HEAD_DIM:(hh + 1) * SB_HEAD_DIM] = acc_ref[hh, e * nq:(e + 1) * nq, :]


def _sb_sample(q, k_own, v_own, bias, cache_k, cache_v, page_table, layer, pp=8):
    db, nq, _ = q.shape
    assert 2 * nq == 16 and SB_HEADS * nq <= LANES and cache_k.shape[2:] == (PAGE_SIZE, SB_HEADS, SB_HEAD_DIM)
    depth, n_pool = cache_k.shape[0], cache_k.shape[1]
    n_pages = page_table.shape[1]
    scale = SB_HEAD_DIM ** -0.5
    qh = (q * scale).reshape(db, nq, SB_HEADS, SB_HEAD_DIM)
    eye = jnp.eye(SB_HEADS, dtype=F32)
    qbd = jnp.einsum('bthd,hg->bhdgt', qh, eye).reshape(db, D_SB, SB_HEADS * nq)
    qbd = jnp.pad(qbd, ((0, 0), (0, 0), (0, LANES - SB_HEADS * nq))).astype(BF16)
    bias_l = jnp.pad(jnp.repeat(bias.astype(F32), nq), (0, LANES - SB_HEADS * nq)).reshape(1, LANES)
    pad = ((0, 0), (0, PAGE_SIZE - nq), (0, 0))
    page_rows = PAGE_SIZE * SB_HEADS
    kown = jnp.pad(k_own, pad).reshape(db * page_rows, SB_HEAD_DIM)
    vown = jnp.pad(v_own, pad).reshape(db * page_rows, SB_HEAD_DIM)
    ck = cache_k.reshape(depth * n_pool * page_rows, SB_HEAD_DIM)
    cv = cache_v.reshape(depth * n_pool * page_rows, SB_HEAD_DIM)
    base = layer * n_pool

    def page_map(c):
        return lambda b, s, pt: (base + pt[b, n_pages - 1 - (s * pp + c)], 0)

    page_spec = [pl.BlockSpec((page_rows, SB_HEAD_DIM), page_map(c)) for c in range(pp)]
    own_spec = pl.BlockSpec((page_rows, SB_HEAD_DIM), lambda b, s, pt: (b, 0))
    grid_spec = pltpu.PrefetchScalarGridSpec(
        num_scalar_prefetch=1,
        grid=(db, n_pages // pp),
        in_specs=[pl.BlockSpec((None, D_SB, LANES), lambda b, s, pt: (b, 0, 0)),
                  pl.BlockSpec((1, LANES), lambda b, s, pt: (0, 0)),
                  own_spec, own_spec] + page_spec + page_spec,
        out_specs=pl.BlockSpec((nq, D_SB), lambda b, s, pt: (b, 0)),
        scratch_shapes=[pltpu.VMEM((SB_HEADS, 2 * nq, SB_HEAD_DIM), F32),
                        pltpu.VMEM((1, LANES), F32)],
    )
    return pl.pallas_call(
        functools.partial(_sb_sample_kernel, pp=pp, nq=nq),
        grid_spec=grid_spec,
        out_shape=jax.ShapeDtypeStruct((db * nq, D_SB), F32),
        compiler_params=_cparams(("arbitrary", "arbitrary")),
    )(page_table, qbd, bias_l, kown, vown, *([ck] * pp), *([cv] * pp))


def _ssd_kernel(*refs, tb, has_init, has_buf):
    if has_buf:
        refs = refs[:-6] + refs[-5:]
    if has_init:
        (z0_ref, z1_ref, xa_ref, xb_ref, bc_ref, dt_ref, ctx_ref, h0_ref, cw_ref, cb_ref, dtb_ref, alog_ref,
         dsk_ref, ng_ref, spr_ref, sprt_ref, y_ref, hT_ref, xp_ref, hs_ref, yb_ref) = refs
    else:
        (z0_ref, z1_ref, xa_ref, xb_ref, bc_ref, dt_ref, ctx_ref, cw_ref, cb_ref, dtb_ref, alog_ref,
         dsk_ref, ng_ref, spr_ref, sprt_ref, y_ref, hT_ref, xp_ref, hs_ref, yb_ref) = refs
        h0_ref = None
    c = pl.program_id(1)
    L = TILE
    P2 = LANES
    n_pairs = D_SSM // P2

    @pl.when(c == 0)
    def _():
        xp_ref[0:8, :] = ctx_ref[...]
        if has_init:
            hs_ref[...] = h0_ref[...]
        else:
            hs_ref[...] = jnp.zeros_like(hs_ref)

    xp_ref[8:8 + tb, 0:1024] = xa_ref[...]
    xp_ref[8:8 + tb, 1024:2048] = xb_ref[...]
    xp_ref[8:8 + tb, 2048:3072] = bc_ref[...]
    if tb < L:
        xp_ref[8 + tb:8 + L, :] = jnp.zeros((L - tb, SSM_CONV_DIM), F32)

    conv = cb_ref[...] + cw_ref[0:1, :] * xp_ref[5:5 + L, :]
    for i in range(1, SSM_CONV):
        conv = conv + cw_ref[i:i + 1, :] * xp_ref[5 + i:5 + i + L, :]
    xp_ref[0:8, :] = xp_ref[tb:tb + 8, :]
    xc = _silu(conv)

    row = _iota((L, LANES), 0)
    lane = _iota((L, LANES), 1)
    dt_raw = dt_ref[...]
    if tb < L:
        dt_raw = jnp.concatenate([dt_raw, jnp.zeros((L - tb, LANES), F32)], axis=0)
    valid = (lane < SSM_HEADS) & (row < tb)
    dtv = jnp.where(valid, _softplus(dt_raw + dtb_ref[...]), 0.0)
    a = dtv * (-jnp.exp(alog_ref[...]))
    lower = (_iota((L, L), 0) >= _iota((L, L), 1))
    lower_b = lower.astype(BF16)
    upper_b = (_iota((L, L), 0) <= _iota((L, L), 1)).astype(BF16)
    cum = _sel_dot_l(lower_b, a)
    cum_t = _sel_dot_r(a.T, upper_b)
    last = cum[L - 1:L, :]
    spread = spr_ref[...]
    spread_t = sprt_ref[...]
    dt_x = _sel_dot_r(dtv, spread)
    ecum_x = _sel_dot_r(jnp.exp(cum), spread)
    wdec_x = _sel_dot_r(jnp.exp(last - cum), spread)
    sdec = _sel_dot_l(spread_t, jnp.broadcast_to(jnp.exp(cum_t[:, L - 1:L]), (LANES, LANES)))

    first_half = lane < SSM_HEAD_DIM
    pairs_per_group = n_pairs // SSM_GROUPS
    for g in range(SSM_GROUPS):
        bg = xc[:, D_SSM + g * SSM_STATE:D_SSM + (g + 1) * SSM_STATE]
        cg = xc[:, D_SSM + (SSM_GROUPS + g) * SSM_STATE:D_SSM + (SSM_GROUPS + g + 1) * SSM_STATE]
        bg_b = bg.astype(BF16)
        cg_b = cg.astype(BF16)
        cb = _dot_nt(cg_b, bg_b)
        for r in range(pairs_per_group):
            p = g * pairs_per_group + r
            sl = slice(p * P2, (p + 1) * P2)
            x_p = xc[:, sl]
            xs_p = x_p * dt_x[:, sl]
            xs_b = xs_p.astype(BF16)
            ys = []
            for e in range(2):
                hh = 2 * p + e
                seg = jnp.broadcast_to(cum[:, hh:hh + 1], (L, L)) - jnp.broadcast_to(cum_t[hh:hh + 1, :], (L, L))
                dec = jnp.where(lower, jnp.exp(jnp.minimum(seg, 0.0)), 0.0)
                ys.append(_dot((cb * dec).astype(BF16), xs_b))
            y_p = jnp.where(first_half, ys[0], ys[1])
            h_p = hs_ref[sl, :]
            y_p = y_p + _dot_nt(cg_b, h_p.astype(BF16)) * ecum_x[:, sl]
            xw_t = (xs_p * wdec_x[:, sl]).T.astype(BF16)
            hs_ref[sl, :] = sdec[sl, :] * h_p + _dot(xw_t, bg_b)
            yb_ref[:, sl] = y_p + dsk_ref[:, sl] * x_p

    zfull = jnp.concatenate([z0_ref[...], z1_ref[...]], axis=1)
    y = yb_ref[0:tb, :] * _silu(zfull)
    gw = D_SSM // SSM_GROUPS
    for g in range(SSM_GROUPS):
        yg = y[:, g * gw:(g + 1) * gw]
        yn = yg * lax.rsqrt(jnp.mean(yg * yg, axis=-1, keepdims=True) + EPS)
        y_ref[:, g * gw:(g + 1) * gw] = (yn * ng_ref[:, g * gw:(g + 1) * gw]).astype(y_ref.dtype)

    @pl.when(c == pl.num_programs(1) - 1)
    def _():
        hT_ref[...] = hs_ref[...]


def _ssd(proj, row0, nb, t, ctx8, h0, conv_w, conv_b, dt_bias, a_log, d_skip, norm_g, out_dtype, buf=None, col0=0):
    tb = min(t, TILE)
    nc = t // tb
    rb0 = row0 // tb
    has_init = h0 is not None

    def colblk(width, idx):
        return pl.BlockSpec((tb, width), lambda b, c: (rb0 + b * nc + c, idx))

    def full(shape):
        return pl.BlockSpec(shape, lambda b, c: (0,) * len(shape))

    in_specs = [colblk(1024, 3), colblk(1024, 4), colblk(1024, 5), colblk(1024, 6), colblk(1024, 7),
                colblk(LANES, 64),
                pl.BlockSpec((None, 8, SSM_CONV_DIM), lambda b, c: (b, 0, 0))]
    args = [proj] * 6 + [ctx8]
    if has_init:
        in_specs.append(pl.BlockSpec((None, D_SSM, SSM_STATE), lambda b, c: (b, 0, 0)))
        args.append(h0)
    padl = (0, LANES - SSM_HEADS)
    in_specs += [full((SSM_CONV, SSM_CONV_DIM)), full((1, SSM_CONV_DIM)), full((1, LANES)), full((1, LANES)),
                 full((1, D_SSM)), full((1, D_SSM)), full((LANES, D_SSM)), full((D_SSM, LANES))]
    spread = (jnp.arange(LANES)[:, None] == jnp.arange(D_SSM)[None, :] // SSM_HEAD_DIM).astype(BF16)
    args += [conv_w, conv_b.reshape(1, -1), jnp.pad(dt_bias, padl).reshape(1, LANES),
             jnp.pad(a_log, padl).reshape(1, LANES), jnp.repeat(d_skip, SSM_HEAD_DIM).reshape(1, D_SSM),
             norm_g.reshape(1, D_SSM), spread, spread.T]
    aliases = {}
    y_shape = jax.ShapeDtypeStruct((nb * t, D_SSM), out_dtype)
    if buf is not None:
        assert col0 % D_SSM == 0 and row0 == 0
        in_specs.append(pl.BlockSpec(memory_space=pl.ANY))
        args.append(buf)
        aliases = {len(args) - 1: 0}
        y_shape = jax.ShapeDtypeStruct(buf.shape, buf.dtype)
    cb0 = col0 // D_SSM
    return pl.pallas_call(
        functools.partial(_ssd_kernel, tb=tb, has_init=has_init, has_buf=buf is not None),
        grid=(nb, nc),
        in_specs=in_specs,
        out_specs=[pl.BlockSpec((tb, D_SSM), lambda b, c: (b * nc + c, cb0)),
                   pl.BlockSpec((None, D_SSM, SSM_STATE), lambda b, c: (b, 0, 0))],
        out_shape=[y_shape, jax.ShapeDtypeStruct((nb, D_SSM, SSM_STATE), F32)],
        input_output_aliases=aliases,
        scratch_shapes=[pltpu.VMEM((8 + TILE, SSM_CONV_DIM), F32),
                        pltpu.VMEM((D_SSM, SSM_STATE), F32),
                        pltpu.VMEM((TILE, D_SSM), F32)],
        compiler_params=_cparams(("arbitrary", "arbitrary")),
    )(*args)


def _gla_kernel(*refs, tb, has_init, has_buf):
    if has_buf:
        refs = refs[:-4] + refs[-3:]
    if has_init:
        (q_ref, k_ref, v_ref, r_ref, gk_ref, s0_ref, gw_ref, gb_ref, ng_ref, o_ref, sT_ref, st_ref) = refs
    else:
        (q_ref, k_ref, v_ref, r_ref, gk_ref, gw_ref, gb_ref, ng_ref, o_ref, sT_ref, st_ref) = refs
        s0_ref = None
    c = pl.program_id(1)
    L = TILE
    CH = min(GLA_CHUNK, tb)
    n_sub = max(tb // CH, 1)
    KP = GLA_HEADS * GLA_K_DIM

    @pl.when(c == 0)
    def _():
        if has_init:
            st_ref[...] = s0_ref[...]
        else:
            st_ref[...] = jnp.zeros_like(st_ref)

    def padrows(x):
        if tb < L:
            return jnp.concatenate([x, jnp.zeros((L - tb, x.shape[1]), F32)], axis=0)
        return x

    row = _iota((L, L), 0)
    col = _iota((L, L), 1)
    same_chunk = (row // CH) == (col // CH)
    lower = (row >= col) & same_chunk
    lower_b = lower.astype(BF16)
    rvalid = _iota((L, KP), 0) < tb
    glog = _dot(gk_ref[...].astype(BF16), gw_ref[...].astype(BF16)) + gb_ref[...]
    glog = padrows(-_softplus(-glog) * (1.0 / GLA_TAU))
    glog = jnp.where(rvalid, glog, 0.0)
    q_all = padrows(q_ref[...]) * (GLA_K_DIM ** -0.5)
    k_all = jnp.where(rvalid, padrows(k_ref[...]), 0.0)
    v_all = padrows(v_ref[...])
    lane = _iota((L, LANES), 1)
    rowl = _iota((L, LANES), 0)

    for p in range(KP // LANES):
        sl = slice(p * LANES, (p + 1) * LANES)
        g_p = glog[:, sl]
        b_p = _sel_dot_l(lower_b, g_p)
        qt = q_all[:, sl] * jnp.exp(b_p)
        kt = (k_all[:, sl] * jnp.exp(-b_p)).astype(BF16)
        lasts = [b_p[min((j + 1) * CH, L) - 1:min((j + 1) * CH, L), :] for j in range(n_sub)]
        s_cur = st_ref[sl, :]
        s_list = [s_cur]
        kw_t = []
        for j in range(n_sub):
            in_j = (rowl // CH) == j
            kw = jnp.where(in_j, k_all[:, sl] * jnp.exp(lasts[j] - b_p), 0.0)
            kw_t.append(kw.T.astype(BF16))
        heads_out = []
        new_states = [[None, None] for _ in range(n_sub)]
        for e in range(2):
            hh = 2 * p + e
            own = (lane // GLA_K_DIM) == e
            qm = jnp.where(own, qt, 0.0).astype(BF16)
            att = jnp.where(lower, _dot_nt(qm, kt), 0.0)
            v_h = v_all[:, hh * GLA_V_DIM:(hh + 1) * GLA_V_DIM].astype(BF16)
            heads_out.append((qm, _dot(att.astype(BF16), v_h), v_h))
        decs = []
        for j in range(n_sub):
            dcol = jnp.exp(lasts[j]).reshape(1, LANES)
            decs.append(jnp.broadcast_to(dcol, (LANES, LANES)).T)
        for j in range(n_sub):
            upd0 = _dot(kw_t[j], heads_out[0][2])
            upd1 = _dot(kw_t[j], heads_out[1][2])
            upd = jnp.where(_iota((LANES, GLA_V_DIM), 0) < GLA_K_DIM, upd0, upd1)
            s_list.append(decs[j] * s_list[j] + upd)
        st_ref[sl, :] = s_list[n_sub]
        for e in range(2):
            hh = 2 * p + e
            qm, o_h, _ = heads_out[e]
            inter = _dot(qm, s_list[0].astype(BF16))
            for j in range(1, n_sub):
                inter = jnp.where((rowl // CH) == j, _dot(qm, s_list[j].astype(BF16)), inter)
            o_h = o_h + inter
            o_h = o_h[0:tb, :]
            on = o_h * lax.rsqrt(jnp.mean(o_h * o_h, axis=-1, keepdims=True) + EPS) * ng_ref[...]
            vs = slice(hh * GLA_V_DIM, (hh + 1) * GLA_V_DIM)
            o_ref[:, vs] = (on * _silu(r_ref[:, vs])).astype(o_ref.dtype)

    @pl.when(c == pl.num_programs(1) - 1)
    def _():
        sT_ref[...] = st_ref[...]


def _gla(q, k, v, r, gk_low, nb, t, s0, gk_w, gk_b, norm_g, out_dtype, buf=None, col0=0):
    tb = min(t, TILE)
    nc = t // tb
    has_init = s0 is not None
    KP = GLA_HEADS * GLA_K_DIM

    def rows(width):
        return pl.BlockSpec((tb, width), lambda b, c: (b * nc + c, 0))

    def full(shape):
        return pl.BlockSpec(shape, lambda b, c: (0,) * len(shape))

    in_specs = [rows(KP), rows(KP), rows(D_GLA), rows(D_GLA), rows(GLA_RANK)]
    args = [q, k, v, r, gk_low]
    if has_init:
        in_specs.append(pl.BlockSpec((None, KP, GLA_V_DIM), lambda b, c: (b, 0, 0)))
        args.append(s0)
    in_specs += [full((GLA_RANK, KP)), full((1, KP)), full((1, GLA_V_DIM))]
    args += [gk_w, gk_b.reshape(1, KP), norm_g.reshape(1, GLA_V_DIM)]
    aliases = {}
    o_shape = jax.ShapeDtypeStruct((nb * t, D_GLA), out_dtype)
    if buf is not None:
        assert col0 % D_GLA == 0
        in_specs.append(pl.BlockSpec(memory_space=pl.ANY))
        args.append(buf)
        aliases = {len(args) - 1: 0}
        o_shape = jax.ShapeDtypeStruct(buf.shape, buf.dtype)
    cb0 = col0 // D_GLA
    return pl.pallas_call(
        functools.partial(_gla_kernel, tb=tb, has_init=has_init, has_buf=buf is not None),
        grid=(nb, nc),
        in_specs=in_specs,
        out_specs=[pl.BlockSpec((tb, D_GLA), lambda b, c: (b * nc + c, cb0)),
                   pl.BlockSpec((None, KP, GLA_V_DIM), lambda b, c: (b, 0, 0))],
        out_shape=[o_shape, jax.ShapeDtypeStruct((nb, KP, GLA_V_DIM), F32)],
        input_output_aliases=aliases,
        scratch_shapes=[pltpu.VMEM((KP, GLA_V_DIM), F32)],
        compiler_params=_cparams(("arbitrary", "arbitrary")),
    )(*args)


def _put_rows_kernel(*refs):
    parts, o_ref = refs[:-2], refs[-1]
    off = 0
    for p in parts:
        w = p.shape[1]
        o_ref[:, off:off + w] = p[...].astype(o_ref.dtype)
        off += w


def _put_rows(buf, parts, row0):
    rows = parts[0].shape[0]
    width = sum(p.shape[1] for p in parts)
    assert width == buf.shape[1] and row0 % rows == 0
    return pl.pallas_call(
        _put_rows_kernel,
        grid=(1,),
        in_specs=[pl.BlockSpec(p.shape, lambda i: (0, 0)) for p in parts] + [pl.BlockSpec(memory_space=pl.ANY)],
        out_specs=pl.BlockSpec((rows, width), lambda i: (row0 // rows, 0)),
        out_shape=jax.ShapeDtypeStruct(buf.shape, buf.dtype),
        input_output_aliases={len(parts): 0},
        compiler_params=_cparams(("arbitrary",)),
    )(*parts, buf)


def _head_rows_kernel(*refs, depth):
    ins, (ko_ref, vo_ref) = refs[:2 * depth], refs[2 * depth:]
    tr = ins[0].shape[0]
    for l in range(depth):
        @pl.when(pl.program_id(0) == l)
        def _(l=l):
            for src, dst in ((ins[2 * l], ko_ref), (ins[2 * l + 1], vo_ref)):
                for hh in range(SB_HEADS):
                    dst[pl.ds(hh, tr, stride=SB_HEADS), :] = src[:, hh * SB_HEAD_DIM:(hh + 1) * SB_HEAD_DIM]


def _head_rows(projs, rows, tr):
    depth = len(projs)
    nt = rows // tr
    in_specs, args = [], []
    for l, p in enumerate(projs):
        for cb in (1, 2):
            in_specs.append(pl.BlockSpec((tr, D_SB), lambda ll, i, l=l, cb=cb: (jnp.where(ll == l, i, 0), cb)))
            args.append(p)
    out_spec = pl.BlockSpec((tr * SB_HEADS, SB_HEAD_DIM), lambda ll, i: (ll * nt + i, 0))
    shape = jax.ShapeDtypeStruct((depth * rows * SB_HEADS, SB_HEAD_DIM), F32)
    return pl.pallas_call(
        functools.partial(_head_rows_kernel, depth=depth),
        grid=(depth, nt),
        in_specs=in_specs,
        out_specs=[out_spec, out_spec],
        out_shape=[shape, shape],
        compiler_params=_cparams(("arbitrary", "arbitrary")),
    )(*args)


def kernel(x_prompt, x_sample, cache_k, cache_v, page_table, state_conv, state_ssm, state_gla, norm1_g, w_in, sb_bias, conv_w, conv_b, dt_bias, a_log, d_skip, ssm_norm_g, gla_gk_w, gla_gk_b, gla_norm_g, w_out, norm2_g, w_gate, w_up, w_down, final_norm_g):
    bp, tp, d = x_prompt.shape
    bs, ts, _ = x_sample.shape
    depth = w_in.shape[0]
    mp = bp * tp
    ms = bs * ts
    m = mp + ms
    tm = _row_tile(m, 1376)
    tm_s = _row_tile(m, 688)
    d_ff = w_gate.shape[2]
    KP = GLA_HEADS * GLA_K_DIM
    off_xbc = 3 * D_SB + D_SSM
    off_gla = off_xbc + SSM_CONV_DIM + SSM_HEADS
    n_tail = SSM_CONV - 1

    x = jnp.concatenate([x_prompt.reshape(mp, d), x_sample.reshape(ms, d)], axis=0)
    w_down_bf = _cast_bf16(w_down, _row_tile(d_ff, 688))
    w_in_nk = jnp.swapaxes(w_in, 1, 2)
    outs = {k: [] for k in ("cp", "sp", "gp", "ks", "vs", "cs", "ss", "gs")}
    projs = []

    for l in range(depth):
        xn = _rmsnorm(x, norm1_g[l], BF16, tm_s)
        proj = _matmul(xn, w_in_nk, l, None, tm, 512, w_is_nk=True)

        k_rows = proj[:, D_SB:2 * D_SB]
        v_rows = proj[:, 2 * D_SB:3 * D_SB]
        gq = proj[:, off_gla:off_gla + KP]
        gk = proj[:, off_gla + KP:off_gla + 2 * KP]
        gv = proj[:, off_gla + 2 * KP:off_gla + 2 * KP + D_GLA]
        gr = proj[:, off_gla + 2 * KP + D_GLA:off_gla + 2 * KP + 2 * D_GLA]
        glow = proj[:, off_gla + 2 * KP + 2 * D_GLA:]

        mixed = jnp.zeros((m, d), BF16)
        mixed = _sb_prompt(proj, sb_bias[l], bp, tp, mixed, D_SSM)
        ctx_p = jnp.zeros((bp, 8, SSM_CONV_DIM), F32)
        mixed, ssm_p = _ssd(proj, 0, bp, tp, ctx_p, None, conv_w[l], conv_b[l], dt_bias[l], a_log[l],
                            d_skip[l], ssm_norm_g[l], BF16, buf=mixed, col0=0)
        mixed, gla_p = _gla(gq[:mp], gk[:mp], gv[:mp], gr[:mp], glow[:mp], bp, tp, None,
                            gla_gk_w[l], gla_gk_b[l], gla_norm_g[l], BF16, buf=mixed, col0=D_SSM + D_SB)

        q_s = proj[mp:, 0:D_SB].reshape(bs, ts, D_SB)
        k_s = k_rows[mp:].reshape(bs, ts, D_SB)
        v_s = v_rows[mp:].reshape(bs, ts, D_SB)
        o_sb_s = _sb_sample(q_s, k_s, v_s, sb_bias[l], cache_k, cache_v, page_table, l)
        ctx_s = jnp.pad(state_conv[l], ((0, 0), (8 - (SSM_CONV - 1), 0), (0, 0)))
        y_ssm_s, ssm_s = _ssd(proj, mp, bs, ts, ctx_s, state_ssm[l].reshape(bs, D_SSM, SSM_STATE),
                              conv_w[l], conv_b[l], dt_bias[l], a_log[l], d_skip[l], ssm_norm_g[l], F32)
        o_gl_s, gla_s = _gla(gq[mp:], gk[mp:], gv[mp:], gr[mp:], glow[mp:], bs, ts,
                             state_gla[l].reshape(bs, KP, GLA_V_DIM),
                             gla_gk_w[l], gla_gk_b[l], gla_norm_g[l], F32)

        mixed = _put_rows(mixed, [y_ssm_s, o_sb_s, o_gl_s], mp)
        x = _matmul(mixed, w_out, l, x, tm, 512,
                    k_order=((D_SB, D_SB + D_SSM), (0, D_SB), (D_SB + D_SSM, D_SB + D_SSM + D_GLA)))
        hn = _rmsnorm(x, norm2_g[l], BF16, tm_s)
        hmid = _gate_up(hn, w_gate, w_up, l, tm, 256)
        x = _matmul_bf16w(hmid, w_down_bf, l, x, tm_s, 512)

        xbc_p = jnp.stack([proj[(b + 1) * tp - n_tail:(b + 1) * tp, off_xbc:off_xbc + SSM_CONV_DIM]
                           for b in range(bp)])
        xbc_s = jnp.concatenate([state_conv[l], proj[mp:, off_xbc:off_xbc + SSM_CONV_DIM].reshape(bs, ts, -1)], axis=1)
        projs.append(proj)
        outs["cp"].append(xbc_p)
        outs["sp"].append(ssm_p.reshape(bp, SSM_HEADS, SSM_HEAD_DIM, SSM_STATE))
        outs["gp"].append(gla_p.reshape(bp, GLA_HEADS, GLA_K_DIM, GLA_V_DIM))
        outs["ks"].append(k_s.reshape(bs, ts, SB_HEADS, SB_HEAD_DIM))
        outs["vs"].append(v_s.reshape(bs, ts, SB_HEADS, SB_HEAD_DIM))
        outs["cs"].append(xbc_s[:, xbc_s.shape[1] - n_tail:])
        outs["ss"].append(ssm_s.reshape(bs, SSM_HEADS, SSM_HEAD_DIM, SSM_STATE))
        outs["gs"].append(gla_s.reshape(bs, GLA_HEADS, GLA_K_DIM, GLA_V_DIM))

    y_p = _rmsnorm(x, final_norm_g, F32, _row_tile(mp, 512), rows=mp)
    y_s = _rmsnorm(x[mp:], final_norm_g, F32, ms)
    st = {k: jnp.stack(v) for k, v in outs.items()}
    kp, vp = _head_rows(projs, mp, _row_tile(mp, 512))
    return (y_p.reshape(bp, tp, d), y_s.reshape(bs, ts, d),
            kp.reshape(depth, bp, tp, SB_HEADS, SB_HEAD_DIM), vp.reshape(depth, bp, tp, SB_HEADS, SB_HEAD_DIM),
            st["cp"], st["sp"], st["gp"],
            st["ks"], st["vs"], st["cs"], st["ss"], st["gs"])
```

```python
import functools

import jax
import jax.numpy as jnp
from jax import lax
from jax.experimental import pallas as pl
from jax.experimental.pallas import tpu as pltpu

F32 = jnp.float32
BF16 = jnp.bfloat16

EPS = 1e-6
SB_HEADS = 8
SB_HEAD_DIM = 128
D_SB = SB_HEADS * SB_HEAD_DIM
SSM_HEADS = 32
SSM_HEAD_DIM = 64
D_SSM = SSM_HEADS * SSM_HEAD_DIM
SSM_STATE = 128
SSM_GROUPS = 4
SSM_CONV = 4
SSM_CONV_DIM = D_SSM + 2 * SSM_GROUPS * SSM_STATE
GLA_HEADS = 8
GLA_K_DIM = 64
GLA_V_DIM = 128
D_GLA = GLA_HEADS * GLA_V_DIM
GLA_RANK = 16
GLA_TAU = 16.0
GLA_CHUNK = 64
PAGE_SIZE = 128

LANES = 128
TILE = 128
VMEM_LIMIT = 60 * 1024 * 1024


def _cparams(sem):
    return pltpu.CompilerParams(dimension_semantics=sem, vmem_limit_bytes=VMEM_LIMIT)


def _softplus(z):
    return jnp.maximum(z, 0.0) + jnp.log(1.0 + jnp.exp(-jnp.abs(z)))


def _silu(x):
    return x * (0.5 + 0.5 * jnp.tanh(0.5 * x))


def _split2(d):
    d0 = d.astype(BF16)
    d1 = (d - d0.astype(F32)).astype(BF16)
    return d0, d1


def _dot(a, b):
    return jnp.dot(a, b, preferred_element_type=F32)


def _dot_nt(a, b):
    return lax.dot_general(a, b, (((1,), (1,)), ((), ())), preferred_element_type=F32)


def _sel_dot_r(data, sel):
    d0, d1 = _split2(data)
    return _dot(d0, sel) + _dot(d1, sel)


def _sel_dot_l(sel, data):
    d0, d1 = _split2(data)
    return _dot(sel, d0) + _dot(sel, d1)


def _iota(shape, dim):
    return lax.broadcasted_iota(jnp.int32, shape, dim)


def _rmsnorm_kernel(x_ref, g_ref, o_ref):
    x = x_ref[...]
    y = x * lax.rsqrt(jnp.mean(x * x, axis=-1, keepdims=True) + EPS)
    o_ref[...] = (y * g_ref[...]).astype(o_ref.dtype)


def _row_tile(m, target):
    best = None
    for t in range(16, target + 1, 16):
        if m % t == 0:
            best = t
    assert best is not None, (m, target)
    return best


def _rmsnorm(x, g, out_dtype, tm, rows=None):
    m, d = x.shape
    m = m if rows is None else rows
    return pl.pallas_call(
        _rmsnorm_kernel,
        grid=(m // tm,),
        in_specs=[pl.BlockSpec((tm, d), lambda i: (i, 0)), pl.BlockSpec((1, d), lambda i: (0, 0))],
        out_specs=pl.BlockSpec((tm, d), lambda i: (i, 0)),
        out_shape=jax.ShapeDtypeStruct((m, d), out_dtype),
        compiler_params=_cparams(("arbitrary",)),
    )(x, g.reshape(1, d))


def _mm_kernel(x_ref, w_ref, *rest, has_res, w_is_nk, k_order):
    if has_res:
        res_ref, o_ref, wbf_ref = rest
    else:
        o_ref, wbf_ref = rest

    @pl.when(pl.program_id(1) == 0)
    def _():
        if k_order is None:
            wbf_ref[...] = w_ref[...].astype(BF16)
        else:
            off = 0
            for a, b in k_order:
                wbf_ref[off:off + b - a, :] = w_ref[a:b, :].astype(BF16)
                off += b - a

    acc = _dot_nt(x_ref[...], wbf_ref[...]) if w_is_nk else _dot(x_ref[...], wbf_ref[...])
    if has_res:
        acc = acc + res_ref[...]
    o_ref[...] = acc.astype(o_ref.dtype)


def _matmul(x, w, layer, res, tm, tn, w_is_nk=False, k_order=None):
    m, k = x.shape
    assert k_order is None or (not w_is_nk and sum(b - a for a, b in k_order) == k)
    if w_is_nk:
        n = w.shape[1]
        w_spec = pl.BlockSpec((None, tn, k), lambda j, i: (layer, j, 0))
        w_scratch = pltpu.VMEM((tn, k), BF16)
    else:
        n = w.shape[2]
        w_spec = pl.BlockSpec((None, k, tn), lambda j, i: (layer, 0, j))
        w_scratch = pltpu.VMEM((k, tn), BF16)
    in_specs = [pl.BlockSpec((tm, k), lambda j, i: (i, 0)), w_spec]
    args = [x, w]
    if res is not None:
        in_specs.append(pl.BlockSpec((tm, tn), lambda j, i: (i, j)))
        args.append(res)
    return pl.pallas_call(
        functools.partial(_mm_kernel, has_res=res is not None, w_is_nk=w_is_nk, k_order=k_order),
        grid=(pl.cdiv(n, tn), m // tm),
        in_specs=in_specs,
        out_specs=pl.BlockSpec((tm, tn), lambda j, i: (i, j)),
        out_shape=jax.ShapeDtypeStruct((m, n), F32),
        scratch_shapes=[w_scratch],
        compiler_params=_cparams(("arbitrary", "arbitrary")),
    )(*args)


MM_SLABS = 4


def _slabs(rows, n):
    step = -(-rows // (16 * n)) * 16
    bounds = [min(i * step, rows) for i in range(n + 1)]
    return [(a, b) for a, b in zip(bounds[:-1], bounds[1:]) if b > a]


def _gate_up_kernel(x_ref, wg_ref, wu_ref, o_ref, wbf_ref, *, tn):
    @pl.when(pl.program_id(1) == 0)
    def _():
        wbf_ref[:, :tn] = wg_ref[...].astype(BF16)
        wbf_ref[:, tn:] = wu_ref[...].astype(BF16)

    for r0, r1 in _slabs(o_ref.shape[0], MM_SLABS):
        gu = _dot(x_ref[r0:r1, :], wbf_ref[...])
        o_ref[r0:r1, :] = (_silu(gu[:, :tn]) * gu[:, tn:]).astype(o_ref.dtype)


def _gate_up(x, wg, wu, layer, tm, tn):
    m, k = x.shape
    f = wg.shape[2]
    wspec = pl.BlockSpec((None, k, tn), lambda j, i: (layer, 0, j))
    return pl.pallas_call(
        functools.partial(_gate_up_kernel, tn=tn),
        grid=(f // tn, m // tm),
        in_specs=[pl.BlockSpec((tm, k), lambda j, i: (i, 0)), wspec, wspec],
        out_specs=pl.BlockSpec((tm, tn), lambda j, i: (i, j)),
        out_shape=jax.ShapeDtypeStruct((m, f), BF16),
        scratch_shapes=[pltpu.VMEM((k, 2 * tn), BF16)],
        compiler_params=_cparams(("arbitrary", "arbitrary")),
    )(x, wg, wu)


def _cast_kernel(w_ref, o_ref):
    o_ref[...] = w_ref[...].astype(o_ref.dtype)


def _cast_bf16(w, tr):
    nl, r, c = w.shape
    spec = pl.BlockSpec((None, tr, c), lambda l, i: (l, i, 0))
    return pl.pallas_call(
        _cast_kernel,
        grid=(nl, r // tr),
        in_specs=[spec],
        out_specs=spec,
        out_shape=jax.ShapeDtypeStruct(w.shape, BF16),
        compiler_params=_cparams(("arbitrary", "arbitrary")),
    )(w)


def _mm_res_kernel(x_ref, w_ref, res_ref, o_ref):
    o_ref[...] = _dot(x_ref[...], w_ref[...]) + res_ref[...]


def _matmul_bf16w(x, w, layer, res, tm, tn):
    m, k = x.shape
    n = w.shape[2]
    return pl.pallas_call(
        _mm_res_kernel,
        grid=(m // tm, n // tn),
        in_specs=[pl.BlockSpec((tm, k), lambda i, j: (i, 0)),
                  pl.BlockSpec((None, k, tn), lambda i, j: (layer, 0, j)),
                  pl.BlockSpec((tm, tn), lambda i, j: (i, j))],
        out_specs=pl.BlockSpec((tm, tn), lambda i, j: (i, j)),
        out_shape=jax.ShapeDtypeStruct((m, n), F32),
        compiler_params=_cparams(("arbitrary", "arbitrary")),
    )(x, w, res)


SB_KEYS = 256
SB_UNROLL = 2


def _sb_prompt_kernel(bias_ref, q_ref, k_ref, v_ref, buf_ref, o_ref, acc_ref, run_ref, *, tq, scale):
    del buf_ref
    h = pl.program_id(1)
    i = pl.program_id(2)
    nk = SB_KEYS
    bias = bias_ref[h]
    q = (q_ref[...] * scale).astype(BF16)
    later = _iota((nk, nk), 0) > _iota((nk, nk), 1)
    suffix_total = jnp.concatenate([later.astype(BF16), jnp.ones((nk, LANES), BF16)], axis=1)

    def block(qv, start, run, mask):
        kb = k_ref[pl.ds(start, nk), :].astype(BF16)
        vb = v_ref[pl.ds(start, nk), :].astype(BF16)
        z = _dot_nt(qv, kb) + bias
        sp = _softplus(z)
        lg = -sp if mask is None else jnp.where(mask, -sp, 0.0)
        st = _dot(lg.astype(BF16), suffix_total)
        e = jnp.exp(z - sp + st[:, :nk] + jnp.concatenate([run] * (nk // LANES), axis=1))
        w = e if mask is None else jnp.where(mask, e, 0.0)
        return _dot(w.astype(BF16), vb), run + st[:, nk:]

    acc_ref[...] = jnp.zeros_like(acc_ref)
    run_ref[...] = jnp.zeros_like(run_ref)
    q0 = pl.multiple_of(i * tq, tq)
    for jd in reversed(range(tq // nk)):
        r0 = jd * nk
        rows = tq - r0
        mask = _iota((rows, nk), 1) < _iota((rows, nk), 0)
        pv, run = block(q[r0:], q0 + r0, run_ref[r0:, :], mask)
        acc_ref[r0:, :] += pv
        run_ref[r0:, :] = run

    def body(g, carry):
        run = run_ref[...]
        total = None
        for u in range(SB_UNROLL):
            start = pl.multiple_of(q0 - (g * SB_UNROLL + u + 1) * nk, nk)
            pv, run = block(q, start, run, None)
            total = pv if total is None else total + pv
        acc_ref[...] += total
        run_ref[...] = run
        return carry

    lax.fori_loop(0, (i * tq) // (SB_UNROLL * nk), body, 0)
    o_ref[...] = acc_ref[...].astype(o_ref.dtype)


def _sb_prompt(proj, bias, nb, t, buf, col0, tq=512):
    assert t % tq == 0 and tq % (SB_UNROLL * SB_KEYS) == 0 and col0 % SB_HEAD_DIM == 0
    nq = t // tq
    hb = SB_HEADS
    cb0 = col0 // SB_HEAD_DIM
    return pl.pallas_call(
        functools.partial(_sb_prompt_kernel, tq=tq, scale=SB_HEAD_DIM ** -0.5),
        grid=(nb, hb, nq),
        in_specs=[pl.BlockSpec(memory_space=pltpu.SMEM),
                  pl.BlockSpec((tq, SB_HEAD_DIM), lambda b, h, i: (b * nq + i, h)),
                  pl.BlockSpec((t, SB_HEAD_DIM), lambda b, h, i: (b, hb + h)),
                  pl.BlockSpec((t, SB_HEAD_DIM), lambda b, h, i: (b, 2 * hb + h)),
                  pl.BlockSpec(memory_space=pl.ANY)],
        out_specs=pl.BlockSpec((tq, SB_HEAD_DIM), lambda b, h, i: (b * nq + i, cb0 + h)),
        out_shape=jax.ShapeDtypeStruct(buf.shape, buf.dtype),
        input_output_aliases={4: 0},
        scratch_shapes=[pltpu.VMEM((tq, SB_HEAD_DIM), F32), pltpu.VMEM((tq, LANES), F32)],
        compiler_params=_cparams(("arbitrary", "arbitrary", "arbitrary")),
    )(bias, proj, proj, proj, buf)


def _sb_sample_kernel(pt_ref, qbd_ref, bias_ref, kown_ref, vown_ref, *refs, pp, nq):
    del pt_ref
    kp = refs[:pp]
    vp = refs[pp:2 * pp]
    o_ref = refs[2 * pp]
    acc_ref, run_ref = refs[2 * pp + 1:]
    s = pl.program_id(1)
    n = PAGE_SIZE
    rows_c = 2 * nq
    earlier = (_iota((n, n), 1) > _iota((n, n), 0)).astype(BF16)
    qbd = qbd_ref[...]
    bias = bias_ref[...]

    def head_rows(ref, hh):
        return ref[pl.ds(hh, n, stride=SB_HEADS), :].astype(BF16)

    def scores(k_ref, mask):
        kcat = jnp.concatenate([head_rows(k_ref, hh) for hh in range(SB_HEADS)], axis=1)
        z = _dot(kcat, qbd) + bias
        sp = _softplus(z)
        lg = -sp if mask is None else jnp.where(mask, -sp, 0.0)
        return z - sp, _dot(earlier, lg.astype(BF16)), jnp.sum(lg, axis=0, keepdims=True)

    def add_values(v_ref, w, acc):
        wt = w.T.astype(BF16)
        return [acc[hh] + _dot(wt[(hh // 2) * rows_c:(hh // 2 + 1) * rows_c, :], head_rows(v_ref, hh))
                for hh in range(SB_HEADS)]

    @pl.when(s == 0)
    def _():
        mask = _iota((n, LANES), 0) < (_iota((n, LANES), 1) % nq)
        log_beta, local, total = scores(kown_ref, mask)
        w = jnp.where(mask, jnp.exp(log_beta + local), 0.0)
        acc = add_values(vown_ref, w, [jnp.zeros((rows_c, SB_HEAD_DIM), F32)] * SB_HEADS)
        for hh in range(SB_HEADS):
            acc_ref[hh] = acc[hh]
        run_ref[...] = total

    page_scores = [scores(kp[c], None) for c in range(pp)]
    run = run_ref[...]
    acc = [acc_ref[hh] for hh in range(SB_HEADS)]
    for c in range(pp):
        log_beta, local, total = page_scores[c]
        acc = add_values(vp[c], jnp.exp(log_beta + local + run), acc)
        run = run + total
    run_ref[...] = run
    for hh in range(SB_HEADS):
        acc_ref[hh] = acc[hh]

    @pl.when(s == pl.num_programs(1) - 1)
    def _():
        for hh in range(SB_HEADS):
            e = hh % 2
            o_ref[:, hh * SB_HEAD_DIM:(hh + 1) * SB_HEAD_DIM] = acc_ref[hh, e * nq:(e + 1) * nq, :]


def _sb_sample(q, k_own, v_own, bias, cache_k, cache_v, page_table, layer, pp=8):
    db, nq, _ = q.shape
    assert 2 * nq == 16 and SB_HEADS * nq <= LANES and cache_k.shape[2:] == (PAGE_SIZE, SB_HEADS, SB_HEAD_DIM)
    depth, n_pool = cache_k.shape[0], cache_k.shape[1]
    n_pages = page_table.shape[1]
    scale = SB_HEAD_DIM ** -0.5
    qh = (q * scale).reshape(db, nq, SB_HEADS, SB_HEAD_DIM)
    eye = jnp.eye(SB_HEADS, dtype=F32)
    qbd = jnp.einsum('bthd,hg->bhdgt', qh, eye).reshape(db, D_SB, SB_HEADS * nq)
    qbd = jnp.pad(qbd, ((0, 0), (0, 0), (0, LANES - SB_HEADS * nq))).astype(BF16)
    bias_l = jnp.pad(jnp.repeat(bias.astype(F32), nq), (0, LANES - SB_HEADS * nq)).reshape(1, LANES)
    pad = ((0, 0), (0, PAGE_SIZE - nq), (0, 0))
    page_rows = PAGE_SIZE * SB_HEADS
    kown = jnp.pad(k_own, pad).reshape(db * page_rows, SB_HEAD_DIM)
    vown = jnp.pad(v_own, pad).reshape(db * page_rows, SB_HEAD_DIM)
    ck = cache_k.reshape(depth * n_pool * page_rows, SB_HEAD_DIM)
    cv = cache_v.reshape(depth * n_pool * page_rows, SB_HEAD_DIM)
    base = layer * n_pool

    def page_map(c):
        return lambda b, s, pt: (base + pt[b, n_pages - 1 - (s * pp + c)], 0)

    page_spec = [pl.BlockSpec((page_rows, SB_HEAD_DIM), page_map(c)) for c in range(pp)]
    own_spec = pl.BlockSpec((page_rows, SB_HEAD_DIM), lambda b, s, pt: (b, 0))
    grid_spec = pltpu.PrefetchScalarGridSpec(
        num_scalar_prefetch=1,
        grid=(db, n_pages // pp),
        in_specs=[pl.BlockSpec((None, D_SB, LANES), lambda b, s, pt: (b, 0, 0)),
                  pl.BlockSpec((1, LANES), lambda b, s, pt: (0, 0)),
                  own_spec, own_spec] + page_spec + page_spec,
        out_specs=pl.BlockSpec((nq, D_SB), lambda b, s, pt: (b, 0)),
        scratch_shapes=[pltpu.VMEM((SB_HEADS, 2 * nq, SB_HEAD_DIM), F32),
                        pltpu.VMEM((1, LANES), F32)],
    )
    return pl.pallas_call(
        functools.partial(_sb_sample_kernel, pp=pp, nq=nq),
        grid_spec=grid_spec,
        out_shape=jax.ShapeDtypeStruct((db * nq, D_SB), F32),
        compiler_params=_cparams(("arbitrary", "arbitrary")),
    )(page_table, qbd, bias_l, kown, vown, *([ck] * pp), *([cv] * pp))


def _ssd_kernel(*refs, tb, has_init, has_buf):
    if has_buf:
        refs = refs[:-6] + refs[-5:]
    if has_init:
        (z0_ref, z1_ref, xa_ref, xb_ref, bc_ref, dt_ref, ctx_ref, h0_ref, cw_ref, cb_ref, dtb_ref, alog_ref,
         dsk_ref, ng_ref, spr_ref, sprt_ref, y_ref, hT_ref, xp_ref, hs_ref, yb_ref) = refs
    else:
        (z0_ref, z1_ref, xa_ref, xb_ref, bc_ref, dt_ref, ctx_ref, cw_ref, cb_ref, dtb_ref, alog_ref,
         dsk_ref, ng_ref, spr_ref, sprt_ref, y_ref, hT_ref, xp_ref, hs_ref, yb_ref) = refs
        h0_ref = None
    c = pl.program_id(1)
    L = TILE
    P2 = LANES
    n_pairs = D_SSM // P2

    @pl.when(c == 0)
    def _():
        xp_ref[0:8, :] = ctx_ref[...]
        if has_init:
            hs_ref[...] = h0_ref[...]
        else:
            hs_ref[...] = jnp.zeros_like(hs_ref)

    xp_ref[8:8 + tb, 0:1024] = xa_ref[...]
    xp_ref[8:8 + tb, 1024:2048] = xb_ref[...]
    xp_ref[8:8 + tb, 2048:3072] = bc_ref[...]
    if tb < L:
        xp_ref[8 + tb:8 + L, :] = jnp.zeros((L - tb, SSM_CONV_DIM), F32)

    conv = cb_ref[...] + cw_ref[0:1, :] * xp_ref[5:5 + L, :]
    for i in range(1, SSM_CONV):
        conv = conv + cw_ref[i:i + 1, :] * xp_ref[5 + i:5 + i + L, :]
    xp_ref[0:8, :] = xp_ref[tb:tb + 8, :]
    xc = _silu(conv)

    row = _iota((L, LANES), 0)
    lane = _iota((L, LANES), 1)
    dt_raw = dt_ref[...]
    if tb < L:
        dt_raw = jnp.concatenate([dt_raw, jnp.zeros((L - tb, LANES), F32)], axis=0)
    valid = (lane < SSM_HEADS) & (row < tb)
    dtv = jnp.where(valid, _softplus(dt_raw + dtb_ref[...]), 0.0)
    a = dtv * (-jnp.exp(alog_ref[...]))
    lower = (_iota((L, L), 0) >= _iota((L, L), 1))
    lower_b = lower.astype(BF16)
    upper_b = (_iota((L, L), 0) <= _iota((L, L), 1)).astype(BF16)
    cum = _sel_dot_l(lower_b, a)
    cum_t = _sel_dot_r(a.T, upper_b)
    last = cum[L - 1:L, :]
    spread = spr_ref[...]
    spread_t = sprt_ref[...]
    dt_x = _sel_dot_r(dtv, spread)
    ecum_x = _sel_dot_r(jnp.exp(cum), spread)
    dtw_x = _sel_dot_r(dtv * jnp.exp(last - cum), spread)
    sdec = _sel_dot_l(spread_t, jnp.broadcast_to(jnp.exp(cum_t[:, L - 1:L]), (LANES, LANES)))

    first_half = lane < SSM_HEAD_DIM
    pairs_per_group = n_pairs // SSM_GROUPS
    for g in range(SSM_GROUPS):
        bg = xc[:, D_SSM + g * SSM_STATE:D_SSM + (g + 1) * SSM_STATE]
        cg = xc[:, D_SSM + (SSM_GROUPS + g) * SSM_STATE:D_SSM + (SSM_GROUPS + g + 1) * SSM_STATE]
        bg_b = bg.astype(BF16)
        cg_b = cg.astype(BF16)
        cb = _dot_nt(cg_b, bg_b)
        for r in range(pairs_per_group):
            p = g * pairs_per_group + r
            sl = slice(p * P2, (p + 1) * P2)
            x_p = xc[:, sl]
            xs_p = x_p * dt_x[:, sl]
            xs_b = xs_p.astype(BF16)
            ys = []
            for e in range(2):
                hh = 2 * p + e
                seg = jnp.broadcast_to(cum[:, hh:hh + 1], (L, L)) - jnp.broadcast_to(cum_t[hh:hh + 1, :], (L, L))
                dec = jnp.where(lower, jnp.exp(jnp.minimum(seg, 0.0)), 0.0)
                ys.append(_dot((cb * dec).astype(BF16), xs_b))
            y_p = jnp.where(first_half, ys[0], ys[1])
            h_p = hs_ref[sl, :]
            y_p = y_p + _dot_nt(cg_b, h_p.astype(BF16)) * ecum_x[:, sl]
            xw_t = (x_p * dtw_x[:, sl]).T.astype(BF16)
            hs_ref[sl, :] = sdec[sl, :] * h_p + _dot(xw_t, bg_b)
            yb_ref[:, sl] = y_p + dsk_ref[:, sl] * x_p

    zfull = jnp.concatenate([z0_ref[...], z1_ref[...]], axis=1)
    y = yb_ref[0:tb, :] * _silu(zfull)
    gw = D_SSM // SSM_GROUPS
    for g in range(SSM_GROUPS):
        yg = y[:, g * gw:(g + 1) * gw]
        yn = yg * lax.rsqrt(jnp.mean(yg * yg, axis=-1, keepdims=True) + EPS)
        y_ref[:, g * gw:(g + 1) * gw] = (yn * ng_ref[:, g * gw:(g + 1) * gw]).astype(y_ref.dtype)

    @pl.when(c == pl.num_programs(1) - 1)
    def _():
        hT_ref[...] = hs_ref[...]


def _ssd(proj, row0, nb, t, ctx8, h0, conv_w, conv_b, dt_bias, a_log, d_skip, norm_g, out_dtype, buf=None, col0=0):
    tb = min(t, TILE)
    nc = t // tb
    rb0 = row0 // tb
    has_init = h0 is not None

    def colblk(width, idx):
        return pl.BlockSpec((tb, width), lambda b, c: (rb0 + b * nc + c, idx))

    def full(shape):
        return pl.BlockSpec(shape, lambda b, c: (0,) * len(shape))

    in_specs = [colblk(1024, 3), colblk(1024, 4), colblk(1024, 5), colblk(1024, 6), colblk(1024, 7),
                colblk(LANES, 64),
                pl.BlockSpec((None, 8, SSM_CONV_DIM), lambda b, c: (b, 0, 0))]
    args = [proj] * 6 + [ctx8]
    if has_init:
        in_specs.append(pl.BlockSpec((None, D_SSM, SSM_STATE), lambda b, c: (b, 0, 0)))
        args.append(h0)
    padl = (0, LANES - SSM_HEADS)
    in_specs += [full((SSM_CONV, SSM_CONV_DIM)), full((1, SSM_CONV_DIM)), full((1, LANES)), full((1, LANES)),
                 full((1, D_SSM)), full((1, D_SSM)), full((LANES, D_SSM)), full((D_SSM, LANES))]
    spread = (jnp.arange(LANES)[:, None] == jnp.arange(D_SSM)[None, :] // SSM_HEAD_DIM).astype(BF16)
    args += [conv_w, conv_b.reshape(1, -1), jnp.pad(dt_bias, padl).reshape(1, LANES),
             jnp.pad(a_log, padl).reshape(1, LANES), jnp.repeat(d_skip, SSM_HEAD_DIM).reshape(1, D_SSM),
             norm_g.reshape(1, D_SSM), spread, spread.T]
    aliases = {}
    y_shape = jax.ShapeDtypeStruct((nb * t, D_SSM), out_dtype)
    if buf is not None:
        assert col0 % D_SSM == 0 and row0 == 0
        in_specs.append(pl.BlockSpec(memory_space=pl.ANY))
        args.append(buf)
        aliases = {len(args) - 1: 0}
        y_shape = jax.ShapeDtypeStruct(buf.shape, buf.dtype)
    cb0 = col0 // D_SSM
    return pl.pallas_call(
        functools.partial(_ssd_kernel, tb=tb, has_init=has_init, has_buf=buf is not None),
        grid=(nb, nc),
        in_specs=in_specs,
        out_specs=[pl.BlockSpec((tb, D_SSM), lambda b, c: (b * nc + c, cb0)),
                   pl.BlockSpec((None, D_SSM, SSM_STATE), lambda b, c: (b, 0, 0))],
        out_shape=[y_shape, jax.ShapeDtypeStruct((nb, D_SSM, SSM_STATE), F32)],
        input_output_aliases=aliases,
        scratch_shapes=[pltpu.VMEM((8 + TILE, SSM_CONV_DIM), F32),
                        pltpu.VMEM((D_SSM, SSM_STATE), F32),
                        pltpu.VMEM((TILE, D_SSM), F32)],
        compiler_params=_cparams(("arbitrary", "arbitrary")),
    )(*args)


def _gla_kernel(*refs, tb, has_init, has_buf):
    if has_buf:
        refs = refs[:-4] + refs[-3:]
    wide, tail_ref, refs = refs[:GLA_N_WIDE], refs[GLA_N_WIDE], refs[GLA_N_WIDE + 1:]
    if has_init:
        (s0_ref, gw_ref, gb_ref, ng_ref, o_ref, sT_ref, st_ref) = refs
    else:
        (gw_ref, gb_ref, ng_ref, o_ref, sT_ref, st_ref) = refs
        s0_ref = None
    c = pl.program_id(1)
    L = TILE
    tail = tail_ref[...]
    region = jnp.concatenate([w[...] for w in wide] + [tail], axis=1)
    region = pltpu.roll(region, region.shape[1] - GLA_SHIFT, 1)
    kp_, o0 = GLA_HEADS * GLA_K_DIM, 0
    q_in = region[:, o0:o0 + kp_]
    k_in = region[:, o0 + kp_:o0 + 2 * kp_]
    v_in = region[:, o0 + 2 * kp_:o0 + 2 * kp_ + D_GLA]
    r_in = region[:, o0 + 2 * kp_ + D_GLA:o0 + 2 * kp_ + 2 * D_GLA]
    tail_lane = _iota(tail.shape, 1)
    gate_in = jnp.where((tail_lane >= GLA_SHIFT) & (tail_lane < GLA_SHIFT + GLA_RANK), tail, 0.0)
    CH = min(GLA_CHUNK, tb)
    n_sub = max(tb // CH, 1)
    KP = GLA_HEADS * GLA_K_DIM

    @pl.when(c == 0)
    def _():
        if has_init:
            st_ref[...] = s0_ref[...]
        else:
            st_ref[...] = jnp.zeros_like(st_ref)

    def padrows(x):
        if tb < L:
            return jnp.concatenate([x, jnp.zeros((L - tb, x.shape[1]), F32)], axis=0)
        return x

    row = _iota((L, L), 0)
    col = _iota((L, L), 1)
    same_chunk = (row // CH) == (col // CH)
    lower = (row >= col) & same_chunk
    lower_b = lower.astype(BF16)
    rvalid = _iota((L, KP), 0) < tb
    glog = _dot(gate_in.astype(BF16), gw_ref[...].astype(BF16)) + gb_ref[...]
    glog = padrows(-_softplus(-glog) * (1.0 / GLA_TAU))
    glog = jnp.where(rvalid, glog, 0.0)
    q_all = padrows(q_in) * (GLA_K_DIM ** -0.5)
    k_all = jnp.where(rvalid, padrows(k_in), 0.0)
    v_all = padrows(v_in)
    lane = _iota((L, LANES), 1)
    rowl = _iota((L, LANES), 0)

    for p in range(KP // LANES):
        sl = slice(p * LANES, (p + 1) * LANES)
        g_p = glog[:, sl]
        b_p = _sel_dot_l(lower_b, g_p)
        qt = q_all[:, sl] * jnp.exp(b_p)
        kt = (k_all[:, sl] * jnp.exp(-b_p)).astype(BF16)
        lasts = [b_p[min((j + 1) * CH, L) - 1:min((j + 1) * CH, L), :] for j in range(n_sub)]
        s_cur = st_ref[sl, :]
        s_list = [s_cur]
        kw_t = []
        for j in range(n_sub):
            in_j = (rowl // CH) == j
            kw = jnp.where(in_j, k_all[:, sl] * jnp.exp(lasts[j] - b_p), 0.0)
            kw_t.append(kw.T.astype(BF16))
        heads_out = []
        new_states = [[None, None] for _ in range(n_sub)]
        for e in range(2):
            hh = 2 * p + e
            own = (lane // GLA_K_DIM) == e
            qm = jnp.where(own, qt, 0.0).astype(BF16)
            att = jnp.where(lower, _dot_nt(qm, kt), 0.0)
            v_h = v_all[:, hh * GLA_V_DIM:(hh + 1) * GLA_V_DIM].astype(BF16)
            heads_out.append((qm, _dot(att.astype(BF16), v_h), v_h))
        decs = []
        for j in range(n_sub):
            dcol = jnp.exp(lasts[j]).reshape(1, LANES)
            decs.append(jnp.broadcast_to(dcol, (LANES, LANES)).T)
        for j in range(n_sub):
            upd0 = _dot(kw_t[j], heads_out[0][2])
            upd1 = _dot(kw_t[j], heads_out[1][2])
            upd = jnp.where(_iota((LANES, GLA_V_DIM), 0) < GLA_K_DIM, upd0, upd1)
            s_list.append(decs[j] * s_list[j] + upd)
        st_ref[sl, :] = s_list[n_sub]
        for e in range(2):
            hh = 2 * p + e
            qm, o_h, _ = heads_out[e]
            inter = _dot(qm, s_list[0].astype(BF16))
            for j in range(1, n_sub):
                inter = jnp.where((rowl // CH) == j, _dot(qm, s_list[j].astype(BF16)), inter)
            o_h = o_h + inter
            o_h = o_h[0:tb, :]
            on = o_h * lax.rsqrt(jnp.mean(o_h * o_h, axis=-1, keepdims=True) + EPS) * ng_ref[...]
            vs = slice(hh * GLA_V_DIM, (hh + 1) * GLA_V_DIM)
            o_ref[:, vs] = (on * _silu(r_in[:, vs])).astype(o_ref.dtype)

    @pl.when(c == pl.num_programs(1) - 1)
    def _():
        sT_ref[...] = st_ref[...]


GLA_COL0 = 3 * D_SB + D_SSM + SSM_CONV_DIM
GLA_SHIFT = SSM_HEADS
GLA_WIDE = 1024
GLA_N_WIDE = (2 * GLA_HEADS * GLA_K_DIM + 2 * D_GLA) // GLA_WIDE


def _gla(proj, row0, nb, t, s0, gk_w, gk_b, norm_g, out_dtype, buf=None, col0=0):
    tb = min(t, TILE)
    nc = t // tb
    rb0 = row0 // tb
    has_init = s0 is not None
    KP = GLA_HEADS * GLA_K_DIM
    assert GLA_COL0 % GLA_WIDE == 0 and (GLA_COL0 + GLA_N_WIDE * GLA_WIDE) % LANES == 0

    def colblk(width, idx):
        return pl.BlockSpec((tb, width), lambda b, c: (rb0 + b * nc + c, idx))

    def full(shape):
        return pl.BlockSpec(shape, lambda b, c: (0,) * len(shape))

    in_specs = [colblk(GLA_WIDE, GLA_COL0 // GLA_WIDE + i) for i in range(GLA_N_WIDE)]
    in_specs.append(colblk(LANES, (GLA_COL0 + GLA_N_WIDE * GLA_WIDE) // LANES))
    args = [proj] * (GLA_N_WIDE + 1)
    if has_init:
        in_specs.append(pl.BlockSpec((None, KP, GLA_V_DIM), lambda b, c: (b, 0, 0)))
        args.append(s0)
    in_specs += [full((LANES, KP)), full((1, KP)), full((1, GLA_V_DIM))]
    gw_rows = jnp.zeros((LANES, KP), F32).at[GLA_SHIFT:GLA_SHIFT + GLA_RANK].set(gk_w)
    args += [gw_rows, gk_b.reshape(1, KP), norm_g.reshape(1, GLA_V_DIM)]
    aliases = {}
    o_shape = jax.ShapeDtypeStruct((nb * t, D_GLA), out_dtype)
    if buf is not None:
        assert col0 % D_GLA == 0
        in_specs.append(pl.BlockSpec(memory_space=pl.ANY))
        args.append(buf)
        aliases = {len(args) - 1: 0}
        o_shape = jax.ShapeDtypeStruct(buf.shape, buf.dtype)
    cb0 = col0 // D_GLA
    return pl.pallas_call(
        functools.partial(_gla_kernel, tb=tb, has_init=has_init, has_buf=buf is not None),
        grid=(nb, nc),
        in_specs=in_specs,
        out_specs=[pl.BlockSpec((tb, D_GLA), lambda b, c: (b * nc + c, cb0)),
                   pl.BlockSpec((None, KP, GLA_V_DIM), lambda b, c: (b, 0, 0))],
        out_shape=[o_shape, jax.ShapeDtypeStruct((nb, KP, GLA_V_DIM), F32)],
        input_output_aliases=aliases,
        scratch_shapes=[pltpu.VMEM((KP, GLA_V_DIM), F32)],
        compiler_params=_cparams(("arbitrary", "arbitrary")),
    )(*args)


def _put_rows_kernel(*refs):
    parts, o_ref = refs[:-2], refs[-1]
    off = 0
    for p in parts:
        w = p.shape[1]
        o_ref[:, off:off + w] = p[...].astype(o_ref.dtype)
        off += w


def _put_rows(buf, parts, row0):
    rows = parts[0].shape[0]
    width = sum(p.shape[1] for p in parts)
    assert width == buf.shape[1] and row0 % rows == 0
    return pl.pallas_call(
        _put_rows_kernel,
        grid=(1,),
        in_specs=[pl.BlockSpec(p.shape, lambda i: (0, 0)) for p in parts] + [pl.BlockSpec(memory_space=pl.ANY)],
        out_specs=pl.BlockSpec((rows, width), lambda i: (row0 // rows, 0)),
        out_shape=jax.ShapeDtypeStruct(buf.shape, buf.dtype),
        input_output_aliases={len(parts): 0},
        compiler_params=_cparams(("arbitrary",)),
    )(*parts, buf)


def _head_rows_kernel(*refs, depth):
    ins, (ko_ref, vo_ref) = refs[:2 * depth], refs[2 * depth:]
    tr = ins[0].shape[0]
    for l in range(depth):
        @pl.when(pl.program_id(0) == l)
        def _(l=l):
            for src, dst in ((ins[2 * l], ko_ref), (ins[2 * l + 1], vo_ref)):
                for hh in range(SB_HEADS):
                    dst[pl.ds(hh, tr, stride=SB_HEADS), :] = src[:, hh * SB_HEAD_DIM:(hh + 1) * SB_HEAD_DIM]


def _head_rows(projs, rows, tr):
    depth = len(projs)
    nt = rows // tr
    in_specs, args = [], []
    for l, p in enumerate(projs):
        for cb in (1, 2):
            in_specs.append(pl.BlockSpec((tr, D_SB), lambda ll, i, l=l, cb=cb: (jnp.where(ll == l, i, 0), cb)))
            args.append(p)
    out_spec = pl.BlockSpec((tr * SB_HEADS, SB_HEAD_DIM), lambda ll, i: (ll * nt + i, 0))
    shape = jax.ShapeDtypeStruct((depth * rows * SB_HEADS, SB_HEAD_DIM), F32)
    return pl.pallas_call(
        functools.partial(_head_rows_kernel, depth=depth),
        grid=(depth, nt),
        in_specs=in_specs,
        out_specs=[out_spec, out_spec],
        out_shape=[shape, shape],
        compiler_params=_cparams(("arbitrary", "arbitrary")),
    )(*args)


def kernel(x_prompt, x_sample, cache_k, cache_v, page_table, state_conv, state_ssm, state_gla, norm1_g, w_in, sb_bias, conv_w, conv_b, dt_bias, a_log, d_skip, ssm_norm_g, gla_gk_w, gla_gk_b, gla_norm_g, w_out, norm2_g, w_gate, w_up, w_down, final_norm_g):
    bp, tp, d = x_prompt.shape
    bs, ts, _ = x_sample.shape
    depth = w_in.shape[0]
    mp = bp * tp
    ms = bs * ts
    m = mp + ms
    tm = _row_tile(m, 1376)
    tm_s = _row_tile(m, 688)
    d_ff = w_gate.shape[2]
    KP = GLA_HEADS * GLA_K_DIM
    off_xbc = 3 * D_SB + D_SSM
    n_tail = SSM_CONV - 1

    x = jnp.concatenate([x_prompt.reshape(mp, d), x_sample.reshape(ms, d)], axis=0)
    w_down_bf = _cast_bf16(w_down, _row_tile(d_ff, 688))
    w_in_nk = jnp.swapaxes(w_in, 1, 2)
    outs = {k: [] for k in ("cp", "sp", "gp", "ks", "vs", "cs", "ss", "gs")}
    projs = []

    for l in range(depth):
        xn = _rmsnorm(x, norm1_g[l], BF16, tm_s)
        proj = _matmul(xn, w_in_nk, l, None, tm, 512, w_is_nk=True)


        mixed = jnp.zeros((m, d), BF16)
        mixed = _sb_prompt(proj, sb_bias[l], bp, tp, mixed, D_SSM)
        ctx_p = jnp.zeros((bp, 8, SSM_CONV_DIM), F32)
        mixed, ssm_p = _ssd(proj, 0, bp, tp, ctx_p, None, conv_w[l], conv_b[l], dt_bias[l], a_log[l],
                            d_skip[l], ssm_norm_g[l], BF16, buf=mixed, col0=0)
        mixed, gla_p = _gla(proj, 0, bp, tp, None,
                            gla_gk_w[l], gla_gk_b[l], gla_norm_g[l], BF16, buf=mixed, col0=D_SSM + D_SB)

        q_s = proj[mp:, 0:D_SB].reshape(bs, ts, D_SB)
        k_s = proj[mp:, D_SB:2 * D_SB].reshape(bs, ts, D_SB)
        v_s = proj[mp:, 2 * D_SB:3 * D_SB].reshape(bs, ts, D_SB)
        o_sb_s = _sb_sample(q_s, k_s, v_s, sb_bias[l], cache_k, cache_v, page_table, l)
        ctx_s = jnp.pad(state_conv[l], ((0, 0), (8 - (SSM_CONV - 1), 0), (0, 0)))
        y_ssm_s, ssm_s = _ssd(proj, mp, bs, ts, ctx_s, state_ssm[l].reshape(bs, D_SSM, SSM_STATE),
                              conv_w[l], conv_b[l], dt_bias[l], a_log[l], d_skip[l], ssm_norm_g[l], F32)
        o_gl_s, gla_s = _gla(proj, mp, bs, ts, state_gla[l].reshape(bs, KP, GLA_V_DIM),
                             gla_gk_w[l], gla_gk_b[l], gla_norm_g[l], F32)

        mixed = _put_rows(mixed, [y_ssm_s, o_sb_s, o_gl_s], mp)
        x = _matmul(mixed, w_out, l, x, tm, 512,
                    k_order=((D_SB, D_SB + D_SSM), (0, D_SB), (D_SB + D_SSM, D_SB + D_SSM + D_GLA)))
        hn = _rmsnorm(x, norm2_g[l], BF16, tm_s)
        hmid = _gate_up(hn, w_gate, w_up, l, tm, 256)
        x = _matmul_bf16w(hmid, w_down_bf, l, x, tm_s, 512)

        xbc_p = jnp.stack([proj[(b + 1) * tp - n_tail:(b + 1) * tp, off_xbc:off_xbc + SSM_CONV_DIM]
                           for b in range(bp)])
        xbc_s = jnp.concatenate([state_conv[l], proj[mp:, off_xbc:off_xbc + SSM_CONV_DIM].reshape(bs, ts, -1)], axis=1)
        projs.append(proj)
        outs["cp"].append(xbc_p)
        outs["sp"].append(ssm_p.reshape(bp, SSM_HEADS, SSM_HEAD_DIM, SSM_STATE))
        outs["gp"].append(gla_p.reshape(bp, GLA_HEADS, GLA_K_DIM, GLA_V_DIM))
        outs["ks"].append(k_s.reshape(bs, ts, SB_HEADS, SB_HEAD_DIM))
        outs["vs"].append(v_s.reshape(bs, ts, SB_HEADS, SB_HEAD_DIM))
        outs["cs"].append(xbc_s[:, xbc_s.shape[1] - n_tail:])
        outs["ss"].append(ssm_s.reshape(bs, SSM_HEADS, SSM_HEAD_DIM, SSM_STATE))
        outs["gs"].append(gla_s.reshape(bs, GLA_HEADS, GLA_K_DIM, GLA_V_DIM))

    y_p = _rmsnorm(x, final_norm_g, F32, _row_tile(mp, 512), rows=mp)
    y_s = _rmsnorm(x[mp:], final_norm_g, F32, ms)
    st = {k: jnp.stack(v) for k, v in outs.items()}
    kp, vp = _head_rows(projs, mp, _row_tile(mp, 512))
    return (y_p.reshape(bp, tp, d), y_s.reshape(bs, ts, d),
            kp.reshape(depth, bp, tp, SB_HEADS, SB_HEAD_DIM), vp.reshape(depth, bp, tp, SB_HEADS, SB_HEAD_DIM),
            st["cp"], st["sp"], st["gp"],
            st["ks"], st["vs"], st["cs"], st["ss"], st["gs"])
```

```python
import functools

import jax
import jax.numpy as jnp
from jax import lax
from jax.experimental import pallas as pl
from jax.experimental.pallas import tpu as pltpu

F32 = jnp.float32
BF16 = jnp.bfloat16

EPS = 1e-6
SB_HEADS = 8
SB_HEAD_DIM = 128
D_SB = SB_HEADS * SB_HEAD_DIM
SSM_HEADS = 32
SSM_HEAD_DIM = 64
D_SSM = SSM_HEADS * SSM_HEAD_DIM
SSM_STATE = 128
SSM_GROUPS = 4
SSM_CONV = 4
SSM_CONV_DIM = D_SSM + 2 * SSM_GROUPS * SSM_STATE
GLA_HEADS = 8
GLA_K_DIM = 64
GLA_V_DIM = 128
D_GLA = GLA_HEADS * GLA_V_DIM
GLA_RANK = 16
GLA_TAU = 16.0
GLA_CHUNK = 64
PAGE_SIZE = 128

LANES = 128
TILE = 128
VMEM_LIMIT = 60 * 1024 * 1024


def _cparams(sem):
    return pltpu.CompilerParams(dimension_semantics=sem, vmem_limit_bytes=VMEM_LIMIT)


def _softplus(z):
    return jnp.maximum(z, 0.0) + jnp.log(1.0 + jnp.exp(-jnp.abs(z)))


def _silu(x):
    return x * (0.5 + 0.5 * jnp.tanh(0.5 * x))


def _split2(d):
    d0 = d.astype(BF16)
    d1 = (d - d0.astype(F32)).astype(BF16)
    return d0, d1


def _dot(a, b):
    return jnp.dot(a, b, preferred_element_type=F32)


def _dot_nt(a, b):
    return lax.dot_general(a, b, (((1,), (1,)), ((), ())), preferred_element_type=F32)


def _sel_dot_r(data, sel):
    d0, d1 = _split2(data)
    return _dot(d0, sel) + _dot(d1, sel)


def _sel_dot_l(sel, data):
    d0, d1 = _split2(data)
    return _dot(sel, d0) + _dot(sel, d1)


def _iota(shape, dim):
    return lax.broadcasted_iota(jnp.int32, shape, dim)


def _rmsnorm_kernel(x_ref, g_ref, o_ref):
    x = x_ref[...]
    y = x * lax.rsqrt(jnp.mean(x * x, axis=-1, keepdims=True) + EPS)
    o_ref[...] = (y * g_ref[...]).astype(o_ref.dtype)


def _row_tile(m, target):
    best = None
    for t in range(16, target + 1, 16):
        if m % t == 0:
            best = t
    assert best is not None, (m, target)
    return best


def _rmsnorm(x, g, out_dtype, tm, rows=None):
    m, d = x.shape
    m = m if rows is None else rows
    return pl.pallas_call(
        _rmsnorm_kernel,
        grid=(m // tm,),
        in_specs=[pl.BlockSpec((tm, d), lambda i: (i, 0)), pl.BlockSpec((1, d), lambda i: (0, 0))],
        out_specs=pl.BlockSpec((tm, d), lambda i: (i, 0)),
        out_shape=jax.ShapeDtypeStruct((m, d), out_dtype),
        compiler_params=_cparams(("arbitrary",)),
    )(x, g.reshape(1, d))


def _mm_kernel(x_ref, w_ref, *rest, has_res, w_is_nk, k_order):
    if has_res:
        res_ref, o_ref, wbf_ref = rest
    else:
        o_ref, wbf_ref = rest

    @pl.when(pl.program_id(1) == 0)
    def _():
        if k_order is None:
            wbf_ref[...] = w_ref[...].astype(BF16)
        else:
            off = 0
            for a, b in k_order:
                wbf_ref[off:off + b - a, :] = w_ref[a:b, :].astype(BF16)
                off += b - a

    acc = _dot_nt(x_ref[...], wbf_ref[...]) if w_is_nk else _dot(x_ref[...], wbf_ref[...])
    if has_res:
        acc = acc + res_ref[...]
    o_ref[...] = acc.astype(o_ref.dtype)


def _matmul(x, w, layer, res, tm, tn, w_is_nk=False, k_order=None):
    m, k = x.shape
    assert k_order is None or (not w_is_nk and sum(b - a for a, b in k_order) == k)
    if w_is_nk:
        n = w.shape[1]
        w_spec = pl.BlockSpec((None, tn, k), lambda j, i: (layer, j, 0))
        w_scratch = pltpu.VMEM((tn, k), BF16)
    else:
        n = w.shape[2]
        w_spec = pl.BlockSpec((None, k, tn), lambda j, i: (layer, 0, j))
        w_scratch = pltpu.VMEM((k, tn), BF16)
    in_specs = [pl.BlockSpec((tm, k), lambda j, i: (i, 0)), w_spec]
    args = [x, w]
    if res is not None:
        in_specs.append(pl.BlockSpec((tm, tn), lambda j, i: (i, j)))
        args.append(res)
    return pl.pallas_call(
        functools.partial(_mm_kernel, has_res=res is not None, w_is_nk=w_is_nk, k_order=k_order),
        grid=(pl.cdiv(n, tn), m // tm),
        in_specs=in_specs,
        out_specs=pl.BlockSpec((tm, tn), lambda j, i: (i, j)),
        out_shape=jax.ShapeDtypeStruct((m, n), F32),
        scratch_shapes=[w_scratch],
        compiler_params=_cparams(("arbitrary", "arbitrary")),
    )(*args)


MM_SLABS = 4


def _slabs(rows, n):
    step = -(-rows // (16 * n)) * 16
    bounds = [min(i * step, rows) for i in range(n + 1)]
    return [(a, b) for a, b in zip(bounds[:-1], bounds[1:]) if b > a]


def _gate_up_kernel(x_ref, wg_ref, wu_ref, o_ref, wbf_ref, *, tn):
    @pl.when(pl.program_id(1) == 0)
    def _():
        wbf_ref[:, :tn] = wg_ref[...].astype(BF16)
        wbf_ref[:, tn:] = wu_ref[...].astype(BF16)

    for r0, r1 in _slabs(o_ref.shape[0], MM_SLABS):
        gu = _dot(x_ref[r0:r1, :], wbf_ref[...])
        o_ref[r0:r1, :] = (_silu(gu[:, :tn]) * gu[:, tn:]).astype(o_ref.dtype)


def _gate_up(x, wg, wu, layer, tm, tn):
    m, k = x.shape
    f = wg.shape[2]
    wspec = pl.BlockSpec((None, k, tn), lambda j, i: (layer, 0, j))
    return pl.pallas_call(
        functools.partial(_gate_up_kernel, tn=tn),
        grid=(f // tn, m // tm),
        in_specs=[pl.BlockSpec((tm, k), lambda j, i: (i, 0)), wspec, wspec],
        out_specs=pl.BlockSpec((tm, tn), lambda j, i: (i, j)),
        out_shape=jax.ShapeDtypeStruct((m, f), BF16),
        scratch_shapes=[pltpu.VMEM((k, 2 * tn), BF16)],
        compiler_params=_cparams(("arbitrary", "arbitrary")),
    )(x, wg, wu)


def _cast_kernel(w_ref, o_ref):
    o_ref[...] = w_ref[...].astype(o_ref.dtype)


def _cast_bf16(w, tr):
    nl, r, c = w.shape
    spec = pl.BlockSpec((None, tr, c), lambda l, i: (l, i, 0))
    return pl.pallas_call(
        _cast_kernel,
        grid=(nl, r // tr),
        in_specs=[spec],
        out_specs=spec,
        out_shape=jax.ShapeDtypeStruct(w.shape, BF16),
        compiler_params=_cparams(("arbitrary", "arbitrary")),
    )(w)


def _mm_res_kernel(x_ref, w_ref, res_ref, o_ref):
    o_ref[...] = _dot(x_ref[...], w_ref[...]) + res_ref[...]


def _matmul_bf16w(x, w, layer, res, tm, tn):
    m, k = x.shape
    n = w.shape[2]
    return pl.pallas_call(
        _mm_res_kernel,
        grid=(m // tm, n // tn),
        in_specs=[pl.BlockSpec((tm, k), lambda i, j: (i, 0)),
                  pl.BlockSpec((None, k, tn), lambda i, j: (layer, 0, j)),
                  pl.BlockSpec((tm, tn), lambda i, j: (i, j))],
        out_specs=pl.BlockSpec((tm, tn), lambda i, j: (i, j)),
        out_shape=jax.ShapeDtypeStruct((m, n), F32),
        compiler_params=_cparams(("arbitrary", "arbitrary")),
    )(x, w, res)


SB_KEYS = 256
SB_UNROLL = 2

def _sb_prompt_kernel(bias_ref, q_ref, k_ref, v_ref, buf_ref, o_ref, acc_ref, run_ref, *, tq, scale):
    del buf_ref
    h = pl.program_id(1)
    i = pl.program_id(2)
    nk = SB_KEYS
    bias = bias_ref[h]
    q = (q_ref[...] * scale).astype(BF16)
    later = _iota((nk, nk), 0) > _iota((nk, nk), 1)
    suffix_total = jnp.concatenate([later.astype(BF16), jnp.ones((nk, LANES), BF16)], axis=1)

    def block(qv, start, run, mask):
        kb = k_ref[pl.ds(start, nk), :].astype(BF16)
        vb = v_ref[pl.ds(start, nk), :].astype(BF16)
        z = _dot_nt(qv, kb) + bias
        sp = _softplus(z)
        lg = -sp if mask is None else jnp.where(mask, -sp, 0.0)
        st = _dot(lg.astype(BF16), suffix_total)
        e = jnp.exp(z - sp + st[:, :nk] + jnp.concatenate([run] * (nk // LANES), axis=1))
        w = e if mask is None else jnp.where(mask, e, 0.0)
        return _dot(w.astype(BF16), vb), run + st[:, nk:]

    acc_ref[...] = jnp.zeros_like(acc_ref)
    run_ref[...] = jnp.zeros_like(run_ref)
    q0 = pl.multiple_of(i * tq, tq)
    for jd in reversed(range(tq // nk)):
        r0 = jd * nk
        rows = tq - r0
        mask = _iota((rows, nk), 1) < _iota((rows, nk), 0)
        pv, run = block(q[r0:], q0 + r0, run_ref[r0:, :], mask)
        acc_ref[r0:, :] += pv
        run_ref[r0:, :] = run

    def body(g, carry):
        run = run_ref[...]
        total = None
        for u in range(SB_UNROLL):
            start = pl.multiple_of(q0 - (g * SB_UNROLL + u + 1) * nk, nk)
            pv, run = block(q, start, run, None)
            total = pv if total is None else total + pv
        acc_ref[...] += total
        run_ref[...] = run
        return carry

    lax.fori_loop(0, (i * tq) // (SB_UNROLL * nk), body, 0)
    o_ref[...] = acc_ref[...].astype(o_ref.dtype)


def _sb_prompt(proj, bias, nb, t, buf, col0, tq=1024):
    assert t % tq == 0 and tq % (SB_UNROLL * SB_KEYS) == 0 and col0 % SB_HEAD_DIM == 0
    nq = t // tq
    hb = SB_HEADS
    cb0 = col0 // SB_HEAD_DIM
    return pl.pallas_call(
        functools.partial(_sb_prompt_kernel, tq=tq, scale=SB_HEAD_DIM ** -0.5),
        grid=(nb, hb, nq),
        in_specs=[pl.BlockSpec(memory_space=pltpu.SMEM),
                  pl.BlockSpec((tq, SB_HEAD_DIM), lambda b, h, i: (b * nq + i, h)),
                  pl.BlockSpec((t, SB_HEAD_DIM), lambda b, h, i: (b, hb + h)),
                  pl.BlockSpec((t, SB_HEAD_DIM), lambda b, h, i: (b, 2 * hb + h)),
                  pl.BlockSpec(memory_space=pl.ANY)],
        out_specs=pl.BlockSpec((tq, SB_HEAD_DIM), lambda b, h, i: (b * nq + i, cb0 + h)),
        out_shape=jax.ShapeDtypeStruct(buf.shape, buf.dtype),
        input_output_aliases={4: 0},
        scratch_shapes=[pltpu.VMEM((tq, SB_HEAD_DIM), F32), pltpu.VMEM((tq, LANES), F32)],
        compiler_params=_cparams(("arbitrary", "arbitrary", "arbitrary")),
    )(bias, proj, proj, proj, buf)


def _sb_sample_kernel(pt_ref, qbd_ref, bias_ref, kown_ref, vown_ref, *refs, pp, nq):
    del pt_ref
    kp = refs[:pp]
    vp = refs[pp:2 * pp]
    o_ref = refs[2 * pp]
    acc_ref, run_ref = refs[2 * pp + 1:]
    s = pl.program_id(1)
    n = PAGE_SIZE
    rows_c = 2 * nq
    earlier = (_iota((n, n), 1) > _iota((n, n), 0)).astype(BF16)
    qbd = qbd_ref[...]
    bias = bias_ref[...]

    def head_rows(ref, hh):
        return ref[pl.ds(hh, n, stride=SB_HEADS), :].astype(BF16)

    def scores(k_ref, mask):
        kcat = jnp.concatenate([head_rows(k_ref, hh) for hh in range(SB_HEADS)], axis=1)
        z = _dot(kcat, qbd) + bias
        sp = _softplus(z)
        lg = -sp if mask is None else jnp.where(mask, -sp, 0.0)
        return z - sp, _dot(earlier, lg.astype(BF16)), jnp.sum(lg, axis=0, keepdims=True)

    def add_values(v_ref, w, acc):
        wt = w.T.astype(BF16)
        return [acc[hh] + _dot(wt[(hh // 2) * rows_c:(hh // 2 + 1) * rows_c, :], head_rows(v_ref, hh))
                for hh in range(SB_HEADS)]

    @pl.when(s == 0)
    def _():
        mask = _iota((n, LANES), 0) < (_iota((n, LANES), 1) % nq)
        log_beta, local, total = scores(kown_ref, mask)
        w = jnp.where(mask, jnp.exp(log_beta + local), 0.0)
        acc = add_values(vown_ref, w, [jnp.zeros((rows_c, SB_HEAD_DIM), F32)] * SB_HEADS)
        for hh in range(SB_HEADS):
            acc_ref[hh] = acc[hh]
        run_ref[...] = total

    kcat = jnp.concatenate(
        [jnp.concatenate([head_rows(kp[c], hh) for hh in range(SB_HEADS)], axis=1) for c in range(pp)], axis=0)
    z = _dot(kcat, qbd) + bias
    sp = _softplus(z)
    lg = -sp
    lgb = lg.astype(BF16)
    local = jnp.concatenate([_dot(earlier, lgb[c * n:(c + 1) * n]) for c in range(pp)], axis=0)
    run = run_ref[...]
    runs = []
    for c in range(pp):
        runs.append(jnp.broadcast_to(run, (n, LANES)))
        run = run + jnp.sum(lg[c * n:(c + 1) * n], axis=0, keepdims=True)
    run_ref[...] = run
    w = jnp.exp(z - sp + local + jnp.concatenate(runs, axis=0))
    wt = w.T.astype(BF16)
    for hh in range(SB_HEADS):
        vh = jnp.concatenate([head_rows(vp[c], hh) for c in range(pp)], axis=0)
        acc_ref[hh] += _dot(wt[(hh // 2) * rows_c:(hh // 2 + 1) * rows_c, :], vh)

    @pl.when(s == pl.num_programs(1) - 1)
    def _():
        for hh in range(SB_HEADS):
            e = hh % 2
            o_ref[:, hh * SB_HEAD_DIM:(hh + 1) * SB_HEAD_DIM] = acc_ref[hh, e * nq:(e + 1) * nq, :]


def _sb_sample(q, k_own, v_own, bias, cache_k, cache_v, page_table, layer, pp=8):
    db, nq, _ = q.shape
    assert 2 * nq == 16 and SB_HEADS * nq <= LANES and cache_k.shape[2:] == (PAGE_SIZE, SB_HEADS, SB_HEAD_DIM)
    depth, n_pool = cache_k.shape[0], cache_k.shape[1]
    n_pages = page_table.shape[1]
    scale = SB_HEAD_DIM ** -0.5
    qh = (q * scale).reshape(db, nq, SB_HEADS, SB_HEAD_DIM)
    eye = jnp.eye(SB_HEADS, dtype=F32)
    qbd = jnp.einsum('bthd,hg->bhdgt', qh, eye).reshape(db, D_SB, SB_HEADS * nq)
    qbd = jnp.pad(qbd, ((0, 0), (0, 0), (0, LANES - SB_HEADS * nq))).astype(BF16)
    bias_l = jnp.pad(jnp.repeat(bias.astype(F32), nq), (0, LANES - SB_HEADS * nq)).reshape(1, LANES)
    pad = ((0, 0), (0, PAGE_SIZE - nq), (0, 0))
    page_rows = PAGE_SIZE * SB_HEADS
    kown = jnp.pad(k_own, pad).reshape(db * page_rows, SB_HEAD_DIM)
    vown = jnp.pad(v_own, pad).reshape(db * page_rows, SB_HEAD_DIM)
    ck = cache_k.reshape(depth * n_pool * page_rows, SB_HEAD_DIM)
    cv = cache_v.reshape(depth * n_pool * page_rows, SB_HEAD_DIM)
    base = layer * n_pool

    def page_map(c):
        return lambda b, s, pt: (base + pt[b, n_pages - 1 - (s * pp + c)], 0)

    page_spec = [pl.BlockSpec((page_rows, SB_HEAD_DIM), page_map(c)) for c in range(pp)]
    own_spec = pl.BlockSpec((page_rows, SB_HEAD_DIM), lambda b, s, pt: (b, 0))
    grid_spec = pltpu.PrefetchScalarGridSpec(
        num_scalar_prefetch=1,
        grid=(db, n_pages // pp),
        in_specs=[pl.BlockSpec((None, D_SB, LANES), lambda b, s, pt: (b, 0, 0)),
                  pl.BlockSpec((1, LANES), lambda b, s, pt: (0, 0)),
                  own_spec, own_spec] + page_spec + page_spec,
        out_specs=pl.BlockSpec((nq, D_SB), lambda b, s, pt: (b, 0)),
        scratch_shapes=[pltpu.VMEM((SB_HEADS, 2 * nq, SB_HEAD_DIM), F32),
                        pltpu.VMEM((1, LANES), F32)],
    )
    return pl.pallas_call(
        functools.partial(_sb_sample_kernel, pp=pp, nq=nq),
        grid_spec=grid_spec,
        out_shape=jax.ShapeDtypeStruct((db * nq, D_SB), F32),
        compiler_params=_cparams(("arbitrary", "arbitrary")),
    )(page_table, qbd, bias_l, kown, vown, *([ck] * pp), *([cv] * pp))


def _ssd_kernel(*refs, tb, has_init, has_buf):
    if has_buf:
        refs = refs[:-6] + refs[-5:]
    if has_init:
        (z0_ref, z1_ref, xa_ref, xb_ref, bc_ref, dt_ref, ctx_ref, h0_ref, cw_ref, cb_ref, dtb_ref, alog_ref,
         dsk_ref, ng_ref, spr_ref, sprt_ref, y_ref, hT_ref, xp_ref, hs_ref, yb_ref) = refs
    else:
        (z0_ref, z1_ref, xa_ref, xb_ref, bc_ref, dt_ref, ctx_ref, cw_ref, cb_ref, dtb_ref, alog_ref,
         dsk_ref, ng_ref, spr_ref, sprt_ref, y_ref, hT_ref, xp_ref, hs_ref, yb_ref) = refs
        h0_ref = None
    c = pl.program_id(1)
    L = TILE
    P2 = LANES
    n_pairs = D_SSM // P2

    @pl.when(c == 0)
    def _():
        xp_ref[0:8, :] = ctx_ref[...]
        if has_init:
            hs_ref[...] = h0_ref[...]
        else:
            hs_ref[...] = jnp.zeros_like(hs_ref)

    xp_ref[8:8 + tb, 0:1024] = xa_ref[...]
    xp_ref[8:8 + tb, 1024:2048] = xb_ref[...]
    xp_ref[8:8 + tb, 2048:3072] = bc_ref[...]
    if tb < L:
        xp_ref[8 + tb:8 + L, :] = jnp.zeros((L - tb, SSM_CONV_DIM), F32)

    conv = cb_ref[...] + cw_ref[0:1, :] * xp_ref[5:5 + L, :]
    for i in range(1, SSM_CONV):
        conv = conv + cw_ref[i:i + 1, :] * xp_ref[5 + i:5 + i + L, :]
    xp_ref[0:8, :] = xp_ref[tb:tb + 8, :]
    xc = _silu(conv)

    row = _iota((L, LANES), 0)
    lane = _iota((L, LANES), 1)
    dt_raw = dt_ref[...]
    if tb < L:
        dt_raw = jnp.concatenate([dt_raw, jnp.zeros((L - tb, LANES), F32)], axis=0)
    valid = (lane < SSM_HEADS) & (row < tb)
    dtv = jnp.where(valid, _softplus(dt_raw + dtb_ref[...]), 0.0)
    a = dtv * (-jnp.exp(alog_ref[...]))
    lower = (_iota((L, L), 0) >= _iota((L, L), 1))
    lower_b = lower.astype(BF16)
    upper_b = (_iota((L, L), 0) <= _iota((L, L), 1)).astype(BF16)
    cum = _sel_dot_l(lower_b, a)
    cum_t = _sel_dot_r(a.T, upper_b)
    last = cum[L - 1:L, :]
    spread = spr_ref[...]
    spread_t = sprt_ref[...]
    dt_x = _sel_dot_r(dtv, spread)
    ecum_x = _sel_dot_r(jnp.exp(cum), spread)
    wdec_x = _sel_dot_r(jnp.exp(last - cum), spread)
    sdec = _sel_dot_l(spread_t, jnp.broadcast_to(jnp.exp(cum_t[:, L - 1:L]), (LANES, LANES)))

    first_half = lane < SSM_HEAD_DIM
    pairs_per_group = n_pairs // SSM_GROUPS
    for g in range(SSM_GROUPS):
        bg = xc[:, D_SSM + g * SSM_STATE:D_SSM + (g + 1) * SSM_STATE]
        cg = xc[:, D_SSM + (SSM_GROUPS + g) * SSM_STATE:D_SSM + (SSM_GROUPS + g + 1) * SSM_STATE]
        bg_b = bg.astype(BF16)
        cg_b = cg.astype(BF16)
        cb = _dot_nt(cg_b, bg_b)
        for r in range(pairs_per_group):
            p = g * pairs_per_group + r
            sl = slice(p * P2, (p + 1) * P2)
            x_p = xc[:, sl]
            xs_p = x_p * dt_x[:, sl]
            xs_b = xs_p.astype(BF16)
            ys = []
            for e in range(2):
                hh = 2 * p + e
                seg = jnp.broadcast_to(cum[:, hh:hh + 1], (L, L)) - jnp.broadcast_to(cum_t[hh:hh + 1, :], (L, L))
                dec = jnp.where(lower, jnp.exp(jnp.minimum(seg, 0.0)), 0.0)
                ys.append(_dot((cb * dec).astype(BF16), xs_b))
            y_p = jnp.where(first_half, ys[0], ys[1])
            h_p = hs_ref[sl, :]
            y_p = y_p + _dot_nt(cg_b, h_p.astype(BF16)) * ecum_x[:, sl]
            xw_t = (xs_p * wdec_x[:, sl]).T.astype(BF16)
            hs_ref[sl, :] = sdec[sl, :] * h_p + _dot(xw_t, bg_b)
            yb_ref[:, sl] = y_p + dsk_ref[:, sl] * x_p

    zfull = jnp.concatenate([z0_ref[...], z1_ref[...]], axis=1)
    y = yb_ref[0:tb, :] * _silu(zfull)
    gw = D_SSM // SSM_GROUPS
    for g in range(SSM_GROUPS):
        yg = y[:, g * gw:(g + 1) * gw]
        yn = yg * lax.rsqrt(jnp.mean(yg * yg, axis=-1, keepdims=True) + EPS)
        y_ref[:, g * gw:(g + 1) * gw] = (yn * ng_ref[:, g * gw:(g + 1) * gw]).astype(y_ref.dtype)

    @pl.when(c == pl.num_programs(1) - 1)
    def _():
        hT_ref[...] = hs_ref[...]


def _ssd(proj, row0, nb, t, ctx8, h0, conv_w, conv_b, dt_bias, a_log, d_skip, norm_g, out_dtype, buf=None, col0=0):
    tb = min(t, TILE)
    nc = t // tb
    rb0 = row0 // tb
    has_init = h0 is not None

    def colblk(width, idx):
        return pl.BlockSpec((tb, width), lambda b, c: (rb0 + b * nc + c, idx))

    def full(shape):
        return pl.BlockSpec(shape, lambda b, c: (0,) * len(shape))

    in_specs = [colblk(1024, 3), colblk(1024, 4), colblk(1024, 5), colblk(1024, 6), colblk(1024, 7),
                colblk(LANES, 64),
                pl.BlockSpec((None, 8, SSM_CONV_DIM), lambda b, c: (b, 0, 0))]
    args = [proj] * 6 + [ctx8]
    if has_init:
        in_specs.append(pl.BlockSpec((None, D_SSM, SSM_STATE), lambda b, c: (b, 0, 0)))
        args.append(h0)
    padl = (0, LANES - SSM_HEADS)
    in_specs += [full((SSM_CONV, SSM_CONV_DIM)), full((1, SSM_CONV_DIM)), full((1, LANES)), full((1, LANES)),
                 full((1, D_SSM)), full((1, D_SSM)), full((LANES, D_SSM)), full((D_SSM, LANES))]
    spread = (jnp.arange(LANES)[:, None] == jnp.arange(D_SSM)[None, :] // SSM_HEAD_DIM).astype(BF16)
    args += [conv_w, conv_b.reshape(1, -1), jnp.pad(dt_bias, padl).reshape(1, LANES),
             jnp.pad(a_log, padl).reshape(1, LANES), jnp.repeat(d_skip, SSM_HEAD_DIM).reshape(1, D_SSM),
             norm_g.reshape(1, D_SSM), spread, spread.T]
    aliases = {}
    y_shape = jax.ShapeDtypeStruct((nb * t, D_SSM), out_dtype)
    if buf is not None:
        assert col0 % D_SSM == 0 and row0 == 0
        in_specs.append(pl.BlockSpec(memory_space=pl.ANY))
        args.append(buf)
        aliases = {len(args) - 1: 0}
        y_shape = jax.ShapeDtypeStruct(buf.shape, buf.dtype)
    cb0 = col0 // D_SSM
    return pl.pallas_call(
        functools.partial(_ssd_kernel, tb=tb, has_init=has_init, has_buf=buf is not None),
        grid=(nb, nc),
        in_specs=in_specs,
        out_specs=[pl.BlockSpec((tb, D_SSM), lambda b, c: (b * nc + c, cb0)),
                   pl.BlockSpec((None, D_SSM, SSM_STATE), lambda b, c: (b, 0, 0))],
        out_shape=[y_shape, jax.ShapeDtypeStruct((nb, D_SSM, SSM_STATE), F32)],
        input_output_aliases=aliases,
        scratch_shapes=[pltpu.VMEM((8 + TILE, SSM_CONV_DIM), F32),
                        pltpu.VMEM((D_SSM, SSM_STATE), F32),
                        pltpu.VMEM((TILE, D_SSM), F32)],
        compiler_params=_cparams(("arbitrary", "arbitrary")),
    )(*args)


def _gla_kernel(*refs, tb, has_init, has_buf):
    if has_buf:
        refs = refs[:-4] + refs[-3:]
    wide, tail_ref, refs = refs[:GLA_N_WIDE], refs[GLA_N_WIDE], refs[GLA_N_WIDE + 1:]
    if has_init:
        (s0_ref, gw_ref, gb_ref, ng_ref, o_ref, sT_ref, st_ref) = refs
    else:
        (gw_ref, gb_ref, ng_ref, o_ref, sT_ref, st_ref) = refs
        s0_ref = None
    c = pl.program_id(1)
    L = TILE
    tail = tail_ref[...]
    region = jnp.concatenate([w[...] for w in wide] + [tail], axis=1)
    region = pltpu.roll(region, region.shape[1] - GLA_SHIFT, 1)
    kp_, o0 = GLA_HEADS * GLA_K_DIM, 0
    q_in = region[:, o0:o0 + kp_]
    k_in = region[:, o0 + kp_:o0 + 2 * kp_]
    v_in = region[:, o0 + 2 * kp_:o0 + 2 * kp_ + D_GLA]
    r_in = region[:, o0 + 2 * kp_ + D_GLA:o0 + 2 * kp_ + 2 * D_GLA]
    tail_lane = _iota(tail.shape, 1)
    gate_in = jnp.where((tail_lane >= GLA_SHIFT) & (tail_lane < GLA_SHIFT + GLA_RANK), tail, 0.0)
    CH = min(GLA_CHUNK, tb)
    n_sub = max(tb // CH, 1)
    KP = GLA_HEADS * GLA_K_DIM

    @pl.when(c == 0)
    def _():
        if has_init:
            st_ref[...] = s0_ref[...]
        else:
            st_ref[...] = jnp.zeros_like(st_ref)

    def padrows(x):
        if tb < L:
            return jnp.concatenate([x, jnp.zeros((L - tb, x.shape[1]), F32)], axis=0)
        return x

    row = _iota((L, L), 0)
    col = _iota((L, L), 1)
    same_chunk = (row // CH) == (col // CH)
    lower = (row >= col) & same_chunk
    lower_b = lower.astype(BF16)
    rvalid = _iota((L, KP), 0) < tb
    glog = _dot(gate_in.astype(BF16), gw_ref[...].astype(BF16)) + gb_ref[...]
    glog = padrows(-_softplus(-glog) * (1.0 / GLA_TAU))
    glog = jnp.where(rvalid, glog, 0.0)
    q_all = padrows(q_in) * (GLA_K_DIM ** -0.5)
    k_all = jnp.where(rvalid, padrows(k_in), 0.0)
    v_all = padrows(v_in)
    lane = _iota((L, LANES), 1)
    rowl = _iota((L, LANES), 0)

    b_all = _sel_dot_l(lower_b, glog)
    qt_all = q_all * jnp.exp(b_all)
    kt_all = (k_all * jnp.exp(-b_all)).astype(BF16)
    lasts_all = [b_all[min((j + 1) * CH, L) - 1:min((j + 1) * CH, L), :] for j in range(n_sub)]
    chunk_of_row = _iota((L, KP), 0) // CH
    kw_all = [jnp.where(chunk_of_row == j, k_all * jnp.exp(lasts_all[j] - b_all), 0.0) for j in range(n_sub)]
    normed = []

    for p in range(KP // LANES):
        sl = slice(p * LANES, (p + 1) * LANES)
        qt = qt_all[:, sl]
        kt = kt_all[:, sl]
        lasts = [la[:, sl] for la in lasts_all]
        s_cur = st_ref[sl, :]
        s_list = [s_cur]
        kw_t = [kw[:, sl].T.astype(BF16) for kw in kw_all]
        heads_out = []
        new_states = [[None, None] for _ in range(n_sub)]
        for e in range(2):
            hh = 2 * p + e
            own = (lane // GLA_K_DIM) == e
            qm = jnp.where(own, qt, 0.0).astype(BF16)
            att = jnp.where(lower, _dot_nt(qm, kt), 0.0)
            v_h = v_all[:, hh * GLA_V_DIM:(hh + 1) * GLA_V_DIM].astype(BF16)
            heads_out.append((qm, _dot(att.astype(BF16), v_h), v_h))
        decs = []
        for j in range(n_sub):
            dcol = jnp.exp(lasts[j]).reshape(1, LANES)
            decs.append(jnp.broadcast_to(dcol, (LANES, LANES)).T)
        for j in range(n_sub):
            upd0 = _dot(kw_t[j], heads_out[0][2])
            upd1 = _dot(kw_t[j], heads_out[1][2])
            upd = jnp.where(_iota((LANES, GLA_V_DIM), 0) < GLA_K_DIM, upd0, upd1)
            s_list.append(decs[j] * s_list[j] + upd)
        st_ref[sl, :] = s_list[n_sub]
        for e in range(2):
            hh = 2 * p + e
            qm, o_h, _ = heads_out[e]
            inter = _dot(qm, s_list[0].astype(BF16))
            for j in range(1, n_sub):
                inter = jnp.where((rowl // CH) == j, _dot(qm, s_list[j].astype(BF16)), inter)
            o_h = (o_h + inter)[0:tb, :]
            normed.append(o_h * lax.rsqrt(jnp.mean(o_h * o_h, axis=-1, keepdims=True) + EPS))

    gain = jnp.concatenate([ng_ref[...]] * GLA_HEADS, axis=1)
    o_ref[...] = (jnp.concatenate(normed, axis=1) * gain * _silu(r_in)).astype(o_ref.dtype)

    @pl.when(c == pl.num_programs(1) - 1)
    def _():
        sT_ref[...] = st_ref[...]


GLA_COL0 = 3 * D_SB + D_SSM + SSM_CONV_DIM
GLA_SHIFT = SSM_HEADS
GLA_WIDE = 1024
GLA_N_WIDE = (2 * GLA_HEADS * GLA_K_DIM + 2 * D_GLA) // GLA_WIDE


def _gla(proj, row0, nb, t, s0, gk_w, gk_b, norm_g, out_dtype, buf=None, col0=0):
    tb = min(t, TILE)
    nc = t // tb
    rb0 = row0 // tb
    has_init = s0 is not None
    KP = GLA_HEADS * GLA_K_DIM
    assert GLA_COL0 % GLA_WIDE == 0 and (GLA_COL0 + GLA_N_WIDE * GLA_WIDE) % LANES == 0

    def colblk(width, idx):
        return pl.BlockSpec((tb, width), lambda b, c: (rb0 + b * nc + c, idx))

    def full(shape):
        return pl.BlockSpec(shape, lambda b, c: (0,) * len(shape))

    in_specs = [colblk(GLA_WIDE, GLA_COL0 // GLA_WIDE + i) for i in range(GLA_N_WIDE)]
    in_specs.append(colblk(LANES, (GLA_COL0 + GLA_N_WIDE * GLA_WIDE) // LANES))
    args = [proj] * (GLA_N_WIDE + 1)
    if has_init:
        in_specs.append(pl.BlockSpec((None, KP, GLA_V_DIM), lambda b, c: (b, 0, 0)))
        args.append(s0)
    in_specs += [full((LANES, KP)), full((1, KP)), full((1, GLA_V_DIM))]
    gw_rows = jnp.zeros((LANES, KP), F32).at[GLA_SHIFT:GLA_SHIFT + GLA_RANK].set(gk_w)
    args += [gw_rows, gk_b.reshape(1, KP), norm_g.reshape(1, GLA_V_DIM)]
    aliases = {}
    o_shape = jax.ShapeDtypeStruct((nb * t, D_GLA), out_dtype)
    if buf is not None:
        assert col0 % D_GLA == 0
        in_specs.append(pl.BlockSpec(memory_space=pl.ANY))
        args.append(buf)
        aliases = {len(args) - 1: 0}
        o_shape = jax.ShapeDtypeStruct(buf.shape, buf.dtype)
    cb0 = col0 // D_GLA
    return pl.pallas_call(
        functools.partial(_gla_kernel, tb=tb, has_init=has_init, has_buf=buf is not None),
        grid=(nb, nc),
        in_specs=in_specs,
        out_specs=[pl.BlockSpec((tb, D_GLA), lambda b, c: (b * nc + c, cb0)),
                   pl.BlockSpec((None, KP, GLA_V_DIM), lambda b, c: (b, 0, 0))],
        out_shape=[o_shape, jax.ShapeDtypeStruct((nb, KP, GLA_V_DIM), F32)],
        input_output_aliases=aliases,
        scratch_shapes=[pltpu.VMEM((KP, GLA_V_DIM), F32)],
        compiler_params=_cparams(("arbitrary", "arbitrary")),
    )(*args)


def _put_rows_kernel(*refs):
    parts, o_ref = refs[:-2], refs[-1]
    off = 0
    for p in parts:
        w = p.shape[1]
        o_ref[:, off:off + w] = p[...].astype(o_ref.dtype)
        off += w


def _put_rows(buf, parts, row0):
    rows = parts[0].shape[0]
    width = sum(p.shape[1] for p in parts)
    assert width == buf.shape[1] and row0 % rows == 0
    return pl.pallas_call(
        _put_rows_kernel,
        grid=(1,),
        in_specs=[pl.BlockSpec(p.shape, lambda i: (0, 0)) for p in parts] + [pl.BlockSpec(memory_space=pl.ANY)],
        out_specs=pl.BlockSpec((rows, width), lambda i: (row0 // rows, 0)),
        out_shape=jax.ShapeDtypeStruct(buf.shape, buf.dtype),
        input_output_aliases={len(parts): 0},
        compiler_params=_cparams(("arbitrary",)),
    )(*parts, buf)


def _head_rows_kernel(*refs, depth):
    ins, (ko_ref, vo_ref) = refs[:2 * depth], refs[2 * depth:]
    tr = ins[0].shape[0]
    for l in range(depth):
        @pl.when(pl.program_id(0) == l)
        def _(l=l):
            for src, dst in ((ins[2 * l], ko_ref), (ins[2 * l + 1], vo_ref)):
                for hh in range(SB_HEADS):
                    dst[pl.ds(hh, tr, stride=SB_HEADS), :] = src[:, hh * SB_HEAD_DIM:(hh + 1) * SB_HEAD_DIM]


def _head_rows(projs, rows, tr):
    depth = len(projs)
    nt = rows // tr
    in_specs, args = [], []
    for l, p in enumerate(projs):
        for cb in (1, 2):
            in_specs.append(pl.BlockSpec((tr, D_SB), lambda ll, i, l=l, cb=cb: (jnp.where(ll == l, i, 0), cb)))
            args.append(p)
    out_spec = pl.BlockSpec((tr * SB_HEADS, SB_HEAD_DIM), lambda ll, i: (ll * nt + i, 0))
    shape = jax.ShapeDtypeStruct((depth * rows * SB_HEADS, SB_HEAD_DIM), F32)
    return pl.pallas_call(
        functools.partial(_head_rows_kernel, depth=depth),
        grid=(depth, nt),
        in_specs=in_specs,
        out_specs=[out_spec, out_spec],
        out_shape=[shape, shape],
        compiler_params=_cparams(("arbitrary", "arbitrary")),
    )(*args)


def kernel(x_prompt, x_sample, cache_k, cache_v, page_table, state_conv, state_ssm, state_gla, norm1_g, w_in, sb_bias, conv_w, conv_b, dt_bias, a_log, d_skip, ssm_norm_g, gla_gk_w, gla_gk_b, gla_norm_g, w_out, norm2_g, w_gate, w_up, w_down, final_norm_g):
    bp, tp, d = x_prompt.shape
    bs, ts, _ = x_sample.shape
    depth = w_in.shape[0]
    mp = bp * tp
    ms = bs * ts
    m = mp + ms
    tm = _row_tile(m, 1376)
    tm_s = _row_tile(m, 688)
    d_ff = w_gate.shape[2]
    KP = GLA_HEADS * GLA_K_DIM
    off_xbc = 3 * D_SB + D_SSM
    n_tail = SSM_CONV - 1

    x = jnp.concatenate([x_prompt.reshape(mp, d), x_sample.reshape(ms, d)], axis=0)
    w_down_bf = _cast_bf16(w_down, _row_tile(d_ff, 688))
    w_in_nk = jnp.swapaxes(w_in, 1, 2)
    outs = {k: [] for k in ("cp", "sp", "gp", "ks", "vs", "cs", "ss", "gs")}
    projs = []

    for l in range(depth):
        xn = _rmsnorm(x, norm1_g[l], BF16, tm_s)
        proj = _matmul(xn, w_in_nk, l, None, tm, 512, w_is_nk=True)


        mixed = jnp.zeros((m, d), BF16)
        mixed = _sb_prompt(proj, sb_bias[l], bp, tp, mixed, D_SSM)
        ctx_p = jnp.zeros((bp, 8, SSM_CONV_DIM), F32)
        mixed, ssm_p = _ssd(proj, 0, bp, tp, ctx_p, None, conv_w[l], conv_b[l], dt_bias[l], a_log[l],
                            d_skip[l], ssm_norm_g[l], BF16, buf=mixed, col0=0)
        mixed, gla_p = _gla(proj, 0, bp, tp, None,
                            gla_gk_w[l], gla_gk_b[l], gla_norm_g[l], BF16, buf=mixed, col0=D_SSM + D_SB)

        q_s = proj[mp:, 0:D_SB].reshape(bs, ts, D_SB)
        k_s = proj[mp:, D_SB:2 * D_SB].reshape(bs, ts, D_SB)
        v_s = proj[mp:, 2 * D_SB:3 * D_SB].reshape(bs, ts, D_SB)
        o_sb_s = _sb_sample(q_s, k_s, v_s, sb_bias[l], cache_k, cache_v, page_table, l)
        ctx_s = jnp.pad(state_conv[l], ((0, 0), (8 - (SSM_CONV - 1), 0), (0, 0)))
        y_ssm_s, ssm_s = _ssd(proj, mp, bs, ts, ctx_s, state_ssm[l].reshape(bs, D_SSM, SSM_STATE),
                              conv_w[l], conv_b[l], dt_bias[l], a_log[l], d_skip[l], ssm_norm_g[l], F32)
        o_gl_s, gla_s = _gla(proj, mp, bs, ts, state_gla[l].reshape(bs, KP, GLA_V_DIM),
                             gla_gk_w[l], gla_gk_b[l], gla_norm_g[l], F32)

        mixed = _put_rows(mixed, [y_ssm_s, o_sb_s, o_gl_s], mp)
        x = _matmul(mixed, w_out, l, x, tm, 512,
                    k_order=((D_SB, D_SB + D_SSM), (0, D_SB), (D_SB + D_SSM, D_SB + D_SSM + D_GLA)))
        hn = _rmsnorm(x, norm2_g[l], BF16, tm_s)
        hmid = _gate_up(hn, w_gate, w_up, l, tm, 256)
        x = _matmul_bf16w(hmid, w_down_bf, l, x, tm_s, 512)

        xbc_p = jnp.stack([proj[(b + 1) * tp - n_tail:(b + 1) * tp, off_xbc:off_xbc + SSM_CONV_DIM]
                           for b in range(bp)])
        xbc_s = jnp.concatenate([state_conv[l], proj[mp:, off_xbc:off_xbc + SSM_CONV_DIM].reshape(bs, ts, -1)], axis=1)
        projs.append(proj)
        outs["cp"].append(xbc_p)
        outs["sp"].append(ssm_p.reshape(bp, SSM_HEADS, SSM_HEAD_DIM, SSM_STATE))
        outs["gp"].append(gla_p.reshape(bp, GLA_HEADS, GLA_K_DIM, GLA_V_DIM))
        outs["ks"].append(k_s.reshape(bs, ts, SB_HEADS, SB_HEAD_DIM))
        outs["vs"].append(v_s.reshape(bs, ts, SB_HEADS, SB_HEAD_DIM))
        outs["cs"].append(xbc_s[:, xbc_s.shape[1] - n_tail:])
        outs["ss"].append(ssm_s.reshape(bs, SSM_HEADS, SSM_HEAD_DIM, SSM_STATE))
        outs["gs"].append(gla_s.reshape(bs, GLA_HEADS, GLA_K_DIM, GLA_V_DIM))

    y_p = _rmsnorm(x, final_norm_g, F32, _row_tile(mp, 512), rows=mp)
    y_s = _rmsnorm(x[mp:], final_norm_g, F32, ms)
    st = {k: jnp.stack(v) for k, v in outs.items()}
    kp, vp = _head_rows(projs, mp, _row_tile(mp, 512))
    return (y_p.reshape(bp, tp, d), y_s.reshape(bs, ts, d),
            kp.reshape(depth, bp, tp, SB_HEADS, SB_HEAD_DIM), vp.reshape(depth, bp, tp, SB_HEADS, SB_HEAD_DIM),
            st["cp"], st["sp"], st["gp"],
            st["ks"], st["vs"], st["cs"], st["ss"], st["gs"])
```

```python
import functools

import jax
import jax.numpy as jnp
from jax import lax
from jax.experimental import pallas as pl
from jax.experimental.pallas import tpu as pltpu

F32 = jnp.float32
BF16 = jnp.bfloat16

EPS = 1e-6
SB_HEADS = 8
SB_HEAD_DIM = 128
D_SB = SB_HEADS * SB_HEAD_DIM
SSM_HEADS = 32
SSM_HEAD_DIM = 64
D_SSM = SSM_HEADS * SSM_HEAD_DIM
SSM_STATE = 128
SSM_GROUPS = 4
SSM_CONV = 4
SSM_CONV_DIM = D_SSM + 2 * SSM_GROUPS * SSM_STATE
GLA_HEADS = 8
GLA_K_DIM = 64
GLA_V_DIM = 128
D_GLA = GLA_HEADS * GLA_V_DIM
GLA_RANK = 16
GLA_TAU = 16.0
GLA_CHUNK = 64
PAGE_SIZE = 128

LANES = 128
TILE = 128
VMEM_LIMIT = 60 * 1024 * 1024
MM_ROWS = 1376
MM_ROWS_SMALL = 688
MM_COLS = 512
COPY_ROWS = 512


def _cparams(sem):
    return pltpu.CompilerParams(dimension_semantics=sem, vmem_limit_bytes=VMEM_LIMIT)


def _softplus(z):
    return jnp.maximum(z, 0.0) + jnp.log(1.0 + jnp.exp(-jnp.abs(z)))


def _silu(x):
    return x * (0.5 + 0.5 * jnp.tanh(0.5 * x))


def _split2(d):
    d0 = d.astype(BF16)
    d1 = (d - d0.astype(F32)).astype(BF16)
    return d0, d1


def _dot(a, b):
    return jnp.dot(a, b, preferred_element_type=F32)


def _dot_nt(a, b):
    return lax.dot_general(a, b, (((1,), (1,)), ((), ())), preferred_element_type=F32)


def _sel_dot_r(data, sel):
    d0, d1 = _split2(data)
    return _dot(d0, sel) + _dot(d1, sel)


def _sel_dot_l(sel, data):
    d0, d1 = _split2(data)
    return _dot(sel, d0) + _dot(sel, d1)


def _iota(shape, dim):
    return lax.broadcasted_iota(jnp.int32, shape, dim)


def _rmsnorm_kernel(x_ref, g_ref, o_ref):
    x = x_ref[...]
    y = x * lax.rsqrt(jnp.mean(x * x, axis=-1, keepdims=True) + EPS)
    o_ref[...] = (y * g_ref[...]).astype(o_ref.dtype)


def _row_tile(m, target):
    best = None
    for t in range(16, target + 1, 16):
        if m % t == 0:
            best = t
    assert best is not None, (m, target)
    return best


def _rmsnorm(x, g, out_dtype, tm, rows=None):
    m, d = x.shape
    m = m if rows is None else rows
    return pl.pallas_call(
        _rmsnorm_kernel,
        grid=(m // tm,),
        in_specs=[pl.BlockSpec((tm, d), lambda i: (i, 0)), pl.BlockSpec((1, d), lambda i: (0, 0))],
        out_specs=pl.BlockSpec((tm, d), lambda i: (i, 0)),
        out_shape=jax.ShapeDtypeStruct((m, d), out_dtype),
        compiler_params=_cparams(("arbitrary",)),
    )(x, g.reshape(1, d))


def _mm_kernel(x_ref, w_ref, *rest, has_res, w_is_nk, k_order):
    if has_res:
        res_ref, o_ref, wbf_ref = rest
    else:
        o_ref, wbf_ref = rest

    @pl.when(pl.program_id(1) == 0)
    def _():
        if k_order is None:
            wbf_ref[...] = w_ref[...].astype(BF16)
        else:
            off = 0
            for a, b in k_order:
                wbf_ref[off:off + b - a, :] = w_ref[a:b, :].astype(BF16)
                off += b - a

    acc = _dot_nt(x_ref[...], wbf_ref[...]) if w_is_nk else _dot(x_ref[...], wbf_ref[...])
    if has_res:
        acc = acc + res_ref[...]
    o_ref[...] = acc.astype(o_ref.dtype)


def _matmul(x, w, layer, res, tm, tn, w_is_nk=False, k_order=None):
    m, k = x.shape
    assert k_order is None or (not w_is_nk and sum(b - a for a, b in k_order) == k)
    if w_is_nk:
        n = w.shape[1]
        w_spec = pl.BlockSpec((None, tn, k), lambda j, i: (layer, j, 0))
        w_scratch = pltpu.VMEM((tn, k), BF16)
    else:
        n = w.shape[2]
        w_spec = pl.BlockSpec((None, k, tn), lambda j, i: (layer, 0, j))
        w_scratch = pltpu.VMEM((k, tn), BF16)
    in_specs = [pl.BlockSpec((tm, k), lambda j, i: (i, 0)), w_spec]
    args = [x, w]
    if res is not None:
        in_specs.append(pl.BlockSpec((tm, tn), lambda j, i: (i, j)))
        args.append(res)
    return pl.pallas_call(
        functools.partial(_mm_kernel, has_res=res is not None, w_is_nk=w_is_nk, k_order=k_order),
        grid=(pl.cdiv(n, tn), m // tm),
        in_specs=in_specs,
        out_specs=pl.BlockSpec((tm, tn), lambda j, i: (i, j)),
        out_shape=jax.ShapeDtypeStruct((m, n), F32),
        scratch_shapes=[w_scratch],
        compiler_params=_cparams(("arbitrary", "arbitrary")),
    )(*args)


MM_SLABS = 4


def _slabs(rows, n):
    step = -(-rows // (16 * n)) * 16
    bounds = [min(i * step, rows) for i in range(n + 1)]
    return [(a, b) for a, b in zip(bounds[:-1], bounds[1:]) if b > a]


def _gate_up_kernel(x_ref, wg_ref, wu_ref, o_ref, wbf_ref, *, tn):
    @pl.when(pl.program_id(1) == 0)
    def _():
        wbf_ref[:, :tn] = wg_ref[...].astype(BF16)
        wbf_ref[:, tn:] = wu_ref[...].astype(BF16)

    for r0, r1 in _slabs(o_ref.shape[0], MM_SLABS):
        gu = _dot(x_ref[r0:r1, :], wbf_ref[...])
        o_ref[r0:r1, :] = (_silu(gu[:, :tn]) * gu[:, tn:]).astype(o_ref.dtype)


def _gate_up(x, wg, wu, layer, tm, tn):
    m, k = x.shape
    f = wg.shape[2]
    wspec = pl.BlockSpec((None, k, tn), lambda j, i: (layer, 0, j))
    return pl.pallas_call(
        functools.partial(_gate_up_kernel, tn=tn),
        grid=(f // tn, m // tm),
        in_specs=[pl.BlockSpec((tm, k), lambda j, i: (i, 0)), wspec, wspec],
        out_specs=pl.BlockSpec((tm, tn), lambda j, i: (i, j)),
        out_shape=jax.ShapeDtypeStruct((m, f), BF16),
        scratch_shapes=[pltpu.VMEM((k, 2 * tn), BF16)],
        compiler_params=_cparams(("arbitrary", "arbitrary")),
    )(x, wg, wu)


def _cast_kernel(w_ref, o_ref):
    o_ref[...] = w_ref[...].astype(o_ref.dtype)


def _cast_bf16(w, tr):
    nl, r, c = w.shape
    spec = pl.BlockSpec((None, tr, c), lambda l, i: (l, i, 0))
    return pl.pallas_call(
        _cast_kernel,
        grid=(nl, r // tr),
        in_specs=[spec],
        out_specs=spec,
        out_shape=jax.ShapeDtypeStruct(w.shape, BF16),
        compiler_params=_cparams(("arbitrary", "arbitrary")),
    )(w)


def _mm_res_kernel(x_ref, w_ref, res_ref, o_ref):
    o_ref[...] = _dot(x_ref[...], w_ref[...]) + res_ref[...]


def _matmul_bf16w(x, w, layer, res, tm, tn):
    m, k = x.shape
    n = w.shape[2]
    return pl.pallas_call(
        _mm_res_kernel,
        grid=(m // tm, n // tn),
        in_specs=[pl.BlockSpec((tm, k), lambda i, j: (i, 0)),
                  pl.BlockSpec((None, k, tn), lambda i, j: (layer, 0, j)),
                  pl.BlockSpec((tm, tn), lambda i, j: (i, j))],
        out_specs=pl.BlockSpec((tm, tn), lambda i, j: (i, j)),
        out_shape=jax.ShapeDtypeStruct((m, n), F32),
        compiler_params=_cparams(("arbitrary", "arbitrary")),
    )(x, w, res)


SB_KEYS = 256
SB_UNROLL = 2

def _sb_prompt_kernel(bias_ref, q_ref, k_ref, v_ref, buf_ref, o_ref, acc_ref, run_ref, *, tq, scale):
    del buf_ref
    h = pl.program_id(1)
    i = pl.program_id(2)
    nk = SB_KEYS
    bias = bias_ref[h]
    q = (q_ref[...] * scale).astype(BF16)
    later = _iota((nk, nk), 0) > _iota((nk, nk), 1)
    suffix_total = jnp.concatenate([later.astype(BF16), jnp.ones((nk, LANES), BF16)], axis=1)

    def block(qv, start, run, mask):
        kb = k_ref[pl.ds(start, nk), :].astype(BF16)
        vb = v_ref[pl.ds(start, nk), :].astype(BF16)
        z = _dot_nt(qv, kb) + bias
        sp = _softplus(z)
        lg = -sp if mask is None else jnp.where(mask, -sp, 0.0)
        st = _dot(lg.astype(BF16), suffix_total)
        e = jnp.exp(z - sp + st[:, :nk] + jnp.concatenate([run] * (nk // LANES), axis=1))
        w = e if mask is None else jnp.where(mask, e, 0.0)
        return _dot(w.astype(BF16), vb), run + st[:, nk:]

    acc_ref[...] = jnp.zeros_like(acc_ref)
    run_ref[...] = jnp.zeros_like(run_ref)
    q0 = pl.multiple_of(i * tq, tq)
    for jd in reversed(range(tq // nk)):
        r0 = jd * nk
        rows = tq - r0
        mask = _iota((rows, nk), 1) < _iota((rows, nk), 0)
        pv, run = block(q[r0:], q0 + r0, run_ref[r0:, :], mask)
        acc_ref[r0:, :] += pv
        run_ref[r0:, :] = run

    def body(g, carry):
        run = run_ref[...]
        total = None
        for u in range(SB_UNROLL):
            start = pl.multiple_of(q0 - (g * SB_UNROLL + u + 1) * nk, nk)
            pv, run = block(q, start, run, None)
            total = pv if total is None else total + pv
        acc_ref[...] += total
        run_ref[...] = run
        return carry

    lax.fori_loop(0, (i * tq) // (SB_UNROLL * nk), body, 0)
    o_ref[...] = acc_ref[...].astype(o_ref.dtype)


def _sb_prompt(proj, bias, nb, t, buf, col0, tq=1024):
    assert t % tq == 0 and tq % (SB_UNROLL * SB_KEYS) == 0 and col0 % SB_HEAD_DIM == 0
    nq = t // tq
    hb = SB_HEADS
    cb0 = col0 // SB_HEAD_DIM
    return pl.pallas_call(
        functools.partial(_sb_prompt_kernel, tq=tq, scale=SB_HEAD_DIM ** -0.5),
        grid=(nb, hb, nq),
        in_specs=[pl.BlockSpec(memory_space=pltpu.SMEM),
                  pl.BlockSpec((tq, SB_HEAD_DIM), lambda b, h, i: (b * nq + i, h)),
                  pl.BlockSpec((t, SB_HEAD_DIM), lambda b, h, i: (b, hb + h)),
                  pl.BlockSpec((t, SB_HEAD_DIM), lambda b, h, i: (b, 2 * hb + h)),
                  pl.BlockSpec(memory_space=pl.ANY)],
        out_specs=pl.BlockSpec((tq, SB_HEAD_DIM), lambda b, h, i: (b * nq + i, cb0 + h)),
        out_shape=jax.ShapeDtypeStruct(buf.shape, buf.dtype),
        input_output_aliases={4: 0},
        scratch_shapes=[pltpu.VMEM((tq, SB_HEAD_DIM), F32), pltpu.VMEM((tq, LANES), F32)],
        compiler_params=_cparams(("arbitrary", "arbitrary", "arbitrary")),
    )(bias, proj, proj, proj, buf)


def _sb_sample_kernel(pt_ref, qbd_ref, bias_ref, kown_ref, vown_ref, *refs, pp, nq):
    del pt_ref
    kp = refs[:pp]
    vp = refs[pp:2 * pp]
    o_ref = refs[2 * pp]
    acc_ref, run_ref = refs[2 * pp + 1:]
    s = pl.program_id(1)
    n = PAGE_SIZE
    rows_c = 2 * nq
    earlier = (_iota((n, n), 1) > _iota((n, n), 0)).astype(BF16)
    qbd = qbd_ref[...]
    bias = bias_ref[...]

    def head_rows(ref, hh):
        return ref[pl.ds(hh, n, stride=SB_HEADS), :].astype(BF16)

    def scores(k_ref, mask):
        kcat = jnp.concatenate([head_rows(k_ref, hh) for hh in range(SB_HEADS)], axis=1)
        z = _dot(kcat, qbd) + bias
        sp = _softplus(z)
        lg = -sp if mask is None else jnp.where(mask, -sp, 0.0)
        return z - sp, _dot(earlier, lg.astype(BF16)), jnp.sum(lg, axis=0, keepdims=True)

    def add_values(v_ref, w, acc):
        wt = w.T.astype(BF16)
        return [acc[hh] + _dot(wt[(hh // 2) * rows_c:(hh // 2 + 1) * rows_c, :], head_rows(v_ref, hh))
                for hh in range(SB_HEADS)]

    @pl.when(s == 0)
    def _():
        mask = _iota((n, LANES), 0) < (_iota((n, LANES), 1) % nq)
        log_beta, local, total = scores(kown_ref, mask)
        w = jnp.where(mask, jnp.exp(log_beta + local), 0.0)
        acc = add_values(vown_ref, w, [jnp.zeros((rows_c, SB_HEAD_DIM), F32)] * SB_HEADS)
        for hh in range(SB_HEADS):
            acc_ref[hh] = acc[hh]
        run_ref[...] = total

    kcat = jnp.concatenate(
        [jnp.concatenate([head_rows(kp[c], hh) for hh in range(SB_HEADS)], axis=1) for c in range(pp)], axis=0)
    z = _dot(kcat, qbd) + bias
    sp = _softplus(z)
    lg = -sp
    lgb = lg.astype(BF16)
    local = jnp.concatenate([_dot(earlier, lgb[c * n:(c + 1) * n]) for c in range(pp)], axis=0)
    run = run_ref[...]
    runs = []
    for c in range(pp):
        runs.append(jnp.broadcast_to(run, (n, LANES)))
        run = run + jnp.sum(lg[c * n:(c + 1) * n], axis=0, keepdims=True)
    run_ref[...] = run
    w = jnp.exp(z - sp + local + jnp.concatenate(runs, axis=0))
    wt = w.T.astype(BF16)
    for hh in range(SB_HEADS):
        vh = jnp.concatenate([head_rows(vp[c], hh) for c in range(pp)], axis=0)
        acc_ref[hh] += _dot(wt[(hh // 2) * rows_c:(hh // 2 + 1) * rows_c, :], vh)

    @pl.when(s == pl.num_programs(1) - 1)
    def _():
        for hh in range(SB_HEADS):
            e = hh % 2
            o_ref[:, hh * SB_HEAD_DIM:(hh + 1) * SB_HEAD_DIM] = acc_ref[hh, e * nq:(e + 1) * nq, :]


def _sb_sample(q, k_own, v_own, bias, cache_k, cache_v, page_table, layer, pp=16):
    db, nq, _ = q.shape
    assert 2 * nq == 16 and SB_HEADS * nq <= LANES and cache_k.shape[2:] == (PAGE_SIZE, SB_HEADS, SB_HEAD_DIM)
    depth, n_pool = cache_k.shape[0], cache_k.shape[1]
    n_pages = page_table.shape[1]
    scale = SB_HEAD_DIM ** -0.5
    qh = (q * scale).reshape(db, nq, SB_HEADS, SB_HEAD_DIM)
    eye = jnp.eye(SB_HEADS, dtype=F32)
    qbd = jnp.einsum('bthd,hg->bhdgt', qh, eye).reshape(db, D_SB, SB_HEADS * nq)
    qbd = jnp.pad(qbd, ((0, 0), (0, 0), (0, LANES - SB_HEADS * nq))).astype(BF16)
    bias_l = jnp.pad(jnp.repeat(bias.astype(F32), nq), (0, LANES - SB_HEADS * nq)).reshape(1, LANES)
    pad = ((0, 0), (0, PAGE_SIZE - nq), (0, 0))
    page_rows = PAGE_SIZE * SB_HEADS
    kown = jnp.pad(k_own, pad).reshape(db * page_rows, SB_HEAD_DIM)
    vown = jnp.pad(v_own, pad).reshape(db * page_rows, SB_HEAD_DIM)
    ck = cache_k.reshape(depth * n_pool * page_rows, SB_HEAD_DIM)
    cv = cache_v.reshape(depth * n_pool * page_rows, SB_HEAD_DIM)
    base = layer * n_pool

    def page_map(c):
        return lambda b, s, pt: (base + pt[b, n_pages - 1 - (s * pp + c)], 0)

    page_spec = [pl.BlockSpec((page_rows, SB_HEAD_DIM), page_map(c)) for c in range(pp)]
    own_spec = pl.BlockSpec((page_rows, SB_HEAD_DIM), lambda b, s, pt: (b, 0))
    grid_spec = pltpu.PrefetchScalarGridSpec(
        num_scalar_prefetch=1,
        grid=(db, n_pages // pp),
        in_specs=[pl.BlockSpec((None, D_SB, LANES), lambda b, s, pt: (b, 0, 0)),
                  pl.BlockSpec((1, LANES), lambda b, s, pt: (0, 0)),
                  own_spec, own_spec] + page_spec + page_spec,
        out_specs=pl.BlockSpec((nq, D_SB), lambda b, s, pt: (b, 0)),
        scratch_shapes=[pltpu.VMEM((SB_HEADS, 2 * nq, SB_HEAD_DIM), F32),
                        pltpu.VMEM((1, LANES), F32)],
    )
    return pl.pallas_call(
        functools.partial(_sb_sample_kernel, pp=pp, nq=nq),
        grid_spec=grid_spec,
        out_shape=jax.ShapeDtypeStruct((db * nq, D_SB), F32),
        compiler_params=_cparams(("arbitrary", "arbitrary")),
    )(page_table, qbd, bias_l, kown, vown, *([ck] * pp), *([cv] * pp))


def _ssd_kernel(*refs, tb, has_init, has_buf):
    if has_buf:
        refs = refs[:-6] + refs[-5:]
    if has_init:
        (z0_ref, z1_ref, xa_ref, xb_ref, bc_ref, dt_ref, ctx_ref, h0_ref, cw_ref, cb_ref, dtb_ref, alog_ref,
         dsk_ref, ng_ref, spr_ref, sprt_ref, y_ref, hT_ref, xp_ref, hs_ref, yb_ref) = refs
    else:
        (z0_ref, z1_ref, xa_ref, xb_ref, bc_ref, dt_ref, ctx_ref, cw_ref, cb_ref, dtb_ref, alog_ref,
         dsk_ref, ng_ref, spr_ref, sprt_ref, y_ref, hT_ref, xp_ref, hs_ref, yb_ref) = refs
        h0_ref = None
    c = pl.program_id(1)
    L = TILE
    P2 = LANES
    n_pairs = D_SSM // P2

    @pl.when(c == 0)
    def _():
        xp_ref[0:8, :] = ctx_ref[...]
        if has_init:
            hs_ref[...] = h0_ref[...]
        else:
            hs_ref[...] = jnp.zeros_like(hs_ref)

    xp_ref[8:8 + tb, 0:1024] = xa_ref[...]
    xp_ref[8:8 + tb, 1024:2048] = xb_ref[...]
    xp_ref[8:8 + tb, 2048:3072] = bc_ref[...]
    if tb < L:
        xp_ref[8 + tb:8 + L, :] = jnp.zeros((L - tb, SSM_CONV_DIM), F32)

    conv = cb_ref[...] + cw_ref[0:1, :] * xp_ref[5:5 + L, :]
    for i in range(1, SSM_CONV):
        conv = conv + cw_ref[i:i + 1, :] * xp_ref[5 + i:5 + i + L, :]
    xp_ref[0:8, :] = xp_ref[tb:tb + 8, :]
    xc = _silu(conv)

    row = _iota((L, LANES), 0)
    lane = _iota((L, LANES), 1)
    dt_raw = dt_ref[...]
    if tb < L:
        dt_raw = jnp.concatenate([dt_raw, jnp.zeros((L - tb, LANES), F32)], axis=0)
    valid = (lane < SSM_HEADS) & (row < tb)
    dtv = jnp.where(valid, _softplus(dt_raw + dtb_ref[...]), 0.0)
    a = dtv * (-jnp.exp(alog_ref[...]))
    lower = (_iota((L, L), 0) >= _iota((L, L), 1))
    lower_b = lower.astype(BF16)
    upper_b = (_iota((L, L), 0) <= _iota((L, L), 1)).astype(BF16)
    cum = _sel_dot_l(lower_b, a)
    cum_t = _sel_dot_r(a.T, upper_b)
    last = cum[L - 1:L, :]
    spread = spr_ref[...]
    spread_t = sprt_ref[...]
    dt_x = _sel_dot_r(dtv, spread)
    ecum_x = _sel_dot_r(jnp.exp(cum), spread)
    wdec_x = _sel_dot_r(jnp.exp(last - cum), spread)
    sdec = _sel_dot_l(spread_t, jnp.broadcast_to(jnp.exp(cum_t[:, L - 1:L]), (LANES, LANES)))

    first_half = lane < SSM_HEAD_DIM
    pairs_per_group = n_pairs // SSM_GROUPS
    x_all = xc[:, :D_SSM]
    xs_all = x_all * dt_x
    xs_bf = xs_all.astype(BF16)
    xw_all = xs_all * wdec_x
    h_old = hs_ref[...]
    h_old_b = h_old.astype(BF16)
    gcols = D_SSM // SSM_GROUPS
    for g in range(SSM_GROUPS):
        bg = xc[:, D_SSM + g * SSM_STATE:D_SSM + (g + 1) * SSM_STATE]
        cg = xc[:, D_SSM + (SSM_GROUPS + g) * SSM_STATE:D_SSM + (SSM_GROUPS + g + 1) * SSM_STATE]
        bg_b = bg.astype(BF16)
        cg_b = cg.astype(BF16)
        cb = _dot_nt(cg_b, bg_b)
        gs = slice(g * gcols, (g + 1) * gcols)
        inter = _dot_nt(cg_b, h_old_b[gs, :]) * ecum_x[:, gs]
        hs_ref[gs, :] = sdec[gs, :] * h_old[gs, :] + _dot(xw_all[:, gs].T.astype(BF16), bg_b)
        for r in range(pairs_per_group):
            p = g * pairs_per_group + r
            sl = slice(p * P2, (p + 1) * P2)
            xs_b = xs_bf[:, sl]
            ys = []
            for e in range(2):
                hh = 2 * p + e
                seg = jnp.broadcast_to(cum[:, hh:hh + 1], (L, L)) - jnp.broadcast_to(cum_t[hh:hh + 1, :], (L, L))
                dec = jnp.where(lower, jnp.exp(jnp.minimum(seg, 0.0)), 0.0)
                ys.append(_dot((cb * dec).astype(BF16), xs_b))
            yb_ref[:, sl] = jnp.where(first_half, ys[0], ys[1]) + inter[:, r * P2:(r + 1) * P2]

    zfull = jnp.concatenate([z0_ref[...], z1_ref[...]], axis=1)
    y = (yb_ref[0:tb, :] + dsk_ref[...] * x_all[0:tb, :]) * _silu(zfull)
    gw = D_SSM // SSM_GROUPS
    for g in range(SSM_GROUPS):
        yg = y[:, g * gw:(g + 1) * gw]
        yn = yg * lax.rsqrt(jnp.mean(yg * yg, axis=-1, keepdims=True) + EPS)
        y_ref[:, g * gw:(g + 1) * gw] = (yn * ng_ref[:, g * gw:(g + 1) * gw]).astype(y_ref.dtype)

    @pl.when(c == pl.num_programs(1) - 1)
    def _():
        hT_ref[...] = hs_ref[...]


def _ssd(proj, row0, nb, t, ctx8, h0, conv_w, conv_b, dt_bias, a_log, d_skip, norm_g, out_dtype, buf=None, col0=0):
    tb = min(t, TILE)
    nc = t // tb
    rb0 = row0 // tb
    has_init = h0 is not None

    def colblk(width, idx):
        return pl.BlockSpec((tb, width), lambda b, c: (rb0 + b * nc + c, idx))

    def full(shape):
        return pl.BlockSpec(shape, lambda b, c: (0,) * len(shape))

    in_specs = [colblk(1024, 3), colblk(1024, 4), colblk(1024, 5), colblk(1024, 6), colblk(1024, 7),
                colblk(LANES, 64),
                pl.BlockSpec((None, 8, SSM_CONV_DIM), lambda b, c: (b, 0, 0))]
    args = [proj] * 6 + [ctx8]
    if has_init:
        in_specs.append(pl.BlockSpec((None, D_SSM, SSM_STATE), lambda b, c: (b, 0, 0)))
        args.append(h0)
    padl = (0, LANES - SSM_HEADS)
    in_specs += [full((SSM_CONV, SSM_CONV_DIM)), full((1, SSM_CONV_DIM)), full((1, LANES)), full((1, LANES)),
                 full((1, D_SSM)), full((1, D_SSM)), full((LANES, D_SSM)), full((D_SSM, LANES))]
    spread = (jnp.arange(LANES)[:, None] == jnp.arange(D_SSM)[None, :] // SSM_HEAD_DIM).astype(BF16)
    args += [conv_w, conv_b.reshape(1, -1), jnp.pad(dt_bias, padl).reshape(1, LANES),
             jnp.pad(a_log, padl).reshape(1, LANES), jnp.repeat(d_skip, SSM_HEAD_DIM).reshape(1, D_SSM),
             norm_g.reshape(1, D_SSM), spread, spread.T]
    aliases = {}
    y_shape = jax.ShapeDtypeStruct((nb * t, D_SSM), out_dtype)
    if buf is not None:
        assert col0 % D_SSM == 0 and row0 == 0
        in_specs.append(pl.BlockSpec(memory_space=pl.ANY))
        args.append(buf)
        aliases = {len(args) - 1: 0}
        y_shape = jax.ShapeDtypeStruct(buf.shape, buf.dtype)
    cb0 = col0 // D_SSM
    return pl.pallas_call(
        functools.partial(_ssd_kernel, tb=tb, has_init=has_init, has_buf=buf is not None),
        grid=(nb, nc),
        in_specs=in_specs,
        out_specs=[pl.BlockSpec((tb, D_SSM), lambda b, c: (b * nc + c, cb0)),
                   pl.BlockSpec((None, D_SSM, SSM_STATE), lambda b, c: (b, 0, 0))],
        out_shape=[y_shape, jax.ShapeDtypeStruct((nb, D_SSM, SSM_STATE), F32)],
        input_output_aliases=aliases,
        scratch_shapes=[pltpu.VMEM((8 + TILE, SSM_CONV_DIM), F32),
                        pltpu.VMEM((D_SSM, SSM_STATE), F32),
                        pltpu.VMEM((TILE, D_SSM), F32)],
        compiler_params=_cparams(("arbitrary", "arbitrary")),
    )(*args)


def _gla_kernel(*refs, tb, has_init, has_buf):
    if has_buf:
        refs = refs[:-4] + refs[-3:]
    wide, tail_ref, refs = refs[:GLA_N_WIDE], refs[GLA_N_WIDE], refs[GLA_N_WIDE + 1:]
    if has_init:
        (s0_ref, gw_ref, gb_ref, ng_ref, o_ref, sT_ref, st_ref) = refs
    else:
        (gw_ref, gb_ref, ng_ref, o_ref, sT_ref, st_ref) = refs
        s0_ref = None
    c = pl.program_id(1)
    L = TILE
    tail = tail_ref[...]
    region = jnp.concatenate([w[...] for w in wide] + [tail], axis=1)
    region = pltpu.roll(region, region.shape[1] - GLA_SHIFT, 1)
    kp_, o0 = GLA_HEADS * GLA_K_DIM, 0
    q_in = region[:, o0:o0 + kp_]
    k_in = region[:, o0 + kp_:o0 + 2 * kp_]
    v_in = region[:, o0 + 2 * kp_:o0 + 2 * kp_ + D_GLA]
    r_in = region[:, o0 + 2 * kp_ + D_GLA:o0 + 2 * kp_ + 2 * D_GLA]
    tail_lane = _iota(tail.shape, 1)
    gate_in = jnp.where((tail_lane >= GLA_SHIFT) & (tail_lane < GLA_SHIFT + GLA_RANK), tail, 0.0)
    CH = min(GLA_CHUNK, tb)
    n_sub = max(tb // CH, 1)
    KP = GLA_HEADS * GLA_K_DIM

    @pl.when(c == 0)
    def _():
        if has_init:
            st_ref[...] = s0_ref[...]
        else:
            st_ref[...] = jnp.zeros_like(st_ref)

    def padrows(x):
        if tb < L:
            return jnp.concatenate([x, jnp.zeros((L - tb, x.shape[1]), F32)], axis=0)
        return x

    row = _iota((L, L), 0)
    col = _iota((L, L), 1)
    same_chunk = (row // CH) == (col // CH)
    lower = (row >= col) & same_chunk
    lower_b = lower.astype(BF16)
    rvalid = _iota((L, KP), 0) < tb
    glog = _dot(gate_in.astype(BF16), gw_ref[...].astype(BF16)) + gb_ref[...]
    glog = padrows(-_softplus(-glog) * (1.0 / GLA_TAU))
    glog = jnp.where(rvalid, glog, 0.0)
    q_all = padrows(q_in) * (GLA_K_DIM ** -0.5)
    k_all = jnp.where(rvalid, padrows(k_in), 0.0)
    v_all = padrows(v_in)
    lane = _iota((L, LANES), 1)
    rowl = _iota((L, LANES), 0)

    b_all = _sel_dot_l(lower_b, glog)
    qt_all = q_all * jnp.exp(b_all)
    kt_all = (k_all * jnp.exp(-b_all)).astype(BF16)
    lasts_all = [b_all[min((j + 1) * CH, L) - 1:min((j + 1) * CH, L), :] for j in range(n_sub)]
    chunk_of_row = _iota((L, KP), 0) // CH
    kw_all = [jnp.where(chunk_of_row == j, k_all * jnp.exp(lasts_all[j] - b_all), 0.0) for j in range(n_sub)]
    normed = []

    for p in range(KP // LANES):
        sl = slice(p * LANES, (p + 1) * LANES)
        qt = qt_all[:, sl]
        kt = kt_all[:, sl]
        lasts = [la[:, sl] for la in lasts_all]
        s_cur = st_ref[sl, :]
        s_list = [s_cur]
        kw_t = [kw[:, sl].T.astype(BF16) for kw in kw_all]
        heads_out = []
        for e in range(2):
            hh = 2 * p + e
            own = (lane // GLA_K_DIM) == e
            qm = jnp.where(own, qt, 0.0).astype(BF16)
            att = jnp.where(lower, _dot_nt(qm, kt), 0.0)
            v_h = v_all[:, hh * GLA_V_DIM:(hh + 1) * GLA_V_DIM].astype(BF16)
            heads_out.append((qm, _dot(att.astype(BF16), v_h), v_h))
        decs = []
        for j in range(n_sub):
            dcol = jnp.exp(lasts[j]).reshape(1, LANES)
            decs.append(jnp.broadcast_to(dcol, (LANES, LANES)).T)
        for j in range(n_sub):
            upd0 = _dot(kw_t[j], heads_out[0][2])
            upd1 = _dot(kw_t[j], heads_out[1][2])
            upd = jnp.where(_iota((LANES, GLA_V_DIM), 0) < GLA_K_DIM, upd0, upd1)
            s_list.append(decs[j] * s_list[j] + upd)
        st_ref[sl, :] = s_list[n_sub]
        for e in range(2):
            hh = 2 * p + e
            qm, o_h, _ = heads_out[e]
            inter = _dot(qm, s_list[0].astype(BF16))
            for j in range(1, n_sub):
                inter = jnp.where((rowl // CH) == j, _dot(qm, s_list[j].astype(BF16)), inter)
            o_h = (o_h + inter)[0:tb, :]
            normed.append(o_h * lax.rsqrt(jnp.mean(o_h * o_h, axis=-1, keepdims=True) + EPS))

    gain = jnp.concatenate([ng_ref[...]] * GLA_HEADS, axis=1)
    o_ref[...] = (jnp.concatenate(normed, axis=1) * gain * _silu(r_in)).astype(o_ref.dtype)

    @pl.when(c == pl.num_programs(1) - 1)
    def _():
        sT_ref[...] = st_ref[...]


GLA_COL0 = 3 * D_SB + D_SSM + SSM_CONV_DIM
GLA_SHIFT = SSM_HEADS
GLA_WIDE = 1024
GLA_N_WIDE = (2 * GLA_HEADS * GLA_K_DIM + 2 * D_GLA) // GLA_WIDE


def _gla(proj, row0, nb, t, s0, gk_w, gk_b, norm_g, out_dtype, buf=None, col0=0):
    tb = min(t, TILE)
    nc = t // tb
    rb0 = row0 // tb
    has_init = s0 is not None
    KP = GLA_HEADS * GLA_K_DIM
    assert GLA_COL0 % GLA_WIDE == 0 and (GLA_COL0 + GLA_N_WIDE * GLA_WIDE) % LANES == 0

    def colblk(width, idx):
        return pl.BlockSpec((tb, width), lambda b, c: (rb0 + b * nc + c, idx))

    def full(shape):
        return pl.BlockSpec(shape, lambda b, c: (0,) * len(shape))

    in_specs = [colblk(GLA_WIDE, GLA_COL0 // GLA_WIDE + i) for i in range(GLA_N_WIDE)]
    in_specs.append(colblk(LANES, (GLA_COL0 + GLA_N_WIDE * GLA_WIDE) // LANES))
    args = [proj] * (GLA_N_WIDE + 1)
    if has_init:
        in_specs.append(pl.BlockSpec((None, KP, GLA_V_DIM), lambda b, c: (b, 0, 0)))
        args.append(s0)
    in_specs += [full((LANES, KP)), full((1, KP)), full((1, GLA_V_DIM))]
    gw_rows = jnp.zeros((LANES, KP), F32).at[GLA_SHIFT:GLA_SHIFT + GLA_RANK].set(gk_w)
    args += [gw_rows, gk_b.reshape(1, KP), norm_g.reshape(1, GLA_V_DIM)]
    aliases = {}
    o_shape = jax.ShapeDtypeStruct((nb * t, D_GLA), out_dtype)
    if buf is not None:
        assert col0 % D_GLA == 0
        in_specs.append(pl.BlockSpec(memory_space=pl.ANY))
        args.append(buf)
        aliases = {len(args) - 1: 0}
        o_shape = jax.ShapeDtypeStruct(buf.shape, buf.dtype)
    cb0 = col0 // D_GLA
    return pl.pallas_call(
        functools.partial(_gla_kernel, tb=tb, has_init=has_init, has_buf=buf is not None),
        grid=(nb, nc),
        in_specs=in_specs,
        out_specs=[pl.BlockSpec((tb, D_GLA), lambda b, c: (b * nc + c, cb0)),
                   pl.BlockSpec((None, KP, GLA_V_DIM), lambda b, c: (b, 0, 0))],
        out_shape=[o_shape, jax.ShapeDtypeStruct((nb, KP, GLA_V_DIM), F32)],
        input_output_aliases=aliases,
        scratch_shapes=[pltpu.VMEM((KP, GLA_V_DIM), F32)],
        compiler_params=_cparams(("arbitrary", "arbitrary")),
    )(*args)


def _put_rows_kernel(*refs):
    parts, o_ref = refs[:-2], refs[-1]
    off = 0
    for p in parts:
        w = p.shape[1]
        o_ref[:, off:off + w] = p[...].astype(o_ref.dtype)
        off += w


def _put_rows(buf, parts, row0):
    rows = parts[0].shape[0]
    width = sum(p.shape[1] for p in parts)
    assert width == buf.shape[1] and row0 % rows == 0
    return pl.pallas_call(
        _put_rows_kernel,
        grid=(1,),
        in_specs=[pl.BlockSpec(p.shape, lambda i: (0, 0)) for p in parts] + [pl.BlockSpec(memory_space=pl.ANY)],
        out_specs=pl.BlockSpec((rows, width), lambda i: (row0 // rows, 0)),
        out_shape=jax.ShapeDtypeStruct(buf.shape, buf.dtype),
        input_output_aliases={len(parts): 0},
        compiler_params=_cparams(("arbitrary",)),
    )(*parts, buf)


def _head_rows_kernel(*refs, depth):
    ins, (ko_ref, vo_ref) = refs[:2 * depth], refs[2 * depth:]
    tr = ins[0].shape[0]
    for l in range(depth):
        @pl.when(pl.program_id(0) == l)
        def _(l=l):
            for src, dst in ((ins[2 * l], ko_ref), (ins[2 * l + 1], vo_ref)):
                for hh in range(SB_HEADS):
                    dst[pl.ds(hh, tr, stride=SB_HEADS), :] = src[:, hh * SB_HEAD_DIM:(hh + 1) * SB_HEAD_DIM]


def _head_rows(projs, rows, tr):
    depth = len(projs)
    nt = rows // tr
    in_specs, args = [], []
    for l, p in enumerate(projs):
        for cb in (1, 2):
            in_specs.append(pl.BlockSpec((tr, D_SB), lambda ll, i, l=l, cb=cb: (jnp.where(ll == l, i, 0), cb)))
            args.append(p)
    out_spec = pl.BlockSpec((tr * SB_HEADS, SB_HEAD_DIM), lambda ll, i: (ll * nt + i, 0))
    shape = jax.ShapeDtypeStruct((depth * rows * SB_HEADS, SB_HEAD_DIM), F32)
    return pl.pallas_call(
        functools.partial(_head_rows_kernel, depth=depth),
        grid=(depth, nt),
        in_specs=in_specs,
        out_specs=[out_spec, out_spec],
        out_shape=[shape, shape],
        compiler_params=_cparams(("arbitrary", "arbitrary")),
    )(*args)


def kernel(x_prompt, x_sample, cache_k, cache_v, page_table, state_conv, state_ssm, state_gla, norm1_g, w_in, sb_bias, conv_w, conv_b, dt_bias, a_log, d_skip, ssm_norm_g, gla_gk_w, gla_gk_b, gla_norm_g, w_out, norm2_g, w_gate, w_up, w_down, final_norm_g):
    bp, tp, d = x_prompt.shape
    bs, ts, _ = x_sample.shape
    depth = w_in.shape[0]
    mp = bp * tp
    ms = bs * ts
    m = mp + ms
    tm = _row_tile(m, MM_ROWS)
    tm_s = _row_tile(m, MM_ROWS_SMALL)
    d_ff = w_gate.shape[2]
    KP = GLA_HEADS * GLA_K_DIM
    off_xbc = 3 * D_SB + D_SSM
    n_tail = SSM_CONV - 1

    x = jnp.concatenate([x_prompt.reshape(mp, d), x_sample.reshape(ms, d)], axis=0)
    w_down_bf = _cast_bf16(w_down, _row_tile(d_ff, MM_ROWS_SMALL))
    w_in_nk = jnp.swapaxes(w_in, 1, 2)
    outs = {k: [] for k in ("cp", "sp", "gp", "ks", "vs", "cs", "ss", "gs")}
    projs = []

    for l in range(depth):
        xn = _rmsnorm(x, norm1_g[l], BF16, tm_s)
        proj = _matmul(xn, w_in_nk, l, None, tm, MM_COLS, w_is_nk=True)

        mixed = jnp.zeros((m, d), BF16)
        mixed = _sb_prompt(proj, sb_bias[l], bp, tp, mixed, D_SSM)
        ctx_p = jnp.zeros((bp, 8, SSM_CONV_DIM), F32)
        mixed, ssm_p = _ssd(proj, 0, bp, tp, ctx_p, None, conv_w[l], conv_b[l], dt_bias[l], a_log[l],
                            d_skip[l], ssm_norm_g[l], BF16, buf=mixed, col0=0)
        mixed, gla_p = _gla(proj, 0, bp, tp, None,
                            gla_gk_w[l], gla_gk_b[l], gla_norm_g[l], BF16, buf=mixed, col0=D_SSM + D_SB)

        q_s = proj[mp:, 0:D_SB].reshape(bs, ts, D_SB)
        k_s = proj[mp:, D_SB:2 * D_SB].reshape(bs, ts, D_SB)
        v_s = proj[mp:, 2 * D_SB:3 * D_SB].reshape(bs, ts, D_SB)
        o_sb_s = _sb_sample(q_s, k_s, v_s, sb_bias[l], cache_k, cache_v, page_table, l)
        ctx_s = jnp.pad(state_conv[l], ((0, 0), (8 - (SSM_CONV - 1), 0), (0, 0)))
        y_ssm_s, ssm_s = _ssd(proj, mp, bs, ts, ctx_s, state_ssm[l].reshape(bs, D_SSM, SSM_STATE),
                              conv_w[l], conv_b[l], dt_bias[l], a_log[l], d_skip[l], ssm_norm_g[l], F32)
        o_gl_s, gla_s = _gla(proj, mp, bs, ts, state_gla[l].reshape(bs, KP, GLA_V_DIM),
                             gla_gk_w[l], gla_gk_b[l], gla_norm_g[l], F32)

        mixed = _put_rows(mixed, [y_ssm_s, o_sb_s, o_gl_s], mp)
        x = _matmul(mixed, w_out, l, x, tm, MM_COLS,
                    k_order=((D_SB, D_SB + D_SSM), (0, D_SB), (D_SB + D_SSM, D_SB + D_SSM + D_GLA)))
        hn = _rmsnorm(x, norm2_g[l], BF16, tm_s)
        hmid = _gate_up(hn, w_gate, w_up, l, tm, MM_COLS // 2)
        x = _matmul_bf16w(hmid, w_down_bf, l, x, tm_s, MM_COLS)

        xbc_p = jnp.stack([proj[(b + 1) * tp - n_tail:(b + 1) * tp, off_xbc:off_xbc + SSM_CONV_DIM]
                           for b in range(bp)])
        xbc_s = jnp.concatenate([state_conv[l], proj[mp:, off_xbc:off_xbc + SSM_CONV_DIM].reshape(bs, ts, -1)], axis=1)
        projs.append(proj)
        outs["cp"].append(xbc_p)
        outs["sp"].append(ssm_p.reshape(bp, SSM_HEADS, SSM_HEAD_DIM, SSM_STATE))
        outs["gp"].append(gla_p.reshape(bp, GLA_HEADS, GLA_K_DIM, GLA_V_DIM))
        outs["ks"].append(k_s.reshape(bs, ts, SB_HEADS, SB_HEAD_DIM))
        outs["vs"].append(v_s.reshape(bs, ts, SB_HEADS, SB_HEAD_DIM))
        outs["cs"].append(xbc_s[:, xbc_s.shape[1] - n_tail:])
        outs["ss"].append(ssm_s.reshape(bs, SSM_HEADS, SSM_HEAD_DIM, SSM_STATE))
        outs["gs"].append(gla_s.reshape(bs, GLA_HEADS, GLA_K_DIM, GLA_V_DIM))

    y_p = _rmsnorm(x, final_norm_g, F32, _row_tile(mp, COPY_ROWS), rows=mp)
    y_s = _rmsnorm(x[mp:], final_norm_g, F32, ms)
    st = {k: jnp.stack(v) for k, v in outs.items()}
    kp, vp = _head_rows(projs, mp, _row_tile(mp, COPY_ROWS))
    return (y_p.reshape(bp, tp, d), y_s.reshape(bs, ts, d),
            kp.reshape(depth, bp, tp, SB_HEADS, SB_HEAD_DIM), vp.reshape(depth, bp, tp, SB_HEADS, SB_HEAD_DIM),
            st["cp"], st["sp"], st["gp"],
            st["ks"], st["vs"], st["cs"], st["ss"], st["gs"])
```

```python
import functools

import jax
import jax.numpy as jnp
from jax import lax
from jax.experimental import pallas as pl
from jax.experimental.pallas import tpu as pltpu

F32 = jnp.float32
BF16 = jnp.bfloat16

EPS = 1e-6
SB_HEADS = 8
SB_HEAD_DIM = 128
D_SB = SB_HEADS * SB_HEAD_DIM
SSM_HEADS = 32
SSM_HEAD_DIM = 64
D_SSM = SSM_HEADS * SSM_HEAD_DIM
SSM_STATE = 128
SSM_GROUPS = 4
SSM_CONV = 4
SSM_CONV_DIM = D_SSM + 2 * SSM_GROUPS * SSM_STATE
GLA_HEADS = 8
GLA_K_DIM = 64
GLA_V_DIM = 128
D_GLA = GLA_HEADS * GLA_V_DIM
GLA_RANK = 16
GLA_TAU = 16.0
GLA_CHUNK = 64
PAGE_SIZE = 128

LANES = 128
TILE = 128
VMEM_LIMIT = 60 * 1024 * 1024
MM_ROWS = 1376
MM_ROWS_SMALL = 688
MM_COLS = 512
COPY_ROWS = 512


def _cparams(sem):
    return pltpu.CompilerParams(dimension_semantics=sem, vmem_limit_bytes=VMEM_LIMIT)


def _softplus(z):
    return jnp.maximum(z, 0.0) + jnp.log(1.0 + jnp.exp(-jnp.abs(z)))


def _silu(x):
    return x * (0.5 + 0.5 * jnp.tanh(0.5 * x))


def _split2(d):
    d0 = d.astype(BF16)
    d1 = (d - d0.astype(F32)).astype(BF16)
    return d0, d1


def _dot(a, b):
    return jnp.dot(a, b, preferred_element_type=F32)


def _dot_nt(a, b):
    return lax.dot_general(a, b, (((1,), (1,)), ((), ())), preferred_element_type=F32)


def _sel_dot_r(data, sel):
    d0, d1 = _split2(data)
    return _dot(d0, sel) + _dot(d1, sel)


def _sel_dot_l(sel, data):
    d0, d1 = _split2(data)
    return _dot(sel, d0) + _dot(sel, d1)


def _iota(shape, dim):
    return lax.broadcasted_iota(jnp.int32, shape, dim)


def _rmsnorm_kernel(x_ref, g_ref, o_ref):
    x = x_ref[...]
    y = x * lax.rsqrt(jnp.mean(x * x, axis=-1, keepdims=True) + EPS)
    o_ref[...] = (y * g_ref[...]).astype(o_ref.dtype)


def _row_tile(m, target):
    best = None
    for t in range(16, target + 1, 16):
        if m % t == 0:
            best = t
    assert best is not None, (m, target)
    return best


def _rmsnorm(x, g, out_dtype, tm, rows=None):
    m, d = x.shape
    m = m if rows is None else rows
    return pl.pallas_call(
        _rmsnorm_kernel,
        grid=(m // tm,),
        in_specs=[pl.BlockSpec((tm, d), lambda i: (i, 0)), pl.BlockSpec((1, d), lambda i: (0, 0))],
        out_specs=pl.BlockSpec((tm, d), lambda i: (i, 0)),
        out_shape=jax.ShapeDtypeStruct((m, d), out_dtype),
        compiler_params=_cparams(("arbitrary",)),
    )(x, g.reshape(1, d))


def _mm_kernel(x_ref, w_ref, *rest, has_res, w_is_nk, k_order):
    if has_res:
        res_ref, o_ref, wbf_ref = rest
    else:
        o_ref, wbf_ref = rest

    @pl.when(pl.program_id(1) == 0)
    def _():
        if k_order is None:
            wbf_ref[...] = w_ref[...].astype(BF16)
        else:
            off = 0
            for a, b in k_order:
                wbf_ref[off:off + b - a, :] = w_ref[a:b, :].astype(BF16)
                off += b - a

    acc = _dot_nt(x_ref[...], wbf_ref[...]) if w_is_nk else _dot(x_ref[...], wbf_ref[...])
    if has_res:
        acc = acc + res_ref[...]
    o_ref[...] = acc.astype(o_ref.dtype)


def _matmul(x, w, layer, res, tm, tn, w_is_nk=False, k_order=None):
    m, k = x.shape
    assert k_order is None or (not w_is_nk and sum(b - a for a, b in k_order) == k)
    if w_is_nk:
        n = w.shape[1]
        w_spec = pl.BlockSpec((None, tn, k), lambda j, i: (layer, j, 0))
        w_scratch = pltpu.VMEM((tn, k), BF16)
    else:
        n = w.shape[2]
        w_spec = pl.BlockSpec((None, k, tn), lambda j, i: (layer, 0, j))
        w_scratch = pltpu.VMEM((k, tn), BF16)
    in_specs = [pl.BlockSpec((tm, k), lambda j, i: (i, 0)), w_spec]
    args = [x, w]
    if res is not None:
        in_specs.append(pl.BlockSpec((tm, tn), lambda j, i: (i, j)))
        args.append(res)
    return pl.pallas_call(
        functools.partial(_mm_kernel, has_res=res is not None, w_is_nk=w_is_nk, k_order=k_order),
        grid=(pl.cdiv(n, tn), m // tm),
        in_specs=in_specs,
        out_specs=pl.BlockSpec((tm, tn), lambda j, i: (i, j)),
        out_shape=jax.ShapeDtypeStruct((m, n), F32),
        scratch_shapes=[w_scratch],
        compiler_params=_cparams(("arbitrary", "arbitrary")),
    )(*args)


MM_SLABS = 4


def _slabs(rows, n):
    step = -(-rows // (16 * n)) * 16
    bounds = [min(i * step, rows) for i in range(n + 1)]
    return [(a, b) for a, b in zip(bounds[:-1], bounds[1:]) if b > a]


def _gate_up_kernel(x_ref, wg_ref, wu_ref, o_ref, wbf_ref, *, tn):
    @pl.when(pl.program_id(1) == 0)
    def _():
        wbf_ref[:, :tn] = wg_ref[...].astype(BF16)
        wbf_ref[:, tn:] = wu_ref[...].astype(BF16)

    for r0, r1 in _slabs(o_ref.shape[0], MM_SLABS):
        gu = _dot(x_ref[r0:r1, :], wbf_ref[...])
        o_ref[r0:r1, :] = (_silu(gu[:, :tn]) * gu[:, tn:]).astype(o_ref.dtype)


def _gate_up(x, wg, wu, layer, tm, tn):
    m, k = x.shape
    f = wg.shape[2]
    wspec = pl.BlockSpec((None, k, tn), lambda j, i: (layer, 0, j))
    return pl.pallas_call(
        functools.partial(_gate_up_kernel, tn=tn),
        grid=(f // tn, m // tm),
        in_specs=[pl.BlockSpec((tm, k), lambda j, i: (i, 0)), wspec, wspec],
        out_specs=pl.BlockSpec((tm, tn), lambda j, i: (i, j)),
        out_shape=jax.ShapeDtypeStruct((m, f), BF16),
        scratch_shapes=[pltpu.VMEM((k, 2 * tn), BF16)],
        compiler_params=_cparams(("arbitrary", "arbitrary")),
    )(x, wg, wu)


def _cast_kernel(w_ref, o_ref):
    o_ref[...] = w_ref[...].astype(o_ref.dtype)


def _cast_bf16(w, tr):
    nl, r, c = w.shape
    spec = pl.BlockSpec((None, tr, c), lambda l, i: (l, i, 0))
    return pl.pallas_call(
        _cast_kernel,
        grid=(nl, r // tr),
        in_specs=[spec],
        out_specs=spec,
        out_shape=jax.ShapeDtypeStruct(w.shape, BF16),
        compiler_params=_cparams(("arbitrary", "arbitrary")),
    )(w)


def _mm_res_kernel(x_ref, w_ref, res_ref, o_ref):
    o_ref[...] = _dot(x_ref[...], w_ref[...]) + res_ref[...]


def _matmul_bf16w(x, w, layer, res, tm, tn):
    m, k = x.shape
    n = w.shape[2]
    return pl.pallas_call(
        _mm_res_kernel,
        grid=(m // tm, n // tn),
        in_specs=[pl.BlockSpec((tm, k), lambda i, j: (i, 0)),
                  pl.BlockSpec((None, k, tn), lambda i, j: (layer, 0, j)),
                  pl.BlockSpec((tm, tn), lambda i, j: (i, j))],
        out_specs=pl.BlockSpec((tm, tn), lambda i, j: (i, j)),
        out_shape=jax.ShapeDtypeStruct((m, n), F32),
        compiler_params=_cparams(("arbitrary", "arbitrary")),
    )(x, w, res)


SB_KEYS = 256
SB_UNROLL = 2

def _sb_prompt_kernel(bias_ref, q_ref, k_ref, v_ref, buf_ref, o_ref, acc_ref, run_ref, *, tq, scale):
    del buf_ref
    h = pl.program_id(1)
    i = pl.program_id(2)
    nk = SB_KEYS
    bias = bias_ref[h]
    q = (q_ref[...] * scale).astype(BF16)
    later = _iota((nk, nk), 0) > _iota((nk, nk), 1)
    suffix_total = jnp.concatenate([later.astype(BF16), jnp.ones((nk, LANES), BF16)], axis=1)

    def block(qv, start, run, mask):
        kb = k_ref[pl.ds(start, nk), :].astype(BF16)
        vb = v_ref[pl.ds(start, nk), :].astype(BF16)
        z = _dot_nt(qv, kb) + bias
        sp = _softplus(z)
        lg = -sp if mask is None else jnp.where(mask, -sp, 0.0)
        st = _dot(lg.astype(BF16), suffix_total)
        e = jnp.exp(z - sp + st[:, :nk] + jnp.concatenate([run] * (nk // LANES), axis=1))
        w = e if mask is None else jnp.where(mask, e, 0.0)
        return _dot(w.astype(BF16), vb), run + st[:, nk:]

    acc_ref[...] = jnp.zeros_like(acc_ref)
    run_ref[...] = jnp.zeros_like(run_ref)
    q0 = pl.multiple_of(i * tq, tq)
    for jd in reversed(range(tq // nk)):
        r0 = jd * nk
        rows = tq - r0
        mask = _iota((rows, nk), 1) < _iota((rows, nk), 0)
        pv, run = block(q[r0:], q0 + r0, run_ref[r0:, :], mask)
        acc_ref[r0:, :] += pv
        run_ref[r0:, :] = run

    def body(g, carry):
        run = run_ref[...]
        total = None
        for u in range(SB_UNROLL):
            start = pl.multiple_of(q0 - (g * SB_UNROLL + u + 1) * nk, nk)
            pv, run = block(q, start, run, None)
            total = pv if total is None else total + pv
        acc_ref[...] += total
        run_ref[...] = run
        return carry

    lax.fori_loop(0, (i * tq) // (SB_UNROLL * nk), body, 0)
    o_ref[...] = acc_ref[...].astype(o_ref.dtype)


def _sb_prompt(proj, bias, nb, t, buf, col0, tq=2048):
    assert t % tq == 0 and tq % (SB_UNROLL * SB_KEYS) == 0 and col0 % SB_HEAD_DIM == 0
    nq = t // tq
    hb = SB_HEADS
    cb0 = col0 // SB_HEAD_DIM
    return pl.pallas_call(
        functools.partial(_sb_prompt_kernel, tq=tq, scale=SB_HEAD_DIM ** -0.5),
        grid=(nb, hb, nq),
        in_specs=[pl.BlockSpec(memory_space=pltpu.SMEM),
                  pl.BlockSpec((tq, SB_HEAD_DIM), lambda b, h, i: (b * nq + i, h)),
                  pl.BlockSpec((t, SB_HEAD_DIM), lambda b, h, i: (b, hb + h)),
                  pl.BlockSpec((t, SB_HEAD_DIM), lambda b, h, i: (b, 2 * hb + h)),
                  pl.BlockSpec(memory_space=pl.ANY)],
        out_specs=pl.BlockSpec((tq, SB_HEAD_DIM), lambda b, h, i: (b * nq + i, cb0 + h)),
        out_shape=jax.ShapeDtypeStruct(buf.shape, buf.dtype),
        input_output_aliases={4: 0},
        scratch_shapes=[pltpu.VMEM((tq, SB_HEAD_DIM), F32), pltpu.VMEM((tq, LANES), F32)],
        compiler_params=_cparams(("arbitrary", "arbitrary", "arbitrary")),
    )(bias, proj, proj, proj, buf)


def _sb_sample_kernel(pt_ref, qbd_ref, bias_ref, kown_ref, vown_ref, *refs, pp, nq):
    del pt_ref
    kp = refs[:pp]
    vp = refs[pp:2 * pp]
    o_ref = refs[2 * pp]
    acc_ref, run_ref = refs[2 * pp + 1:]
    s = pl.program_id(1)
    n = PAGE_SIZE
    rows_c = 2 * nq
    earlier = (_iota((n, n), 1) > _iota((n, n), 0)).astype(BF16)
    qbd = qbd_ref[...]
    bias = bias_ref[...]

    def head_rows(ref, hh):
        return ref[pl.ds(hh, n, stride=SB_HEADS), :].astype(BF16)

    def scores(k_ref, mask):
        kcat = jnp.concatenate([head_rows(k_ref, hh) for hh in range(SB_HEADS)], axis=1)
        z = _dot(kcat, qbd) + bias
        sp = _softplus(z)
        lg = -sp if mask is None else jnp.where(mask, -sp, 0.0)
        return z - sp, _dot(earlier, lg.astype(BF16)), jnp.sum(lg, axis=0, keepdims=True)

    def add_values(v_ref, w, acc):
        wt = w.T.astype(BF16)
        return [acc[hh] + _dot(wt[(hh // 2) * rows_c:(hh // 2 + 1) * rows_c, :], head_rows(v_ref, hh))
                for hh in range(SB_HEADS)]

    @pl.when(s == 0)
    def _():
        mask = _iota((n, LANES), 0) < (_iota((n, LANES), 1) % nq)
        log_beta, local, total = scores(kown_ref, mask)
        w = jnp.where(mask, jnp.exp(log_beta + local), 0.0)
        acc = add_values(vown_ref, w, [jnp.zeros((rows_c, SB_HEAD_DIM), F32)] * SB_HEADS)
        for hh in range(SB_HEADS):
            acc_ref[hh] = acc[hh]
        run_ref[...] = total

    kcat = jnp.concatenate(
        [jnp.concatenate([head_rows(kp[c], hh) for hh in range(SB_HEADS)], axis=1) for c in range(pp)], axis=0)
    z = _dot(kcat, qbd) + bias
    sp = _softplus(z)
    lg = -sp
    lgb = lg.astype(BF16)
    local = jnp.concatenate([_dot(earlier, lgb[c * n:(c + 1) * n]) for c in range(pp)], axis=0)
    run = run_ref[...]
    runs = []
    for c in range(pp):
        runs.append(jnp.broadcast_to(run, (n, LANES)))
        run = run + jnp.sum(lg[c * n:(c + 1) * n], axis=0, keepdims=True)
    run_ref[...] = run
    w = jnp.exp(z - sp + local + jnp.concatenate(runs, axis=0))
    wt = w.T.astype(BF16)
    for hh in range(SB_HEADS):
        vh = jnp.concatenate([head_rows(vp[c], hh) for c in range(pp)], axis=0)
        acc_ref[hh] += _dot(wt[(hh // 2) * rows_c:(hh // 2 + 1) * rows_c, :], vh)

    @pl.when(s == pl.num_programs(1) - 1)
    def _():
        for hh in range(SB_HEADS):
            e = hh % 2
            o_ref[:, hh * SB_HEAD_DIM:(hh + 1) * SB_HEAD_DIM] = acc_ref[hh, e * nq:(e + 1) * nq, :]


def _sb_sample(q, k_own, v_own, bias, cache_k, cache_v, page_table, layer, pp=16):
    db, nq, _ = q.shape
    assert 2 * nq == 16 and SB_HEADS * nq <= LANES and cache_k.shape[2:] == (PAGE_SIZE, SB_HEADS, SB_HEAD_DIM)
    depth, n_pool = cache_k.shape[0], cache_k.shape[1]
    n_pages = page_table.shape[1]
    scale = SB_HEAD_DIM ** -0.5
    qh = (q * scale).reshape(db, nq, SB_HEADS, SB_HEAD_DIM)
    eye = jnp.eye(SB_HEADS, dtype=F32)
    qbd = jnp.einsum('bthd,hg->bhdgt', qh, eye).reshape(db, D_SB, SB_HEADS * nq)
    qbd = jnp.pad(qbd, ((0, 0), (0, 0), (0, LANES - SB_HEADS * nq))).astype(BF16)
    bias_l = jnp.pad(jnp.repeat(bias.astype(F32), nq), (0, LANES - SB_HEADS * nq)).reshape(1, LANES)
    pad = ((0, 0), (0, PAGE_SIZE - nq), (0, 0))
    page_rows = PAGE_SIZE * SB_HEADS
    kown = jnp.pad(k_own, pad).reshape(db * page_rows, SB_HEAD_DIM)
    vown = jnp.pad(v_own, pad).reshape(db * page_rows, SB_HEAD_DIM)
    ck = cache_k.reshape(depth * n_pool * page_rows, SB_HEAD_DIM)
    cv = cache_v.reshape(depth * n_pool * page_rows, SB_HEAD_DIM)
    base = layer * n_pool

    def page_map(c):
        return lambda b, s, pt: (base + pt[b, n_pages - 1 - (s * pp + c)], 0)

    page_spec = [pl.BlockSpec((page_rows, SB_HEAD_DIM), page_map(c)) for c in range(pp)]
    own_spec = pl.BlockSpec((page_rows, SB_HEAD_DIM), lambda b, s, pt: (b, 0))
    grid_spec = pltpu.PrefetchScalarGridSpec(
        num_scalar_prefetch=1,
        grid=(db, n_pages // pp),
        in_specs=[pl.BlockSpec((None, D_SB, LANES), lambda b, s, pt: (b, 0, 0)),
                  pl.BlockSpec((1, LANES), lambda b, s, pt: (0, 0)),
                  own_spec, own_spec] + page_spec + page_spec,
        out_specs=pl.BlockSpec((nq, D_SB), lambda b, s, pt: (b, 0)),
        scratch_shapes=[pltpu.VMEM((SB_HEADS, 2 * nq, SB_HEAD_DIM), F32),
                        pltpu.VMEM((1, LANES), F32)],
    )
    return pl.pallas_call(
        functools.partial(_sb_sample_kernel, pp=pp, nq=nq),
        grid_spec=grid_spec,
        out_shape=jax.ShapeDtypeStruct((db * nq, D_SB), F32),
        compiler_params=_cparams(("arbitrary", "arbitrary")),
    )(page_table, qbd, bias_l, kown, vown, *([ck] * pp), *([cv] * pp))


def _ssd_kernel(*refs, tb, has_init, has_buf):
    if has_buf:
        refs = refs[:-6] + refs[-5:]
    if has_init:
        (z0_ref, z1_ref, xa_ref, xb_ref, bc_ref, dt_ref, ctx_ref, h0_ref, cw_ref, cb_ref, dtb_ref, alog_ref,
         dsk_ref, ng_ref, spr_ref, sprt_ref, y_ref, hT_ref, xp_ref, hs_ref, yb_ref) = refs
    else:
        (z0_ref, z1_ref, xa_ref, xb_ref, bc_ref, dt_ref, ctx_ref, cw_ref, cb_ref, dtb_ref, alog_ref,
         dsk_ref, ng_ref, spr_ref, sprt_ref, y_ref, hT_ref, xp_ref, hs_ref, yb_ref) = refs
        h0_ref = None
    c = pl.program_id(1)
    L = TILE
    P2 = LANES
    n_pairs = D_SSM // P2

    @pl.when(c == 0)
    def _():
        xp_ref[0:8, :] = ctx_ref[...]
        if has_init:
            hs_ref[...] = h0_ref[...]
        else:
            hs_ref[...] = jnp.zeros_like(hs_ref)

    xp_ref[8:8 + tb, 0:1024] = xa_ref[...]
    xp_ref[8:8 + tb, 1024:2048] = xb_ref[...]
    xp_ref[8:8 + tb, 2048:3072] = bc_ref[...]
    if tb < L:
        xp_ref[8 + tb:8 + L, :] = jnp.zeros((L - tb, SSM_CONV_DIM), F32)

    conv = cb_ref[...] + cw_ref[0:1, :] * xp_ref[5:5 + L, :]
    for i in range(1, SSM_CONV):
        conv = conv + cw_ref[i:i + 1, :] * xp_ref[5 + i:5 + i + L, :]
    xp_ref[0:8, :] = xp_ref[tb:tb + 8, :]
    xc = _silu(conv)

    row = _iota((L, LANES), 0)
    lane = _iota((L, LANES), 1)
    dt_raw = dt_ref[...]
    if tb < L:
        dt_raw = jnp.concatenate([dt_raw, jnp.zeros((L - tb, LANES), F32)], axis=0)
    valid = (lane < SSM_HEADS) & (row < tb)
    dtv = jnp.where(valid, _softplus(dt_raw + dtb_ref[...]), 0.0)
    a = dtv * (-jnp.exp(alog_ref[...]))
    lower = (_iota((L, L), 0) >= _iota((L, L), 1))
    lower_b = lower.astype(BF16)
    upper_b = (_iota((L, L), 0) <= _iota((L, L), 1)).astype(BF16)
    cum = _sel_dot_l(lower_b, a)
    cum_t = _sel_dot_r(a.T, upper_b)
    last = cum[L - 1:L, :]
    spread = spr_ref[...]
    spread_t = sprt_ref[...]
    dt_x = _sel_dot_r(dtv, spread)
    ecum_x = _sel_dot_r(jnp.exp(cum), spread)
    wdec_x = _sel_dot_r(jnp.exp(last - cum), spread)
    sdec = _sel_dot_l(spread_t, jnp.broadcast_to(jnp.exp(cum_t[:, L - 1:L]), (LANES, LANES)))

    first_half = lane < SSM_HEAD_DIM
    pairs_per_group = n_pairs // SSM_GROUPS
    x_all = xc[:, :D_SSM]
    xs_all = x_all * dt_x
    xs_bf = xs_all.astype(BF16)
    xw_all = xs_all * wdec_x
    h_old = hs_ref[...]
    h_old_b = h_old.astype(BF16)
    gcols = D_SSM // SSM_GROUPS
    for g in range(SSM_GROUPS):
        bg = xc[:, D_SSM + g * SSM_STATE:D_SSM + (g + 1) * SSM_STATE]
        cg = xc[:, D_SSM + (SSM_GROUPS + g) * SSM_STATE:D_SSM + (SSM_GROUPS + g + 1) * SSM_STATE]
        bg_b = bg.astype(BF16)
        cg_b = cg.astype(BF16)
        cb = _dot_nt(cg_b, bg_b)
        gs = slice(g * gcols, (g + 1) * gcols)
        inter = _dot_nt(cg_b, h_old_b[gs, :]) * ecum_x[:, gs]
        hs_ref[gs, :] = sdec[gs, :] * h_old[gs, :] + _dot(xw_all[:, gs].T.astype(BF16), bg_b)
        for r in range(pairs_per_group):
            p = g * pairs_per_group + r
            sl = slice(p * P2, (p + 1) * P2)
            xs_b = xs_bf[:, sl]
            ys = []
            for e in range(2):
                hh = 2 * p + e
                seg = jnp.broadcast_to(cum[:, hh:hh + 1], (L, L)) - jnp.broadcast_to(cum_t[hh:hh + 1, :], (L, L))
                dec = jnp.where(lower, jnp.exp(jnp.minimum(seg, 0.0)), 0.0)
                ys.append(_dot((cb * dec).astype(BF16), xs_b))
            yb_ref[:, sl] = jnp.where(first_half, ys[0], ys[1]) + inter[:, r * P2:(r + 1) * P2]

    zfull = jnp.concatenate([z0_ref[...], z1_ref[...]], axis=1)
    y = (yb_ref[0:tb, :] + dsk_ref[...] * x_all[0:tb, :]) * _silu(zfull)
    gw = D_SSM // SSM_GROUPS
    for g in range(SSM_GROUPS):
        yg = y[:, g * gw:(g + 1) * gw]
        yn = yg * lax.rsqrt(jnp.mean(yg * yg, axis=-1, keepdims=True) + EPS)
        y_ref[:, g * gw:(g + 1) * gw] = (yn * ng_ref[:, g * gw:(g + 1) * gw]).astype(y_ref.dtype)

    @pl.when(c == pl.num_programs(1) - 1)
    def _():
        hT_ref[...] = hs_ref[...]


def _ssd(proj, row0, nb, t, ctx8, h0, conv_w, conv_b, dt_bias, a_log, d_skip, norm_g, out_dtype, buf=None, col0=0):
    tb = min(t, TILE)
    nc = t // tb
    rb0 = row0 // tb
    has_init = h0 is not None

    def colblk(width, idx):
        return pl.BlockSpec((tb, width), lambda b, c: (rb0 + b * nc + c, idx))

    def full(shape):
        return pl.BlockSpec(shape, lambda b, c: (0,) * len(shape))

    in_specs = [colblk(1024, 3), colblk(1024, 4), colblk(1024, 5), colblk(1024, 6), colblk(1024, 7),
                colblk(LANES, 64),
                pl.BlockSpec((None, 8, SSM_CONV_DIM), lambda b, c: (b, 0, 0))]
    args = [proj] * 6 + [ctx8]
    if has_init:
        in_specs.append(pl.BlockSpec((None, D_SSM, SSM_STATE), lambda b, c: (b, 0, 0)))
        args.append(h0)
    padl = (0, LANES - SSM_HEADS)
    in_specs += [full((SSM_CONV, SSM_CONV_DIM)), full((1, SSM_CONV_DIM)), full((1, LANES)), full((1, LANES)),
                 full((1, D_SSM)), full((1, D_SSM)), full((LANES, D_SSM)), full((D_SSM, LANES))]
    spread = (jnp.arange(LANES)[:, None] == jnp.arange(D_SSM)[None, :] // SSM_HEAD_DIM).astype(BF16)
    args += [conv_w, conv_b.reshape(1, -1), jnp.pad(dt_bias, padl).reshape(1, LANES),
             jnp.pad(a_log, padl).reshape(1, LANES), jnp.repeat(d_skip, SSM_HEAD_DIM).reshape(1, D_SSM),
             norm_g.reshape(1, D_SSM), spread, spread.T]
    aliases = {}
    y_shape = jax.ShapeDtypeStruct((nb * t, D_SSM), out_dtype)
    if buf is not None:
        assert col0 % D_SSM == 0 and row0 == 0
        in_specs.append(pl.BlockSpec(memory_space=pl.ANY))
        args.append(buf)
        aliases = {len(args) - 1: 0}
        y_shape = jax.ShapeDtypeStruct(buf.shape, buf.dtype)
    cb0 = col0 // D_SSM
    return pl.pallas_call(
        functools.partial(_ssd_kernel, tb=tb, has_init=has_init, has_buf=buf is not None),
        grid=(nb, nc),
        in_specs=in_specs,
        out_specs=[pl.BlockSpec((tb, D_SSM), lambda b, c: (b * nc + c, cb0)),
                   pl.BlockSpec((None, D_SSM, SSM_STATE), lambda b, c: (b, 0, 0))],
        out_shape=[y_shape, jax.ShapeDtypeStruct((nb, D_SSM, SSM_STATE), F32)],
        input_output_aliases=aliases,
        scratch_shapes=[pltpu.VMEM((8 + TILE, SSM_CONV_DIM), F32),
                        pltpu.VMEM((D_SSM, SSM_STATE), F32),
                        pltpu.VMEM((TILE, D_SSM), F32)],
        compiler_params=_cparams(("arbitrary", "arbitrary")),
    )(*args)


def _gla_kernel(*refs, tb, has_init, has_buf):
    if has_buf:
        refs = refs[:-4] + refs[-3:]
    wide, tail_ref, refs = refs[:GLA_N_WIDE], refs[GLA_N_WIDE], refs[GLA_N_WIDE + 1:]
    if has_init:
        (s0_ref, gw_ref, gb_ref, ng_ref, o_ref, sT_ref, st_ref) = refs
    else:
        (gw_ref, gb_ref, ng_ref, o_ref, sT_ref, st_ref) = refs
        s0_ref = None
    c = pl.program_id(1)
    L = TILE
    tail = tail_ref[...]
    region = jnp.concatenate([w[...] for w in wide] + [tail], axis=1)
    region = pltpu.roll(region, region.shape[1] - GLA_SHIFT, 1)
    kp_, o0 = GLA_HEADS * GLA_K_DIM, 0
    q_in = region[:, o0:o0 + kp_]
    k_in = region[:, o0 + kp_:o0 + 2 * kp_]
    v_in = region[:, o0 + 2 * kp_:o0 + 2 * kp_ + D_GLA]
    r_in = region[:, o0 + 2 * kp_ + D_GLA:o0 + 2 * kp_ + 2 * D_GLA]
    tail_lane = _iota(tail.shape, 1)
    gate_in = jnp.where((tail_lane >= GLA_SHIFT) & (tail_lane < GLA_SHIFT + GLA_RANK), tail, 0.0)
    CH = min(GLA_CHUNK, tb)
    n_sub = max(tb // CH, 1)
    KP = GLA_HEADS * GLA_K_DIM

    @pl.when(c == 0)
    def _():
        if has_init:
            st_ref[...] = s0_ref[...]
        else:
            st_ref[...] = jnp.zeros_like(st_ref)

    def padrows(x):
        if tb < L:
            return jnp.concatenate([x, jnp.zeros((L - tb, x.shape[1]), F32)], axis=0)
        return x

    row = _iota((L, L), 0)
    col = _iota((L, L), 1)
    same_chunk = (row // CH) == (col // CH)
    lower = (row >= col) & same_chunk
    lower_b = lower.astype(BF16)
    rvalid = _iota((L, KP), 0) < tb
    glog = _dot(gate_in.astype(BF16), gw_ref[...].astype(BF16)) + gb_ref[...]
    glog = padrows(-_softplus(-glog) * (1.0 / GLA_TAU))
    glog = jnp.where(rvalid, glog, 0.0)
    q_all = padrows(q_in) * (GLA_K_DIM ** -0.5)
    k_all = jnp.where(rvalid, padrows(k_in), 0.0)
    v_all = padrows(v_in)
    lane = _iota((L, LANES), 1)
    rowl = _iota((L, LANES), 0)

    b_all = _sel_dot_l(lower_b, glog)
    qt_all = q_all * jnp.exp(b_all)
    kt_all = (k_all * jnp.exp(-b_all)).astype(BF16)
    lasts_all = [b_all[min((j + 1) * CH, L) - 1:min((j + 1) * CH, L), :] for j in range(n_sub)]
    chunk_of_row = _iota((L, KP), 0) // CH
    kw_all = [jnp.where(chunk_of_row == j, k_all * jnp.exp(lasts_all[j] - b_all), 0.0) for j in range(n_sub)]
    normed = []

    for p in range(KP // LANES):
        sl = slice(p * LANES, (p + 1) * LANES)
        qt = qt_all[:, sl]
        kt = kt_all[:, sl]
        lasts = [la[:, sl] for la in lasts_all]
        s_cur = st_ref[sl, :]
        s_list = [s_cur]
        kw_t = [kw[:, sl].T.astype(BF16) for kw in kw_all]
        heads_out = []
        for e in range(2):
            hh = 2 * p + e
            own = (lane // GLA_K_DIM) == e
            qm = jnp.where(own, qt, 0.0).astype(BF16)
            att = jnp.where(lower, _dot_nt(qm, kt), 0.0)
            v_h = v_all[:, hh * GLA_V_DIM:(hh + 1) * GLA_V_DIM].astype(BF16)
            heads_out.append((qm, _dot(att.astype(BF16), v_h), v_h))
        decs = []
        for j in range(n_sub):
            dcol = jnp.exp(lasts[j]).reshape(1, LANES)
            decs.append(jnp.broadcast_to(dcol, (LANES, LANES)).T)
        for j in range(n_sub):
            upd0 = _dot(kw_t[j], heads_out[0][2])
            upd1 = _dot(kw_t[j], heads_out[1][2])
            upd = jnp.where(_iota((LANES, GLA_V_DIM), 0) < GLA_K_DIM, upd0, upd1)
            s_list.append(decs[j] * s_list[j] + upd)
        st_ref[sl, :] = s_list[n_sub]
        for e in range(2):
            hh = 2 * p + e
            qm, o_h, _ = heads_out[e]
            inter = _dot(qm, s_list[0].astype(BF16))
            for j in range(1, n_sub):
                inter = jnp.where((rowl // CH) == j, _dot(qm, s_list[j].astype(BF16)), inter)
            o_h = (o_h + inter)[0:tb, :]
            normed.append(o_h * lax.rsqrt(jnp.mean(o_h * o_h, axis=-1, keepdims=True) + EPS))

    gain = jnp.concatenate([ng_ref[...]] * GLA_HEADS, axis=1)
    o_ref[...] = (jnp.concatenate(normed, axis=1) * gain * _silu(r_in)).astype(o_ref.dtype)

    @pl.when(c == pl.num_programs(1) - 1)
    def _():
        sT_ref[...] = st_ref[...]


GLA_COL0 = 3 * D_SB + D_SSM + SSM_CONV_DIM
GLA_SHIFT = SSM_HEADS
GLA_WIDE = 1024
GLA_N_WIDE = (2 * GLA_HEADS * GLA_K_DIM + 2 * D_GLA) // GLA_WIDE


def _gla(proj, row0, nb, t, s0, gk_w, gk_b, norm_g, out_dtype, buf=None, col0=0):
    tb = min(t, TILE)
    nc = t // tb
    rb0 = row0 // tb
    has_init = s0 is not None
    KP = GLA_HEADS * GLA_K_DIM
    assert GLA_COL0 % GLA_WIDE == 0 and (GLA_COL0 + GLA_N_WIDE * GLA_WIDE) % LANES == 0

    def colblk(width, idx):
        return pl.BlockSpec((tb, width), lambda b, c: (rb0 + b * nc + c, idx))

    def full(shape):
        return pl.BlockSpec(shape, lambda b, c: (0,) * len(shape))

    in_specs = [colblk(GLA_WIDE, GLA_COL0 // GLA_WIDE + i) for i in range(GLA_N_WIDE)]
    in_specs.append(colblk(LANES, (GLA_COL0 + GLA_N_WIDE * GLA_WIDE) // LANES))
    args = [proj] * (GLA_N_WIDE + 1)
    if has_init:
        in_specs.append(pl.BlockSpec((None, KP, GLA_V_DIM), lambda b, c: (b, 0, 0)))
        args.append(s0)
    in_specs += [full((LANES, KP)), full((1, KP)), full((1, GLA_V_DIM))]
    gw_rows = jnp.zeros((LANES, KP), F32).at[GLA_SHIFT:GLA_SHIFT + GLA_RANK].set(gk_w)
    args += [gw_rows, gk_b.reshape(1, KP), norm_g.reshape(1, GLA_V_DIM)]
    aliases = {}
    o_shape = jax.ShapeDtypeStruct((nb * t, D_GLA), out_dtype)
    if buf is not None:
        assert col0 % D_GLA == 0
        in_specs.append(pl.BlockSpec(memory_space=pl.ANY))
        args.append(buf)
        aliases = {len(args) - 1: 0}
        o_shape = jax.ShapeDtypeStruct(buf.shape, buf.dtype)
    cb0 = col0 // D_GLA
    return pl.pallas_call(
        functools.partial(_gla_kernel, tb=tb, has_init=has_init, has_buf=buf is not None),
        grid=(nb, nc),
        in_specs=in_specs,
        out_specs=[pl.BlockSpec((tb, D_GLA), lambda b, c: (b * nc + c, cb0)),
                   pl.BlockSpec((None, KP, GLA_V_DIM), lambda b, c: (b, 0, 0))],
        out_shape=[o_shape, jax.ShapeDtypeStruct((nb, KP, GLA_V_DIM), F32)],
        input_output_aliases=aliases,
        scratch_shapes=[pltpu.VMEM((KP, GLA_V_DIM), F32)],
        compiler_params=_cparams(("arbitrary", "arbitrary")),
    )(*args)


def _put_rows_kernel(*refs):
    parts, o_ref = refs[:-2], refs[-1]
    off = 0
    for p in parts:
        w = p.shape[1]
        o_ref[:, off:off + w] = p[...].astype(o_ref.dtype)
        off += w


def _put_rows(buf, parts, row0):
    rows = parts[0].shape[0]
    width = sum(p.shape[1] for p in parts)
    assert width == buf.shape[1] and row0 % rows == 0
    return pl.pallas_call(
        _put_rows_kernel,
        grid=(1,),
        in_specs=[pl.BlockSpec(p.shape, lambda i: (0, 0)) for p in parts] + [pl.BlockSpec(memory_space=pl.ANY)],
        out_specs=pl.BlockSpec((rows, width), lambda i: (row0 // rows, 0)),
        out_shape=jax.ShapeDtypeStruct(buf.shape, buf.dtype),
        input_output_aliases={len(parts): 0},
        compiler_params=_cparams(("arbitrary",)),
    )(*parts, buf)


def _head_rows_kernel(*refs, depth):
    ins, (ko_ref, vo_ref) = refs[:2 * depth], refs[2 * depth:]
    tr = ins[0].shape[0]
    for l in range(depth):
        @pl.when(pl.program_id(0) == l)
        def _(l=l):
            for src, dst in ((ins[2 * l], ko_ref), (ins[2 * l + 1], vo_ref)):
                for hh in range(SB_HEADS):
                    dst[pl.ds(hh, tr, stride=SB_HEADS), :] = src[:, hh * SB_HEAD_DIM:(hh + 1) * SB_HEAD_DIM]


def _head_rows(projs, rows, tr):
    depth = len(projs)
    nt = rows // tr
    in_specs, args = [], []
    for l, p in enumerate(projs):
        for cb in (1, 2):
            in_specs.append(pl.BlockSpec((tr, D_SB), lambda ll, i, l=l, cb=cb: (jnp.where(ll == l, i, 0), cb)))
            args.append(p)
    out_spec = pl.BlockSpec((tr * SB_HEADS, SB_HEAD_DIM), lambda ll, i: (ll * nt + i, 0))
    shape = jax.ShapeDtypeStruct((depth * rows * SB_HEADS, SB_HEAD_DIM), F32)
    return pl.pallas_call(
        functools.partial(_head_rows_kernel, depth=depth),
        grid=(depth, nt),
        in_specs=in_specs,
        out_specs=[out_spec, out_spec],
        out_shape=[shape, shape],
        compiler_params=_cparams(("arbitrary", "arbitrary")),
    )(*args)


def kernel(x_prompt, x_sample, cache_k, cache_v, page_table, state_conv, state_ssm, state_gla, norm1_g, w_in, sb_bias, conv_w, conv_b, dt_bias, a_log, d_skip, ssm_norm_g, gla_gk_w, gla_gk_b, gla_norm_g, w_out, norm2_g, w_gate, w_up, w_down, final_norm_g):
    bp, tp, d = x_prompt.shape
    bs, ts, _ = x_sample.shape
    depth = w_in.shape[0]
    mp = bp * tp
    ms = bs * ts
    m = mp + ms
    tm = _row_tile(m, MM_ROWS)
    tm_s = _row_tile(m, MM_ROWS_SMALL)
    d_ff = w_gate.shape[2]
    KP = GLA_HEADS * GLA_K_DIM
    off_xbc = 3 * D_SB + D_SSM
    n_tail = SSM_CONV - 1

    x = jnp.concatenate([x_prompt.reshape(mp, d), x_sample.reshape(ms, d)], axis=0)
    w_down_bf = _cast_bf16(w_down, _row_tile(d_ff, MM_ROWS_SMALL))
    w_in_nk = jnp.swapaxes(w_in, 1, 2)
    outs = {k: [] for k in ("cp", "sp", "gp", "ks", "vs", "cs", "ss", "gs")}
    projs = []

    for l in range(depth):
        xn = _rmsnorm(x, norm1_g[l], BF16, tm_s)
        proj = _matmul(xn, w_in_nk, l, None, tm, MM_COLS, w_is_nk=True)

        mixed = jnp.zeros((m, d), BF16)
        mixed = _sb_prompt(proj, sb_bias[l], bp, tp, mixed, D_SSM)
        ctx_p = jnp.zeros((bp, 8, SSM_CONV_DIM), F32)
        mixed, ssm_p = _ssd(proj, 0, bp, tp, ctx_p, None, conv_w[l], conv_b[l], dt_bias[l], a_log[l],
                            d_skip[l], ssm_norm_g[l], BF16, buf=mixed, col0=0)
        mixed, gla_p = _gla(proj, 0, bp, tp, None,
                            gla_gk_w[l], gla_gk_b[l], gla_norm_g[l], BF16, buf=mixed, col0=D_SSM + D_SB)

        q_s = proj[mp:, 0:D_SB].reshape(bs, ts, D_SB)
        k_s = proj[mp:, D_SB:2 * D_SB].reshape(bs, ts, D_SB)
        v_s = proj[mp:, 2 * D_SB:3 * D_SB].reshape(bs, ts, D_SB)
        o_sb_s = _sb_sample(q_s, k_s, v_s, sb_bias[l], cache_k, cache_v, page_table, l)
        ctx_s = jnp.pad(state_conv[l], ((0, 0), (8 - (SSM_CONV - 1), 0), (0, 0)))
        y_ssm_s, ssm_s = _ssd(proj, mp, bs, ts, ctx_s, state_ssm[l].reshape(bs, D_SSM, SSM_STATE),
                              conv_w[l], conv_b[l], dt_bias[l], a_log[l], d_skip[l], ssm_norm_g[l], F32)
        o_gl_s, gla_s = _gla(proj, mp, bs, ts, state_gla[l].reshape(bs, KP, GLA_V_DIM),
                             gla_gk_w[l], gla_gk_b[l], gla_norm_g[l], F32)

        mixed = _put_rows(mixed, [y_ssm_s, o_sb_s, o_gl_s], mp)
        x = _matmul(mixed, w_out, l, x, tm, MM_COLS,
                    k_order=((D_SB, D_SB + D_SSM), (0, D_SB), (D_SB + D_SSM, D_SB + D_SSM + D_GLA)))
        hn = _rmsnorm(x, norm2_g[l], BF16, tm_s)
        hmid = _gate_up(hn, w_gate, w_up, l, tm, MM_COLS // 2)
        x = _matmul_bf16w(hmid, w_down_bf, l, x, tm_s, MM_COLS)

        xbc_p = jnp.stack([proj[(b + 1) * tp - n_tail:(b + 1) * tp, off_xbc:off_xbc + SSM_CONV_DIM]
                           for b in range(bp)])
        xbc_s = jnp.concatenate([state_conv[l], proj[mp:, off_xbc:off_xbc + SSM_CONV_DIM].reshape(bs, ts, -1)], axis=1)
        projs.append(proj)
        outs["cp"].append(xbc_p)
        outs["sp"].append(ssm_p.reshape(bp, SSM_HEADS, SSM_HEAD_DIM, SSM_STATE))
        outs["gp"].append(gla_p.reshape(bp, GLA_HEADS, GLA_K_DIM, GLA_V_DIM))
        outs["ks"].append(k_s.reshape(bs, ts, SB_HEADS, SB_HEAD_DIM))
        outs["vs"].append(v_s.reshape(bs, ts, SB_HEADS, SB_HEAD_DIM))
        outs["cs"].append(xbc_s[:, xbc_s.shape[1] - n_tail:])
        outs["ss"].append(ssm_s.reshape(bs, SSM_HEADS, SSM_HEAD_DIM, SSM_STATE))
        outs["gs"].append(gla_s.reshape(bs, GLA_HEADS, GLA_K_DIM, GLA_V_DIM))

    y_p = _rmsnorm(x, final_norm_g, F32, _row_tile(mp, COPY_ROWS), rows=mp)
    y_s = _rmsnorm(x[mp:], final_norm_g, F32, ms)
    st = {k: jnp.stack(v) for k, v in outs.items()}
    kp, vp = _head_rows(projs, mp, _row_tile(mp, COPY_ROWS))
    return (y_p.reshape(bp, tp, d), y_s.reshape(bs, ts, d),
            kp.reshape(depth, bp, tp, SB_HEADS, SB_HEAD_DIM), vp.reshape(depth, bp, tp, SB_HEADS, SB_HEAD_DIM),
            st["cp"], st["sp"], st["gp"],
            st["ks"], st["vs"], st["cs"], st["ss"], st["gs"])
```

```python
import functools

import jax
import jax.numpy as jnp
from jax import lax
from jax.experimental import pallas as pl
from jax.experimental.pallas import tpu as pltpu

F32 = jnp.float32
BF16 = jnp.bfloat16

EPS = 1e-6
SB_HEADS = 8
SB_HEAD_DIM = 128
D_SB = SB_HEADS * SB_HEAD_DIM
SSM_HEADS = 32
SSM_HEAD_DIM = 64
D_SSM = SSM_HEADS * SSM_HEAD_DIM
SSM_STATE = 128
SSM_GROUPS = 4
SSM_CONV = 4
SSM_CONV_DIM = D_SSM + 2 * SSM_GROUPS * SSM_STATE
GLA_HEADS = 8
GLA_K_DIM = 64
GLA_V_DIM = 128
D_GLA = GLA_HEADS * GLA_V_DIM
GLA_RANK = 16
GLA_TAU = 16.0
GLA_CHUNK = 64
PAGE_SIZE = 128

LANES = 128
TILE = 128
VMEM_LIMIT = 60 * 1024 * 1024
MM_ROWS = 1376
MM_ROWS_SMALL = 688
MM_COLS = 512
COPY_ROWS = 512


def _cparams(sem):
    return pltpu.CompilerParams(dimension_semantics=sem, vmem_limit_bytes=VMEM_LIMIT)


def _softplus(z):
    return jnp.maximum(z, 0.0) + jnp.log(1.0 + jnp.exp(-jnp.abs(z)))


def _silu(x):
    return x * (0.5 + 0.5 * jnp.tanh(0.5 * x))


def _split2(d):
    d0 = d.astype(BF16)
    d1 = (d - d0.astype(F32)).astype(BF16)
    return d0, d1


def _dot(a, b):
    return jnp.dot(a, b, preferred_element_type=F32)


def _dot_nt(a, b):
    return lax.dot_general(a, b, (((1,), (1,)), ((), ())), preferred_element_type=F32)


def _sel_dot_r(data, sel):
    d0, d1 = _split2(data)
    return _dot(d0, sel) + _dot(d1, sel)


def _sel_dot_l(sel, data):
    d0, d1 = _split2(data)
    return _dot(sel, d0) + _dot(sel, d1)


def _iota(shape, dim):
    return lax.broadcasted_iota(jnp.int32, shape, dim)


def _rmsnorm_kernel(x_ref, g_ref, o_ref):
    x = x_ref[...]
    y = x * lax.rsqrt(jnp.mean(x * x, axis=-1, keepdims=True) + EPS)
    o_ref[...] = (y * g_ref[...]).astype(o_ref.dtype)


def _row_tile(m, target):
    best = None
    for t in range(16, target + 1, 16):
        if m % t == 0:
            best = t
    assert best is not None, (m, target)
    return best


def _rmsnorm(x, g, out_dtype, tm, rows=None):
    m, d = x.shape
    m = m if rows is None else rows
    return pl.pallas_call(
        _rmsnorm_kernel,
        grid=(m // tm,),
        in_specs=[pl.BlockSpec((tm, d), lambda i: (i, 0)), pl.BlockSpec((1, d), lambda i: (0, 0))],
        out_specs=pl.BlockSpec((tm, d), lambda i: (i, 0)),
        out_shape=jax.ShapeDtypeStruct((m, d), out_dtype),
        compiler_params=_cparams(("arbitrary",)),
    )(x, g.reshape(1, d))


def _mm_kernel(x_ref, w_ref, *rest, has_res, w_is_nk, k_order):
    if has_res:
        res_ref, o_ref, wbf_ref = rest
    else:
        o_ref, wbf_ref = rest

    @pl.when(pl.program_id(1) == 0)
    def _():
        if k_order is None:
            wbf_ref[...] = w_ref[...].astype(BF16)
        else:
            off = 0
            for a, b in k_order:
                wbf_ref[off:off + b - a, :] = w_ref[a:b, :].astype(BF16)
                off += b - a

    acc = _dot_nt(x_ref[...], wbf_ref[...]) if w_is_nk else _dot(x_ref[...], wbf_ref[...])
    if has_res:
        acc = acc + res_ref[...]
    o_ref[...] = acc.astype(o_ref.dtype)


def _matmul(x, w, layer, res, tm, tn, w_is_nk=False, k_order=None):
    m, k = x.shape
    assert k_order is None or (not w_is_nk and sum(b - a for a, b in k_order) == k)
    if w_is_nk:
        n = w.shape[1]
        w_spec = pl.BlockSpec((None, tn, k), lambda j, i: (layer, j, 0))
        w_scratch = pltpu.VMEM((tn, k), BF16)
    else:
        n = w.shape[2]
        w_spec = pl.BlockSpec((None, k, tn), lambda j, i: (layer, 0, j))
        w_scratch = pltpu.VMEM((k, tn), BF16)
    in_specs = [pl.BlockSpec((tm, k), lambda j, i: (i, 0)), w_spec]
    args = [x, w]
    if res is not None:
        in_specs.append(pl.BlockSpec((tm, tn), lambda j, i: (i, j)))
        args.append(res)
    return pl.pallas_call(
        functools.partial(_mm_kernel, has_res=res is not None, w_is_nk=w_is_nk, k_order=k_order),
        grid=(pl.cdiv(n, tn), m // tm),
        in_specs=in_specs,
        out_specs=pl.BlockSpec((tm, tn), lambda j, i: (i, j)),
        out_shape=jax.ShapeDtypeStruct((m, n), F32),
        scratch_shapes=[w_scratch],
        compiler_params=_cparams(("arbitrary", "arbitrary")),
    )(*args)


MM_SLABS = 4


def _slabs(rows, n):
    step = -(-rows // (16 * n)) * 16
    bounds = [min(i * step, rows) for i in range(n + 1)]
    return [(a, b) for a, b in zip(bounds[:-1], bounds[1:]) if b > a]


def _gate_up_kernel(x_ref, wg_ref, wu_ref, o_ref, wbf_ref, *, tn):
    @pl.when(pl.program_id(1) == 0)
    def _():
        wbf_ref[:, :tn] = wg_ref[...].astype(BF16)
        wbf_ref[:, tn:] = wu_ref[...].astype(BF16)

    for r0, r1 in _slabs(o_ref.shape[0], MM_SLABS):
        gu = _dot(x_ref[r0:r1, :], wbf_ref[...])
        o_ref[r0:r1, :] = (_silu(gu[:, :tn]) * gu[:, tn:]).astype(o_ref.dtype)


def _gate_up(x, wg, wu, layer, tm, tn):
    m, k = x.shape
    f = wg.shape[2]
    wspec = pl.BlockSpec((None, k, tn), lambda j, i: (layer, 0, j))
    return pl.pallas_call(
        functools.partial(_gate_up_kernel, tn=tn),
        grid=(f // tn, m // tm),
        in_specs=[pl.BlockSpec((tm, k), lambda j, i: (i, 0)), wspec, wspec],
        out_specs=pl.BlockSpec((tm, tn), lambda j, i: (i, j)),
        out_shape=jax.ShapeDtypeStruct((m, f), BF16),
        scratch_shapes=[pltpu.VMEM((k, 2 * tn), BF16)],
        compiler_params=_cparams(("arbitrary", "arbitrary")),
    )(x, wg, wu)


def _cast_kernel(w_ref, o_ref):
    o_ref[...] = w_ref[...].astype(o_ref.dtype)


def _cast_bf16(w, tr):
    nl, r, c = w.shape
    spec = pl.BlockSpec((None, tr, c), lambda l, i: (l, i, 0))
    return pl.pallas_call(
        _cast_kernel,
        grid=(nl, r // tr),
        in_specs=[spec],
        out_specs=spec,
        out_shape=jax.ShapeDtypeStruct(w.shape, BF16),
        compiler_params=_cparams(("arbitrary", "arbitrary")),
    )(w)


def _mm_res_kernel(x_ref, w_ref, res_ref, o_ref):
    o_ref[...] = _dot(x_ref[...], w_ref[...]) + res_ref[...]


def _matmul_bf16w(x, w, layer, res, tm, tn):
    m, k = x.shape
    n = w.shape[2]
    return pl.pallas_call(
        _mm_res_kernel,
        grid=(m // tm, n // tn),
        in_specs=[pl.BlockSpec((tm, k), lambda i, j: (i, 0)),
                  pl.BlockSpec((None, k, tn), lambda i, j: (layer, 0, j)),
                  pl.BlockSpec((tm, tn), lambda i, j: (i, j))],
        out_specs=pl.BlockSpec((tm, tn), lambda i, j: (i, j)),
        out_shape=jax.ShapeDtypeStruct((m, n), F32),
        compiler_params=_cparams(("arbitrary", "arbitrary")),
    )(x, w, res)


SB_KEYS = 256
SB_UNROLL = 2

def _sb_prompt_kernel(bias_ref, q_ref, k_ref, v_ref, buf_ref, o_ref, acc_ref, run_ref, *, tq, scale):
    del buf_ref
    h = pl.program_id(1)
    i = pl.program_id(2)
    nk = SB_KEYS
    bias = bias_ref[h]
    q = (q_ref[...] * scale).astype(BF16)
    later = _iota((nk, nk), 0) > _iota((nk, nk), 1)
    suffix_total = jnp.concatenate([later.astype(BF16), jnp.ones((nk, LANES), BF16)], axis=1)

    def block(qv, start, run, mask):
        kb = k_ref[pl.ds(start, nk), :].astype(BF16)
        vb = v_ref[pl.ds(start, nk), :].astype(BF16)
        z = _dot_nt(qv, kb) + bias
        sp = _softplus(z)
        lg = -sp if mask is None else jnp.where(mask, -sp, 0.0)
        st = _dot(lg.astype(BF16), suffix_total)
        e = jnp.exp(z - sp + st[:, :nk] + jnp.concatenate([run] * (nk // LANES), axis=1))
        w = e if mask is None else jnp.where(mask, e, 0.0)
        return _dot(w.astype(BF16), vb), run + st[:, nk:]

    acc_ref[...] = jnp.zeros_like(acc_ref)
    run_ref[...] = jnp.zeros_like(run_ref)
    q0 = pl.multiple_of(i * tq, tq)
    for jd in reversed(range(tq // nk)):
        r0 = jd * nk
        rows = tq - r0
        mask = _iota((rows, nk), 1) < _iota((rows, nk), 0)
        pv, run = block(q[r0:], q0 + r0, run_ref[r0:, :], mask)
        acc_ref[r0:, :] += pv
        run_ref[r0:, :] = run

    def body(g, carry):
        run = run_ref[...]
        total = None
        for u in range(SB_UNROLL):
            start = pl.multiple_of(q0 - (g * SB_UNROLL + u + 1) * nk, nk)
            pv, run = block(q, start, run, None)
            total = pv if total is None else total + pv
        acc_ref[...] += total
        run_ref[...] = run
        return carry

    lax.fori_loop(0, (i * tq) // (SB_UNROLL * nk), body, 0)
    o_ref[...] = acc_ref[...].astype(o_ref.dtype)


def _sb_prompt(proj, bias, nb, t, buf, col0, tq=2048):
    assert t % tq == 0 and tq % (SB_UNROLL * SB_KEYS) == 0 and col0 % SB_HEAD_DIM == 0
    nq = t // tq
    hb = SB_HEADS
    cb0 = col0 // SB_HEAD_DIM
    return pl.pallas_call(
        functools.partial(_sb_prompt_kernel, tq=tq, scale=SB_HEAD_DIM ** -0.5),
        grid=(nb, hb, nq),
        in_specs=[pl.BlockSpec(memory_space=pltpu.SMEM),
                  pl.BlockSpec((tq, SB_HEAD_DIM), lambda b, h, i: (b * nq + i, h)),
                  pl.BlockSpec((t, SB_HEAD_DIM), lambda b, h, i: (b, hb + h)),
                  pl.BlockSpec((t, SB_HEAD_DIM), lambda b, h, i: (b, 2 * hb + h)),
                  pl.BlockSpec(memory_space=pl.ANY)],
        out_specs=pl.BlockSpec((tq, SB_HEAD_DIM), lambda b, h, i: (b * nq + i, cb0 + h)),
        out_shape=jax.ShapeDtypeStruct(buf.shape, buf.dtype),
        input_output_aliases={4: 0},
        scratch_shapes=[pltpu.VMEM((tq, SB_HEAD_DIM), F32), pltpu.VMEM((tq, LANES), F32)],
        compiler_params=_cparams(("arbitrary", "arbitrary", "arbitrary")),
    )(bias, proj, proj, proj, buf)


def _sb_sample_kernel(pt_ref, qbd_ref, bias_ref, kown_ref, vown_ref, *refs, pp, nq):
    del pt_ref
    kp = refs[:pp]
    vp = refs[pp:2 * pp]
    o_ref = refs[2 * pp]
    acc_ref, run_ref = refs[2 * pp + 1:]
    s = pl.program_id(1)
    n = PAGE_SIZE
    rows_c = 2 * nq
    earlier = (_iota((n, n), 1) > _iota((n, n), 0)).astype(BF16)
    qbd = qbd_ref[...]
    bias = bias_ref[...]

    def head_rows(ref, hh):
        return ref[pl.ds(hh, n, stride=SB_HEADS), :].astype(BF16)

    def scores(k_ref, mask):
        kcat = jnp.concatenate([head_rows(k_ref, hh) for hh in range(SB_HEADS)], axis=1)
        z = _dot(kcat, qbd) + bias
        sp = _softplus(z)
        lg = -sp if mask is None else jnp.where(mask, -sp, 0.0)
        return z - sp, _dot(earlier, lg.astype(BF16)), jnp.sum(lg, axis=0, keepdims=True)

    def add_values(v_ref, w, acc):
        wt = w.T.astype(BF16)
        return [acc[hh] + _dot(wt[(hh // 2) * rows_c:(hh // 2 + 1) * rows_c, :], head_rows(v_ref, hh))
                for hh in range(SB_HEADS)]

    @pl.when(s == 0)
    def _():
        mask = _iota((n, LANES), 0) < (_iota((n, LANES), 1) % nq)
        log_beta, local, total = scores(kown_ref, mask)
        w = jnp.where(mask, jnp.exp(log_beta + local), 0.0)
        acc = add_values(vown_ref, w, [jnp.zeros((rows_c, SB_HEAD_DIM), F32)] * SB_HEADS)
        for hh in range(SB_HEADS):
            acc_ref[hh] = acc[hh]
        run_ref[...] = total

    kcat = jnp.concatenate(
        [jnp.concatenate([head_rows(kp[c], hh) for hh in range(SB_HEADS)], axis=1) for c in range(pp)], axis=0)
    z = _dot(kcat, qbd) + bias
    sp = _softplus(z)
    lg = -sp
    lgb = lg.astype(BF16)
    local = jnp.concatenate([_dot(earlier, lgb[c * n:(c + 1) * n]) for c in range(pp)], axis=0)
    run = run_ref[...]
    runs = []
    for c in range(pp):
        runs.append(jnp.broadcast_to(run, (n, LANES)))
        run = run + jnp.sum(lg[c * n:(c + 1) * n], axis=0, keepdims=True)
    run_ref[...] = run
    w = jnp.exp(z - sp + local + jnp.concatenate(runs, axis=0))
    wt = w.T.astype(BF16)
    for hh in range(SB_HEADS):
        vh = jnp.concatenate([head_rows(vp[c], hh) for c in range(pp)], axis=0)
        acc_ref[hh] += _dot(wt[(hh // 2) * rows_c:(hh // 2 + 1) * rows_c, :], vh)

    @pl.when(s == pl.num_programs(1) - 1)
    def _():
        for hh in range(SB_HEADS):
            e = hh % 2
            o_ref[:, hh * SB_HEAD_DIM:(hh + 1) * SB_HEAD_DIM] = acc_ref[hh, e * nq:(e + 1) * nq, :]


def _sb_sample(q, k_own, v_own, bias, cache_k, cache_v, page_table, layer, pp=16):
    db, nq, _ = q.shape
    assert 2 * nq == 16 and SB_HEADS * nq <= LANES and cache_k.shape[2:] == (PAGE_SIZE, SB_HEADS, SB_HEAD_DIM)
    depth, n_pool = cache_k.shape[0], cache_k.shape[1]
    n_pages = page_table.shape[1]
    scale = SB_HEAD_DIM ** -0.5
    qh = (q * scale).reshape(db, nq, SB_HEADS, SB_HEAD_DIM)
    eye = jnp.eye(SB_HEADS, dtype=F32)
    qbd = jnp.einsum('bthd,hg->bhdgt', qh, eye).reshape(db, D_SB, SB_HEADS * nq)
    qbd = jnp.pad(qbd, ((0, 0), (0, 0), (0, LANES - SB_HEADS * nq))).astype(BF16)
    bias_l = jnp.pad(jnp.repeat(bias.astype(F32), nq), (0, LANES - SB_HEADS * nq)).reshape(1, LANES)
    pad = ((0, 0), (0, PAGE_SIZE - nq), (0, 0))
    page_rows = PAGE_SIZE * SB_HEADS
    kown = jnp.pad(k_own, pad).reshape(db * page_rows, SB_HEAD_DIM)
    vown = jnp.pad(v_own, pad).reshape(db * page_rows, SB_HEAD_DIM)
    ck = cache_k.reshape(depth * n_pool * page_rows, SB_HEAD_DIM)
    cv = cache_v.reshape(depth * n_pool * page_rows, SB_HEAD_DIM)
    base = layer * n_pool

    def page_map(c):
        return lambda b, s, pt: (base + pt[b, n_pages - 1 - (s * pp + c)], 0)

    page_spec = [pl.BlockSpec((page_rows, SB_HEAD_DIM), page_map(c)) for c in range(pp)]
    own_spec = pl.BlockSpec((page_rows, SB_HEAD_DIM), lambda b, s, pt: (b, 0))
    grid_spec = pltpu.PrefetchScalarGridSpec(
        num_scalar_prefetch=1,
        grid=(db, n_pages // pp),
        in_specs=[pl.BlockSpec((None, D_SB, LANES), lambda b, s, pt: (b, 0, 0)),
                  pl.BlockSpec((1, LANES), lambda b, s, pt: (0, 0)),
                  own_spec, own_spec] + page_spec + page_spec,
        out_specs=pl.BlockSpec((nq, D_SB), lambda b, s, pt: (b, 0)),
        scratch_shapes=[pltpu.VMEM((SB_HEADS, 2 * nq, SB_HEAD_DIM), F32),
                        pltpu.VMEM((1, LANES), F32)],
    )
    return pl.pallas_call(
        functools.partial(_sb_sample_kernel, pp=pp, nq=nq),
        grid_spec=grid_spec,
        out_shape=jax.ShapeDtypeStruct((db * nq, D_SB), F32),
        compiler_params=_cparams(("arbitrary", "arbitrary")),
    )(page_table, qbd, bias_l, kown, vown, *([ck] * pp), *([cv] * pp))


def _ssd_kernel(*refs, tb, has_init, has_buf):
    if has_buf:
        refs = refs[:-6] + refs[-5:]
    if has_init:
        (z0_ref, z1_ref, xa_ref, xb_ref, bc_ref, dt_ref, ctx_ref, h0_ref, cw_ref, cb_ref, dtb_ref, alog_ref,
         dsk_ref, ng_ref, spr_ref, sprt_ref, y_ref, hT_ref, xp_ref, hs_ref, yb_ref) = refs
    else:
        (z0_ref, z1_ref, xa_ref, xb_ref, bc_ref, dt_ref, ctx_ref, cw_ref, cb_ref, dtb_ref, alog_ref,
         dsk_ref, ng_ref, spr_ref, sprt_ref, y_ref, hT_ref, xp_ref, hs_ref, yb_ref) = refs
        h0_ref = None
    c = pl.program_id(1)
    L = TILE
    P2 = LANES
    n_pairs = D_SSM // P2

    @pl.when(c == 0)
    def _():
        xp_ref[0:8, :] = ctx_ref[...]
        if has_init:
            hs_ref[...] = h0_ref[...]
        else:
            hs_ref[...] = jnp.zeros_like(hs_ref)

    xp_ref[8:8 + tb, 0:1024] = xa_ref[...]
    xp_ref[8:8 + tb, 1024:2048] = xb_ref[...]
    xp_ref[8:8 + tb, 2048:3072] = bc_ref[...]
    if tb < L:
        xp_ref[8 + tb:8 + L, :] = jnp.zeros((L - tb, SSM_CONV_DIM), F32)

    xall = xp_ref[0:8 + L, :]
    conv = cb_ref[...] + cw_ref[SSM_CONV - 1:SSM_CONV, :] * xall[8:8 + L, :]
    for s in range(1, SSM_CONV):
        conv = conv + cw_ref[SSM_CONV - 1 - s:SSM_CONV - s, :] * pltpu.roll(xall, s, 0)[8:8 + L, :]
    xp_ref[0:8, :] = xp_ref[tb:tb + 8, :]
    xc = _silu(conv)

    row = _iota((L, LANES), 0)
    lane = _iota((L, LANES), 1)
    dt_raw = dt_ref[...]
    if tb < L:
        dt_raw = jnp.concatenate([dt_raw, jnp.zeros((L - tb, LANES), F32)], axis=0)
    valid = (lane < SSM_HEADS) & (row < tb)
    dtv = jnp.where(valid, _softplus(dt_raw + dtb_ref[...]), 0.0)
    a = dtv * (-jnp.exp(alog_ref[...]))
    lower = (_iota((L, L), 0) >= _iota((L, L), 1))
    lower_b = lower.astype(BF16)
    upper_b = (_iota((L, L), 0) <= _iota((L, L), 1)).astype(BF16)
    cum = _sel_dot_l(lower_b, a)
    cum_t = _sel_dot_r(a.T, upper_b)
    last = cum[L - 1:L, :]
    spread = spr_ref[...]
    spread_t = sprt_ref[...]
    dt_x = _sel_dot_r(dtv, spread)
    ecum_x = _sel_dot_r(jnp.exp(cum), spread)
    wdec_x = _sel_dot_r(jnp.exp(last - cum), spread)
    sdec = _sel_dot_l(spread_t, jnp.broadcast_to(jnp.exp(cum_t[:, L - 1:L]), (LANES, LANES)))

    first_half = lane < SSM_HEAD_DIM
    pairs_per_group = n_pairs // SSM_GROUPS
    x_all = xc[:, :D_SSM]
    xs_all = x_all * dt_x
    xs_bf = xs_all.astype(BF16)
    xw_all = xs_all * wdec_x
    h_old = hs_ref[...]
    h_old_b = h_old.astype(BF16)
    gcols = D_SSM // SSM_GROUPS
    for g in range(SSM_GROUPS):
        bg = xc[:, D_SSM + g * SSM_STATE:D_SSM + (g + 1) * SSM_STATE]
        cg = xc[:, D_SSM + (SSM_GROUPS + g) * SSM_STATE:D_SSM + (SSM_GROUPS + g + 1) * SSM_STATE]
        bg_b = bg.astype(BF16)
        cg_b = cg.astype(BF16)
        cb = _dot_nt(cg_b, bg_b)
        gs = slice(g * gcols, (g + 1) * gcols)
        inter = _dot_nt(cg_b, h_old_b[gs, :]) * ecum_x[:, gs]
        hs_ref[gs, :] = sdec[gs, :] * h_old[gs, :] + _dot(xw_all[:, gs].T.astype(BF16), bg_b)
        for r in range(pairs_per_group):
            p = g * pairs_per_group + r
            sl = slice(p * P2, (p + 1) * P2)
            xs_b = xs_bf[:, sl]
            ys = []
            for e in range(2):
                hh = 2 * p + e
                seg = jnp.broadcast_to(cum[:, hh:hh + 1], (L, L)) - jnp.broadcast_to(cum_t[hh:hh + 1, :], (L, L))
                dec = jnp.where(lower, jnp.exp(jnp.minimum(seg, 0.0)), 0.0)
                ys.append(_dot((cb * dec).astype(BF16), xs_b))
            yb_ref[:, sl] = jnp.where(first_half, ys[0], ys[1]) + inter[:, r * P2:(r + 1) * P2]

    zfull = jnp.concatenate([z0_ref[...], z1_ref[...]], axis=1)
    y = (yb_ref[0:tb, :] + dsk_ref[...] * x_all[0:tb, :]) * _silu(zfull)
    gw = D_SSM // SSM_GROUPS
    for g in range(SSM_GROUPS):
        yg = y[:, g * gw:(g + 1) * gw]
        yn = yg * lax.rsqrt(jnp.mean(yg * yg, axis=-1, keepdims=True) + EPS)
        y_ref[:, g * gw:(g + 1) * gw] = (yn * ng_ref[:, g * gw:(g + 1) * gw]).astype(y_ref.dtype)

    @pl.when(c == pl.num_programs(1) - 1)
    def _():
        hT_ref[...] = hs_ref[...]


def _ssd(proj, row0, nb, t, ctx8, h0, conv_w, conv_b, dt_bias, a_log, d_skip, norm_g, out_dtype, buf=None, col0=0):
    tb = min(t, TILE)
    nc = t // tb
    rb0 = row0 // tb
    has_init = h0 is not None

    def colblk(width, idx):
        return pl.BlockSpec((tb, width), lambda b, c: (rb0 + b * nc + c, idx))

    def full(shape):
        return pl.BlockSpec(shape, lambda b, c: (0,) * len(shape))

    in_specs = [colblk(1024, 3), colblk(1024, 4), colblk(1024, 5), colblk(1024, 6), colblk(1024, 7),
                colblk(LANES, 64),
                pl.BlockSpec((None, 8, SSM_CONV_DIM), lambda b, c: (b, 0, 0))]
    args = [proj] * 6 + [ctx8]
    if has_init:
        in_specs.append(pl.BlockSpec((None, D_SSM, SSM_STATE), lambda b, c: (b, 0, 0)))
        args.append(h0)
    padl = (0, LANES - SSM_HEADS)
    in_specs += [full((SSM_CONV, SSM_CONV_DIM)), full((1, SSM_CONV_DIM)), full((1, LANES)), full((1, LANES)),
                 full((1, D_SSM)), full((1, D_SSM)), full((LANES, D_SSM)), full((D_SSM, LANES))]
    spread = (jnp.arange(LANES)[:, None] == jnp.arange(D_SSM)[None, :] // SSM_HEAD_DIM).astype(BF16)
    args += [conv_w, conv_b.reshape(1, -1), jnp.pad(dt_bias, padl).reshape(1, LANES),
             jnp.pad(a_log, padl).reshape(1, LANES), jnp.repeat(d_skip, SSM_HEAD_DIM).reshape(1, D_SSM),
             norm_g.reshape(1, D_SSM), spread, spread.T]
    aliases = {}
    y_shape = jax.ShapeDtypeStruct((nb * t, D_SSM), out_dtype)
    if buf is not None:
        assert col0 % D_SSM == 0 and row0 == 0
        in_specs.append(pl.BlockSpec(memory_space=pl.ANY))
        args.append(buf)
        aliases = {len(args) - 1: 0}
        y_shape = jax.ShapeDtypeStruct(buf.shape, buf.dtype)
    cb0 = col0 // D_SSM
    return pl.pallas_call(
        functools.partial(_ssd_kernel, tb=tb, has_init=has_init, has_buf=buf is not None),
        grid=(nb, nc),
        in_specs=in_specs,
        out_specs=[pl.BlockSpec((tb, D_SSM), lambda b, c: (b * nc + c, cb0)),
                   pl.BlockSpec((None, D_SSM, SSM_STATE), lambda b, c: (b, 0, 0))],
        out_shape=[y_shape, jax.ShapeDtypeStruct((nb, D_SSM, SSM_STATE), F32)],
        input_output_aliases=aliases,
        scratch_shapes=[pltpu.VMEM((8 + TILE, SSM_CONV_DIM), F32),
                        pltpu.VMEM((D_SSM, SSM_STATE), F32),
                        pltpu.VMEM((TILE, D_SSM), F32)],
        compiler_params=_cparams(("arbitrary", "arbitrary")),
    )(*args)


def _gla_kernel(*refs, tb, has_init, has_buf):
    if has_buf:
        refs = refs[:-4] + refs[-3:]
    wide, tail_ref, refs = refs[:GLA_N_WIDE], refs[GLA_N_WIDE], refs[GLA_N_WIDE + 1:]
    if has_init:
        (s0_ref, gw_ref, gb_ref, ng_ref, o_ref, sT_ref, st_ref) = refs
    else:
        (gw_ref, gb_ref, ng_ref, o_ref, sT_ref, st_ref) = refs
        s0_ref = None
    c = pl.program_id(1)
    L = TILE
    tail = tail_ref[...]
    region = jnp.concatenate([w[...] for w in wide] + [tail], axis=1)
    region = pltpu.roll(region, region.shape[1] - GLA_SHIFT, 1)
    kp_, o0 = GLA_HEADS * GLA_K_DIM, 0
    q_in = region[:, o0:o0 + kp_]
    k_in = region[:, o0 + kp_:o0 + 2 * kp_]
    v_in = region[:, o0 + 2 * kp_:o0 + 2 * kp_ + D_GLA]
    r_in = region[:, o0 + 2 * kp_ + D_GLA:o0 + 2 * kp_ + 2 * D_GLA]
    tail_lane = _iota(tail.shape, 1)
    gate_in = jnp.where((tail_lane >= GLA_SHIFT) & (tail_lane < GLA_SHIFT + GLA_RANK), tail, 0.0)
    CH = min(GLA_CHUNK, tb)
    n_sub = max(tb // CH, 1)
    KP = GLA_HEADS * GLA_K_DIM

    @pl.when(c == 0)
    def _():
        if has_init:
            st_ref[...] = s0_ref[...]
        else:
            st_ref[...] = jnp.zeros_like(st_ref)

    def padrows(x):
        if tb < L:
            return jnp.concatenate([x, jnp.zeros((L - tb, x.shape[1]), F32)], axis=0)
        return x

    row = _iota((L, L), 0)
    col = _iota((L, L), 1)
    same_chunk = (row // CH) == (col // CH)
    lower = (row >= col) & same_chunk
    lower_b = lower.astype(BF16)
    rvalid = _iota((L, KP), 0) < tb
    glog = _dot(gate_in.astype(BF16), gw_ref[...].astype(BF16)) + gb_ref[...]
    glog = padrows(-_softplus(-glog) * (1.0 / GLA_TAU))
    glog = jnp.where(rvalid, glog, 0.0)
    q_all = padrows(q_in) * (GLA_K_DIM ** -0.5)
    k_all = jnp.where(rvalid, padrows(k_in), 0.0)
    v_all = padrows(v_in)
    lane = _iota((L, LANES), 1)
    rowl = _iota((L, LANES), 0)

    b_all = _sel_dot_l(lower_b, glog)
    qt_all = q_all * jnp.exp(b_all)
    kt_all = (k_all * jnp.exp(-b_all)).astype(BF16)
    lasts_all = [b_all[min((j + 1) * CH, L) - 1:min((j + 1) * CH, L), :] for j in range(n_sub)]
    chunk_of_row = _iota((L, KP), 0) // CH
    kw_all = [jnp.where(chunk_of_row == j, k_all * jnp.exp(lasts_all[j] - b_all), 0.0) for j in range(n_sub)]
    normed = []

    for p in range(KP // LANES):
        sl = slice(p * LANES, (p + 1) * LANES)
        qt = qt_all[:, sl]
        kt = kt_all[:, sl]
        lasts = [la[:, sl] for la in lasts_all]
        s_cur = st_ref[sl, :]
        s_list = [s_cur]
        kw_t = [kw[:, sl].T.astype(BF16) for kw in kw_all]
        heads_out = []
        for e in range(2):
            hh = 2 * p + e
            own = (lane // GLA_K_DIM) == e
            qm = jnp.where(own, qt, 0.0).astype(BF16)
            att = jnp.where(lower, _dot_nt(qm, kt), 0.0)
            v_h = v_all[:, hh * GLA_V_DIM:(hh + 1) * GLA_V_DIM].astype(BF16)
            heads_out.append((qm, _dot(att.astype(BF16), v_h), v_h))
        decs = []
        for j in range(n_sub):
            dcol = jnp.exp(lasts[j]).reshape(1, LANES)
            decs.append(jnp.broadcast_to(dcol, (LANES, LANES)).T)
        for j in range(n_sub):
            upd0 = _dot(kw_t[j], heads_out[0][2])
            upd1 = _dot(kw_t[j], heads_out[1][2])
            upd = jnp.where(_iota((LANES, GLA_V_DIM), 0) < GLA_K_DIM, upd0, upd1)
            s_list.append(decs[j] * s_list[j] + upd)
        st_ref[sl, :] = s_list[n_sub]
        for e in range(2):
            hh = 2 * p + e
            qm, o_h, _ = heads_out[e]
            inter = _dot(qm, s_list[0].astype(BF16))
            for j in range(1, n_sub):
                inter = jnp.where((rowl // CH) == j, _dot(qm, s_list[j].astype(BF16)), inter)
            o_h = (o_h + inter)[0:tb, :]
            normed.append(o_h * lax.rsqrt(jnp.mean(o_h * o_h, axis=-1, keepdims=True) + EPS))

    gain = jnp.concatenate([ng_ref[...]] * GLA_HEADS, axis=1)
    o_ref[...] = (jnp.concatenate(normed, axis=1) * gain * _silu(r_in)).astype(o_ref.dtype)

    @pl.when(c == pl.num_programs(1) - 1)
    def _():
        sT_ref[...] = st_ref[...]


GLA_COL0 = 3 * D_SB + D_SSM + SSM_CONV_DIM
GLA_SHIFT = SSM_HEADS
GLA_WIDE = 1024
GLA_N_WIDE = (2 * GLA_HEADS * GLA_K_DIM + 2 * D_GLA) // GLA_WIDE


def _gla(proj, row0, nb, t, s0, gk_w, gk_b, norm_g, out_dtype, buf=None, col0=0):
    tb = min(t, TILE)
    nc = t // tb
    rb0 = row0 // tb
    has_init = s0 is not None
    KP = GLA_HEADS * GLA_K_DIM
    assert GLA_COL0 % GLA_WIDE == 0 and (GLA_COL0 + GLA_N_WIDE * GLA_WIDE) % LANES == 0

    def colblk(width, idx):
        return pl.BlockSpec((tb, width), lambda b, c: (rb0 + b * nc + c, idx))

    def full(shape):
        return pl.BlockSpec(shape, lambda b, c: (0,) * len(shape))

    in_specs = [colblk(GLA_WIDE, GLA_COL0 // GLA_WIDE + i) for i in range(GLA_N_WIDE)]
    in_specs.append(colblk(LANES, (GLA_COL0 + GLA_N_WIDE * GLA_WIDE) // LANES))
    args = [proj] * (GLA_N_WIDE + 1)
    if has_init:
        in_specs.append(pl.BlockSpec((None, KP, GLA_V_DIM), lambda b, c: (b, 0, 0)))
        args.append(s0)
    in_specs += [full((LANES, KP)), full((1, KP)), full((1, GLA_V_DIM))]
    gw_rows = jnp.zeros((LANES, KP), F32).at[GLA_SHIFT:GLA_SHIFT + GLA_RANK].set(gk_w)
    args += [gw_rows, gk_b.reshape(1, KP), norm_g.reshape(1, GLA_V_DIM)]
    aliases = {}
    o_shape = jax.ShapeDtypeStruct((nb * t, D_GLA), out_dtype)
    if buf is not None:
        assert col0 % D_GLA == 0
        in_specs.append(pl.BlockSpec(memory_space=pl.ANY))
        args.append(buf)
        aliases = {len(args) - 1: 0}
        o_shape = jax.ShapeDtypeStruct(buf.shape, buf.dtype)
    cb0 = col0 // D_GLA
    return pl.pallas_call(
        functools.partial(_gla_kernel, tb=tb, has_init=has_init, has_buf=buf is not None),
        grid=(nb, nc),
        in_specs=in_specs,
        out_specs=[pl.BlockSpec((tb, D_GLA), lambda b, c: (b * nc + c, cb0)),
                   pl.BlockSpec((None, KP, GLA_V_DIM), lambda b, c: (b, 0, 0))],
        out_shape=[o_shape, jax.ShapeDtypeStruct((nb, KP, GLA_V_DIM), F32)],
        input_output_aliases=aliases,
        scratch_shapes=[pltpu.VMEM((KP, GLA_V_DIM), F32)],
        compiler_params=_cparams(("arbitrary", "arbitrary")),
    )(*args)


def _put_rows_kernel(*refs):
    parts, o_ref = refs[:-2], refs[-1]
    off = 0
    for p in parts:
        w = p.shape[1]
        o_ref[:, off:off + w] = p[...].astype(o_ref.dtype)
        off += w


def _put_rows(buf, parts, row0):
    rows = parts[0].shape[0]
    width = sum(p.shape[1] for p in parts)
    assert width == buf.shape[1] and row0 % rows == 0
    return pl.pallas_call(
        _put_rows_kernel,
        grid=(1,),
        in_specs=[pl.BlockSpec(p.shape, lambda i: (0, 0)) for p in parts] + [pl.BlockSpec(memory_space=pl.ANY)],
        out_specs=pl.BlockSpec((rows, width), lambda i: (row0 // rows, 0)),
        out_shape=jax.ShapeDtypeStruct(buf.shape, buf.dtype),
        input_output_aliases={len(parts): 0},
        compiler_params=_cparams(("arbitrary",)),
    )(*parts, buf)


def _head_rows_kernel(*refs, depth):
    ins, (ko_ref, vo_ref) = refs[:2 * depth], refs[2 * depth:]
    tr = ins[0].shape[0]
    for l in range(depth):
        @pl.when(pl.program_id(0) == l)
        def _(l=l):
            for src, dst in ((ins[2 * l], ko_ref), (ins[2 * l + 1], vo_ref)):
                for hh in range(SB_HEADS):
                    dst[pl.ds(hh, tr, stride=SB_HEADS), :] = src[:, hh * SB_HEAD_DIM:(hh + 1) * SB_HEAD_DIM]


def _head_rows(projs, rows, tr):
    depth = len(projs)
    nt = rows // tr
    in_specs, args = [], []
    for l, p in enumerate(projs):
        for cb in (1, 2):
            in_specs.append(pl.BlockSpec((tr, D_SB), lambda ll, i, l=l, cb=cb: (jnp.where(ll == l, i, 0), cb)))
            args.append(p)
    out_spec = pl.BlockSpec((tr * SB_HEADS, SB_HEAD_DIM), lambda ll, i: (ll * nt + i, 0))
    shape = jax.ShapeDtypeStruct((depth * rows * SB_HEADS, SB_HEAD_DIM), F32)
    return pl.pallas_call(
        functools.partial(_head_rows_kernel, depth=depth),
        grid=(depth, nt),
        in_specs=in_specs,
        out_specs=[out_spec, out_spec],
        out_shape=[shape, shape],
        compiler_params=_cparams(("arbitrary", "arbitrary")),
    )(*args)


def kernel(x_prompt, x_sample, cache_k, cache_v, page_table, state_conv, state_ssm, state_gla, norm1_g, w_in, sb_bias, conv_w, conv_b, dt_bias, a_log, d_skip, ssm_norm_g, gla_gk_w, gla_gk_b, gla_norm_g, w_out, norm2_g, w_gate, w_up, w_down, final_norm_g):
    bp, tp, d = x_prompt.shape
    bs, ts, _ = x_sample.shape
    depth = w_in.shape[0]
    mp = bp * tp
    ms = bs * ts
    m = mp + ms
    tm = _row_tile(m, MM_ROWS)
    tm_s = _row_tile(m, MM_ROWS_SMALL)
    d_ff = w_gate.shape[2]
    KP = GLA_HEADS * GLA_K_DIM
    off_xbc = 3 * D_SB + D_SSM
    n_tail = SSM_CONV - 1

    x = jnp.concatenate([x_prompt.reshape(mp, d), x_sample.reshape(ms, d)], axis=0)
    w_down_bf = _cast_bf16(w_down, _row_tile(d_ff, MM_ROWS_SMALL))
    w_in_nk = jnp.swapaxes(w_in, 1, 2)
    outs = {k: [] for k in ("cp", "sp", "gp", "ks", "vs", "cs", "ss", "gs")}
    projs = []

    for l in range(depth):
        xn = _rmsnorm(x, norm1_g[l], BF16, tm_s)
        proj = _matmul(xn, w_in_nk, l, None, tm, MM_COLS, w_is_nk=True)

        mixed = jnp.zeros((m, d), BF16)
        mixed = _sb_prompt(proj, sb_bias[l], bp, tp, mixed, D_SSM)
        ctx_p = jnp.zeros((bp, 8, SSM_CONV_DIM), F32)
        mixed, ssm_p = _ssd(proj, 0, bp, tp, ctx_p, None, conv_w[l], conv_b[l], dt_bias[l], a_log[l],
                            d_skip[l], ssm_norm_g[l], BF16, buf=mixed, col0=0)
        mixed, gla_p = _gla(proj, 0, bp, tp, None,
                            gla_gk_w[l], gla_gk_b[l], gla_norm_g[l], BF16, buf=mixed, col0=D_SSM + D_SB)

        q_s = proj[mp:, 0:D_SB].reshape(bs, ts, D_SB)
        k_s = proj[mp:, D_SB:2 * D_SB].reshape(bs, ts, D_SB)
        v_s = proj[mp:, 2 * D_SB:3 * D_SB].reshape(bs, ts, D_SB)
        o_sb_s = _sb_sample(q_s, k_s, v_s, sb_bias[l], cache_k, cache_v, page_table, l)
        ctx_s = jnp.pad(state_conv[l], ((0, 0), (8 - (SSM_CONV - 1), 0), (0, 0)))
        y_ssm_s, ssm_s = _ssd(proj, mp, bs, ts, ctx_s, state_ssm[l].reshape(bs, D_SSM, SSM_STATE),
                              conv_w[l], conv_b[l], dt_bias[l], a_log[l], d_skip[l], ssm_norm_g[l], F32)
        o_gl_s, gla_s = _gla(proj, mp, bs, ts, state_gla[l].reshape(bs, KP, GLA_V_DIM),
                             gla_gk_w[l], gla_gk_b[l], gla_norm_g[l], F32)

        mixed = _put_rows(mixed, [y_ssm_s, o_sb_s, o_gl_s], mp)
        x = _matmul(mixed, w_out, l, x, tm, MM_COLS,
                    k_order=((D_SB, D_SB + D_SSM), (0, D_SB), (D_SB + D_SSM, D_SB + D_SSM + D_GLA)))
        hn = _rmsnorm(x, norm2_g[l], BF16, tm_s)
        hmid = _gate_up(hn, w_gate, w_up, l, tm, MM_COLS // 2)
        x = _matmul_bf16w(hmid, w_down_bf, l, x, tm_s, MM_COLS)

        xbc_p = jnp.stack([proj[(b + 1) * tp - n_tail:(b + 1) * tp, off_xbc:off_xbc + SSM_CONV_DIM]
                           for b in range(bp)])
        xbc_s = jnp.concatenate([state_conv[l], proj[mp:, off_xbc:off_xbc + SSM_CONV_DIM].reshape(bs, ts, -1)], axis=1)
        projs.append(proj)
        outs["cp"].append(xbc_p)
        outs["sp"].append(ssm_p.reshape(bp, SSM_HEADS, SSM_HEAD_DIM, SSM_STATE))
        outs["gp"].append(gla_p.reshape(bp, GLA_HEADS, GLA_K_DIM, GLA_V_DIM))
        outs["ks"].append(k_s.reshape(bs, ts, SB_HEADS, SB_HEAD_DIM))
        outs["vs"].append(v_s.reshape(bs, ts, SB_HEADS, SB_HEAD_DIM))
        outs["cs"].append(xbc_s[:, xbc_s.shape[1] - n_tail:])
        outs["ss"].append(ssm_s.reshape(bs, SSM_HEADS, SSM_HEAD_DIM, SSM_STATE))
        outs["gs"].append(gla_s.reshape(bs, GLA_HEADS, GLA_K_DIM, GLA_V_DIM))

    y_p = _rmsnorm(x, final_norm_g, F32, _row_tile(mp, COPY_ROWS), rows=mp)
    y_s = _rmsnorm(x[mp:], final_norm_g, F32, ms)
    st = {k: jnp.stack(v) for k, v in outs.items()}
    kp, vp = _head_rows(projs, mp, _row_tile(mp, COPY_ROWS))
    return (y_p.reshape(bp, tp, d), y_s.reshape(bs, ts, d),
            kp.reshape(depth, bp, tp, SB_HEADS, SB_HEAD_DIM), vp.reshape(depth, bp, tp, SB_HEADS, SB_HEAD_DIM),
            st["cp"], st["sp"], st["gp"],
            st["ks"], st["vs"], st["cs"], st["ss"], st["gs"])
```
